```python
import math
import jax, jax.numpy as jnp
from jax import lax
import numpy as np

D_MODEL = 1024
BATCH = 4
SEQ = 8192
DEPTH = 1

CHUNK = 64
MIX_WIDTH = D_MODEL
GDN_HEADS = 4
GDN_DK = D_MODEL // 8
GDN_DV = D_MODEL // 8
GDN_WIDTH = GDN_HEADS * GDN_DV
CONV_K = 4
DIFF_HEADS = 4
DIFF_D = D_MODEL // 16
DIFF_WIDTH = DIFF_HEADS * 2 * DIFF_D
ROPE_THETA = 500000.0
ROT_DIM = DIFF_D // 4
Q_BLOCK = 128
NORM_EPS = 1e-6
SPLIT_SIZES = (GDN_HEADS * GDN_DK, GDN_HEADS * GDN_DK, GDN_WIDTH,
               GDN_HEADS, GDN_HEADS, GDN_WIDTH,
               DIFF_WIDTH, DIFF_WIDTH, DIFF_WIDTH, DIFF_WIDTH)
IN_WIDTH = sum(SPLIT_SIZES)

kernel_name = 'hybrid_gdn_diffattn_parallel_heads'


def rms_norm(x, g):
    xf = x.astype(jnp.float32)
    y = xf * lax.rsqrt(jnp.mean(xf * xf, axis=-1, keepdims=True) + NORM_EPS)
    return (y * g.astype(jnp.float32)).astype(x.dtype)


def l2_norm(x):
    return x * lax.rsqrt(jnp.sum(x * x, axis=-1, keepdims=True) + NORM_EPS)


def causal_dwconv(u, w):
    return lax.conv_general_dilated(u, w[:, None, :].astype(u.dtype), window_strides=(1,),
                                    padding=[(CONV_K - 1, 0)],
                                    dimension_numbers=('NWC', 'WIO', 'NWC'),
                                    feature_group_count=u.shape[-1])


def gated_delta_rule_chunked(q, k, v, beta, g):
    b, s, h, dk = q.shape
    dv = v.shape[-1]
    n = s // CHUNK

    def to_chunks(t):
        return t.reshape(b, n, CHUNK, h, -1).transpose(0, 3, 1, 2, 4)

    q, k, v = to_chunks(q), to_chunks(k), to_chunks(v)
    beta = beta.reshape(b, n, CHUNK, h).transpose(0, 3, 1, 2)
    gc = jnp.cumsum(g.reshape(b, n, CHUNK, h).transpose(0, 3, 1, 2), axis=-1)
    incl = jnp.tril(jnp.ones((CHUNK, CHUNK), dtype=bool))
    strict = jnp.tril(jnp.ones((CHUNK, CHUNK), dtype=bool), k=-1)
    decay = jnp.exp(jnp.where(incl, gc[..., :, None] - gc[..., None, :], -jnp.inf))
    kb = k * beta[..., None]
    vb = v * beta[..., None]
    m = jnp.where(strict, jnp.einsum('bhnid,bhnjd->bhnij', kb, k) * decay, 0.0)
    lhs = m + jnp.eye(CHUNK, dtype=m.dtype)
    rhs = jnp.concatenate([vb, kb * jnp.exp(gc)[..., None]], axis=-1)
    sol = lax.linalg.triangular_solve(lhs, rhs, left_side=True, lower=True, unit_diagonal=True)
    u, w = sol[..., :dv], sol[..., dv:]
    a_intra = jnp.einsum('bhnid,bhnjd->bhnij', q, k) * decay
    q_dec = q * jnp.exp(gc)[..., None]
    k_dec = k * jnp.exp(gc[..., -1:] - gc)[..., None]
    g_last = jnp.exp(gc[..., -1])
    xs = tuple(jnp.moveaxis(t, 2, 0) for t in (w, u, q_dec, k_dec, a_intra, g_last))

    def step(state, inp):
        w_n, u_n, qd_n, kd_n, a_n, gl_n = inp
        v_new = u_n - jnp.einsum('bhcd,bhde->bhce', w_n, state)
        o = jnp.einsum('bhcd,bhde->bhce', qd_n, state) + jnp.einsum('bhcs,bhse->bhce', a_n, v_new)
        state = state * gl_n[..., None, None] + jnp.einsum('bhcd,bhce->bhde', kd_n, v_new)
        return state, o

    state0 = jnp.zeros((b, h, dk, dv), jnp.float32)
    _, o = lax.scan(step, state0, xs)
    return o.transpose(1, 0, 3, 2, 4).reshape(b, s, h, dv)


def gdn_branch(q, k, v, beta_pre, decay_pre, conv_w, a_log, dt_bias, out_g):
    b, s, _ = q.shape
    qkv = jax.nn.silu(causal_dwconv(jnp.concatenate([q, k, v], axis=-1), conv_w)).astype(jnp.float32)
    nk = GDN_HEADS * GDN_DK
    qf = l2_norm(qkv[..., :nk].reshape(b, s, GDN_HEADS, GDN_DK)) * (GDN_DK ** -0.5)
    kf = l2_norm(qkv[..., nk:2 * nk].reshape(b, s, GDN_HEADS, GDN_DK))
    vf = qkv[..., 2 * nk:].reshape(b, s, GDN_HEADS, GDN_DV)
    beta = jax.nn.sigmoid(beta_pre.astype(jnp.float32))
    g = -jnp.exp(a_log.astype(jnp.float32)) * jax.nn.softplus(
        decay_pre.astype(jnp.float32) + dt_bias.astype(jnp.float32))
    o = gated_delta_rule_chunked(qf, kf, vf, beta, g)
    o = rms_norm(o, out_g)
    return o.reshape(b, s, GDN_WIDTH)


def apply_partial_rope(x, cos, sin):
    half = ROT_DIM // 2
    xr = x[..., :ROT_DIM].astype(jnp.float32)
    x1, x2 = xr[..., :half], xr[..., half:]
    rot = jnp.concatenate([x1 * cos - x2 * sin, x2 * cos + x1 * sin], axis=-1)
    return jnp.concatenate([rot.astype(x.dtype), x[..., ROT_DIM:]], axis=-1)


def diff_attention_branch(q, k, v, positions, q_norm_g, k_norm_g, lam_q1, lam_k1, lam_q2, lam_k2,
                          subln_g, lambda_init):
    b, s, _ = q.shape
    q = rms_norm(q.reshape(b, s, DIFF_HEADS, 2, DIFF_D), q_norm_g)
    k = rms_norm(k.reshape(b, s, DIFF_HEADS, 2, DIFF_D), k_norm_g)
    v = v.reshape(b, s, DIFF_HEADS, 2 * DIFF_D)
    inv_freq = ROPE_THETA ** (-jnp.arange(0, ROT_DIM, 2, dtype=jnp.float32) / ROT_DIM)
    ang = positions.astype(jnp.float32)[..., None] * inv_freq
    cos = jnp.cos(ang)[:, :, None, None, :]
    sin = jnp.sin(ang)[:, :, None, None, :]
    q = apply_partial_rope(q, cos, sin)
    k = apply_partial_rope(k, cos, sin)
    lam = (jnp.exp(jnp.sum(lam_q1.astype(jnp.float32) * lam_k1.astype(jnp.float32)))
           - jnp.exp(jnp.sum(lam_q2.astype(jnp.float32) * lam_k2.astype(jnp.float32)))
           + lambda_init)
    qh = q.transpose(0, 2, 3, 1, 4)
    kh = k.transpose(0, 2, 3, 1, 4)
    vh = v.transpose(0, 2, 1, 3).astype(jnp.float32)
    nq = s // Q_BLOCK
    qblocks = qh.reshape(b, DIFF_HEADS, 2, nq, Q_BLOCK, DIFF_D).transpose(3, 0, 1, 2, 4, 5)
    key_chunk = jnp.arange(s) // CHUNK
    scale = DIFF_D ** -0.5

    def block(args):
        qb, bi = args
        q_chunk = (bi * Q_BLOCK + jnp.arange(Q_BLOCK)) // CHUNK
        mask = key_chunk[None, :] <= q_chunk[:, None]
        sc = jnp.einsum('bhiqd,bhikd->bhiqk', qb, kh).astype(jnp.float32) * scale
        p = jax.nn.softmax(jnp.where(mask, sc, -jnp.inf), axis=-1)
        a = p[:, :, 0] - lam * p[:, :, 1]
        return jnp.einsum('bhqk,bhke->bhqe', a, vh)

    o = lax.map(block, (qblocks, jnp.arange(nq)))
    o = o.transpose(1, 0, 3, 2, 4).reshape(b, s, DIFF_HEADS, 2 * DIFF_D)
    o = rms_norm(o, subln_g) * (1.0 - lambda_init)
    return o.reshape(b, s, DIFF_WIDTH)


def hybrid_layer(x, c, positions, lambda_init, norm_g, w_ada, b_ada, w_in, conv_w, a_log, dt_bias,
                 gdn_norm_g, q_norm_g, k_norm_g, lam_q1, lam_k1, lam_q2, lam_k2, subln_g, w_out):
    mod = jax.nn.silu(c) @ w_ada + b_ada
    shift, scale, gate = jnp.split(mod, 3, axis=-1)
    h = rms_norm(x, norm_g) * (1.0 + scale[:, None, :]) + shift[:, None, :]
    proj = h @ w_in
    idx = [int(i) for i in np.cumsum(SPLIT_SIZES)[:-1]]
    a_q, a_k, a_v, a_beta, a_decay, a_gate, b_q, b_k, b_v, b_gate = jnp.split(proj, idx, axis=-1)
    o_a = gdn_branch(a_q, a_k, a_v, a_beta, a_decay, conv_w, a_log, dt_bias, gdn_norm_g)
    o_b = diff_attention_branch(b_q, b_k, b_v, positions, q_norm_g, k_norm_g,
                                lam_q1, lam_k1, lam_q2, lam_k2, subln_g, lambda_init)
    mixed = jnp.concatenate([o_a.astype(x.dtype) * jax.nn.silu(a_gate),
                             o_b.astype(x.dtype) * jax.nn.silu(b_gate)], axis=-1)
    return x + gate[:, None, :] * (mixed @ w_out)


def setup_inputs(seed: int = 0) -> dict:
    key = jax.random.key(seed)
    ks = jax.random.split(key, 20)
    f32 = jnp.float32
    L, D = DEPTH, D_MODEL
    x = jax.random.normal(ks[0], (BATCH, SEQ, D), f32)
    c = jax.random.normal(ks[1], (BATCH, D), f32)
    positions = jnp.broadcast_to(jnp.arange(SEQ, dtype=jnp.int32)[None, :], (BATCH, SEQ))
    norm_g = 1.0 + 0.01 * jax.random.normal(ks[2], (L, D), f32)
    w_ada = 0.5 * D ** -0.5 * jax.random.normal(ks[3], (L, D, 3 * D), f32)
    b_ada = 0.01 * jax.random.normal(ks[4], (L, 3 * D), f32)
    w_in = D ** -0.5 * jax.random.normal(ks[5], (L, D, IN_WIDTH), f32)
    conv_w = CONV_K ** -0.5 * jax.random.normal(ks[6], (L, CONV_K, 2 * GDN_HEADS * GDN_DK + GDN_WIDTH), f32)
    a_log = jnp.log(jax.random.uniform(ks[7], (L, GDN_HEADS), f32, 1.0, 16.0))
    dt = jnp.exp(jax.random.uniform(ks[8], (L, GDN_HEADS), f32, math.log(0.001), math.log(0.1)))
    dt_bias = dt + jnp.log(-jnp.expm1(-dt))
    gdn_norm_g = 1.0 + 0.01 * jax.random.normal(ks[9], (L, GDN_DV), f32)
    q_norm_g = 1.0 + 0.01 * jax.random.normal(ks[10], (L, DIFF_D), f32)
    k_norm_g = 1.0 + 0.01 * jax.random.normal(ks[11], (L, DIFF_D), f32)
    lambda_q1 = 0.1 * jax.random.normal(ks[12], (L, DIFF_D), f32)
    lambda_k1 = 0.1 * jax.random.normal(ks[13], (L, DIFF_D), f32)
    lambda_q2 = 0.1 * jax.random.normal(ks[14], (L, DIFF_D), f32)
    lambda_k2 = 0.1 * jax.random.normal(ks[15], (L, DIFF_D), f32)
    subln_g = 1.0 + 0.01 * jax.random.normal(ks[16], (L, 2 * DIFF_D), f32)
    w_out = MIX_WIDTH ** -0.5 * jax.random.normal(ks[17], (L, MIX_WIDTH, D), f32)
    return {'x': x, 'c': c, 'positions': positions, 'norm_g': norm_g, 'w_ada': w_ada,
            'b_ada': b_ada, 'w_in': w_in, 'conv_w': conv_w, 'a_log': a_log, 'dt_bias': dt_bias,
            'gdn_norm_g': gdn_norm_g, 'q_norm_g': q_norm_g, 'k_norm_g': k_norm_g,
            'lambda_q1': lambda_q1, 'lambda_k1': lambda_k1, 'lambda_q2': lambda_q2,
            'lambda_k2': lambda_k2, 'subln_g': subln_g, 'w_out': w_out}


def reference(x, c, positions, norm_g, w_ada, b_ada, w_in, conv_w, a_log, dt_bias, gdn_norm_g,
              q_norm_g, k_norm_g, lambda_q1, lambda_k1, lambda_q2, lambda_k2, subln_g, w_out):
    for l in range(DEPTH):
        lambda_init = 0.8 - 0.6 * math.exp(-0.3 * l)
        x = hybrid_layer(x, c, positions, lambda_init, norm_g[l], w_ada[l], b_ada[l], w_in[l],
                         conv_w[l], a_log[l], dt_bias[l], gdn_norm_g[l], q_norm_g[l], k_norm_g[l],
                         lambda_q1[l], lambda_k1[l], lambda_q2[l], lambda_k2[l], subln_g[l], w_out[l])
    return x
```

```python
import functools
import math

import jax
import jax.numpy as jnp
from jax import lax
from jax.experimental import pallas as pl
from jax.experimental.pallas import tpu as pltpu

F32 = jnp.float32
BF16 = jnp.bfloat16
HIGHEST = lax.Precision.HIGHEST

CHUNK = 64
GDN_HEADS = 4
GDN_DK = 128
GDN_DV = 128
CONV_K = 4
DIFF_HEADS = 4
DIFF_D = 64
ROT_DIM = 16
ROPE_THETA = 500000.0
NORM_EPS = 1e-6
LANES = 128
NEG_BIG = -1e30

GDN_QKV = 3 * GDN_HEADS * GDN_DK
HEADW = 512
OFF_GQKV = 0
OFF_AGATE = GDN_QKV
OFF_BQ = OFF_AGATE + HEADW
OFF_BK = OFF_BQ + HEADW
OFF_BV = OFF_BK + HEADW
OFF_BGATE = OFF_BV + HEADW
OFF_BD = OFF_BGATE + HEADW
W_COLS = OFF_BD + LANES

PROJ_TM = 512
GDN_TC = 256
ATT_TQ = 256
OUT_TM = 512
VMEM_LIMIT = 48 * 1024 * 1024


def _silu(x):
    return x / (1.0 + jnp.exp(-x))


def _dot(a, b, **kw):
    return jnp.dot(a, b, preferred_element_type=F32, **kw)


def _dot_nt(a, b, **kw):
    return lax.dot_general(a, b, (((1,), (1,)), ((), ())), preferred_element_type=F32, **kw)


def _dot_tn(a, b, **kw):
    return lax.dot_general(a, b, (((0,), (0,)), ((), ())), preferred_element_type=F32, **kw)


def _ada_kernel(c_ref, w_ref, b_ref, o_ref):
    o_ref[...] = _dot(_silu(c_ref[...]), w_ref[...], precision=HIGHEST) + b_ref[...]


def _ada(c_pad, w_ada, b_ada):
    rows, d = c_pad.shape
    n = w_ada.shape[1]
    tn = 512
    return pl.pallas_call(
        _ada_kernel,
        grid=(n // tn,),
        in_specs=[pl.BlockSpec((rows, d), lambda j: (0, 0)),
                  pl.BlockSpec((d, tn), lambda j: (0, j)),
                  pl.BlockSpec((1, tn), lambda j: (0, j))],
        out_specs=pl.BlockSpec((rows, tn), lambda j: (0, j)),
        out_shape=jax.ShapeDtypeStruct((rows, n), F32),
        name="ada",
    )(c_pad, w_ada, b_ada)


def _proj_kernel(x_ref, pos_ref, shift_ref, scale_ref, ng_ref, w_ref, convw_ref, alog_ref, dtb_ref,
                 qg_ref, kg_ref, lmat_ref, gmat_ref, freq_ref, sgn_ref,
                 gq_ref, gk_ref, gv_ref, ag_ref, bq_ref, bk_ref, bv_ref, bgate_ref, bg_ref, bgt_ref,
                 cbuf):
    tm = x_ref.shape[0]
    x = x_ref[...]
    ms = jnp.mean(x * x, axis=-1, keepdims=True)
    y = x * lax.rsqrt(ms + NORM_EPS) * ng_ref[...]
    h = (y * (1.0 + scale_ref[...]) + shift_ref[...]).astype(BF16)

    @pl.when(pl.program_id(1) == 0)
    def _():
        cbuf[0:8, :] = jnp.zeros((8, GDN_QKV), F32)

    for c0 in range(0, GDN_QKV, HEADW):
        cbuf[8:8 + tm, c0:c0 + HEADW] = _dot(h, w_ref[:, OFF_GQKV + c0:OFF_GQKV + c0 + HEADW])
    for g, out_ref in enumerate((gq_ref, gk_ref, gv_ref)):
        for hh in range(GDN_HEADS):
            cs = slice(g * HEADW + hh * LANES, g * HEADW + (hh + 1) * LANES)
            acc = convw_ref[3:4, cs] * cbuf[8:8 + tm, cs]
            for j in range(CONV_K - 1):
                acc = acc + convw_ref[j:j + 1, cs] * cbuf[5 + j:5 + j + tm, cs]
            a = _silu(acc)
            if g < 2:
                a = a * lax.rsqrt(jnp.sum(a * a, axis=-1, keepdims=True) + NORM_EPS)
                if g == 0:
                    a = a * (GDN_DK ** -0.5)
            out_ref[:, hh * LANES:(hh + 1) * LANES] = a.astype(BF16)
    cbuf[0:8, :] = cbuf[tm:tm + 8, :]

    ag_ref[...] = _silu(_dot(h, w_ref[:, OFF_AGATE:OFF_AGATE + HEADW])).astype(BF16)
    bgate_ref[...] = _silu(_dot(h, w_ref[:, OFF_BGATE:OFF_BGATE + HEADW])).astype(BF16)
    bv_ref[...] = _dot(h, w_ref[:, OFF_BV:OFF_BV + HEADW]).astype(BF16)

    ang = pos_ref[...].astype(F32) * freq_ref[...]
    cos_t = jnp.cos(ang)
    sin_t = jnp.sin(ang)
    sin_hi = sin_t * sgn_ref[0:1, :]
    sin_lo = sin_t * sgn_ref[1:2, :]
    for off, g_ref, out_ref, post in ((OFF_BQ, qg_ref, bq_ref, DIFF_D ** -0.5),
                                      (OFF_BK, kg_ref, bk_ref, None)):
        z = _dot(h, w_ref[:, off:off + HEADW])
        gm = _dot((z * z).astype(BF16), gmat_ref[...])
        yn = z * lax.rsqrt(gm + NORM_EPS) * g_ref[...]
        for hh in range(DIFF_HEADS):
            ys = yn[:, hh * LANES:(hh + 1) * LANES]
            r = (ys * cos_t + pltpu.roll(ys, LANES - ROT_DIM // 2, axis=1) * sin_hi
                 + pltpu.roll(ys, ROT_DIM // 2, axis=1) * sin_lo)
            if post is not None:
                r = r * post
            out_ref[:, hh * LANES:(hh + 1) * LANES] = r.astype(BF16)

    z = _dot(h, w_ref[:, OFF_BD:OFF_BD + LANES])
    lane = lax.broadcasted_iota(jnp.int32, (1, LANES), 1)
    beta = 1.0 / (1.0 + jnp.exp(-z))
    zz = z + dtb_ref[...]
    softplus = jnp.maximum(zz, 0.0) + jnp.log(1.0 + jnp.exp(-jnp.abs(zz)))
    is_g = (lane >= GDN_HEADS) & (lane < 2 * GDN_HEADS)
    g = jnp.where(is_g, -jnp.exp(alog_ref[...]) * softplus, 0.0)
    g_hi = g.astype(BF16)
    g_lo = (g - g_hi.astype(F32)).astype(BF16)
    gc = _dot(lmat_ref[...], g_hi) + _dot(lmat_ref[...], g_lo)
    bg = jnp.where(lane < GDN_HEADS, beta, gc)
    bg_ref[...] = bg
    bgt_ref[...] = bg.T[0:8, :]


def _proj(x, pos3, shift, scale, norm_g, w_all, conv_w, alog_row, dtb_row, qg_row, kg_row,
          lmat, gmat, freq_row, sgn_rows):
    b, s, d = x.shape
    tm = PROJ_TM
    row = lambda bi, si: (bi, si, 0)
    const2 = lambda bi, si: (0, 0)
    per_b = lambda bi, si: (bi, 0, 0)
    hw = jax.ShapeDtypeStruct((b, s, HEADW), BF16)
    out_shape = [hw] * 8 + [jax.ShapeDtypeStruct((b, s, LANES), F32),
                            jax.ShapeDtypeStruct((b, 8, s), F32)]
    hw_spec = pl.BlockSpec((None, tm, HEADW), row)
    out_specs = [hw_spec] * 8 + [pl.BlockSpec((None, tm, LANES), row),
                                 pl.BlockSpec((None, 8, tm), lambda bi, si: (bi, 0, si))]
    in_specs = [
        pl.BlockSpec((None, tm, d), row),
        pl.BlockSpec((None, tm, 1), row),
        pl.BlockSpec((None, 1, d), per_b),
        pl.BlockSpec((None, 1, d), per_b),
        pl.BlockSpec((1, d), const2),
        pl.BlockSpec((d, W_COLS), const2),
        pl.BlockSpec((CONV_K, GDN_QKV), const2),
        pl.BlockSpec((1, LANES), const2),
        pl.BlockSpec((1, LANES), const2),
        pl.BlockSpec((1, HEADW), const2),
        pl.BlockSpec((1, HEADW), const2),
        pl.BlockSpec((tm, tm), const2),
        pl.BlockSpec((HEADW, HEADW), const2),
        pl.BlockSpec((1, LANES), const2),
        pl.BlockSpec((2, LANES), const2),
    ]
    return pl.pallas_call(
        _proj_kernel,
        grid=(b, s // tm),
        in_specs=in_specs,
        out_specs=out_specs,
        out_shape=out_shape,
        scratch_shapes=[pltpu.VMEM((tm + 8, GDN_QKV), F32)],
        compiler_params=pltpu.CompilerParams(
            dimension_semantics=("arbitrary", "arbitrary"), vmem_limit_bytes=VMEM_LIMIT),
        name="proj",
    )(x, pos3, shift, scale, norm_g, w_all, conv_w, alog_row, dtb_row, qg_row, kg_row,
      lmat, gmat, freq_row, sgn_rows)


def _gdn_kernel(q_ref, k_ref, v_ref, gate_ref, bg_ref, bgt_ref, ng_ref, o_ref,
                state, u_s, w_s, qd_s, kd_s, a_s):
    tc = q_ref.shape[0]
    nc = tc // CHUNK

    @pl.when(pl.program_id(1) == 0)
    def _():
        state[...] = jnp.zeros(state.shape, F32)

    ii = lax.broadcasted_iota(jnp.int32, (CHUNK, CHUNK), 0)
    jj = lax.broadcasted_iota(jnp.int32, (CHUNK, CHUNK), 1)
    eye = (ii == jj).astype(F32)

    for hh in range(GDN_HEADS):
        cs = slice(hh * LANES, (hh + 1) * LANES)
        for c in range(nc):
            rs = slice(c * CHUNK, (c + 1) * CHUNK)
            q = q_ref[rs, cs]
            k = k_ref[rs, cs]
            v = v_ref[rs, cs]
            bcol = bg_ref[rs, hh:hh + 1]
            gcol = bg_ref[rs, GDN_HEADS + hh:GDN_HEADS + hh + 1]
            brow = bgt_ref[hh:hh + 1, rs]
            grow = bgt_ref[GDN_HEADS + hh:GDN_HEADS + hh + 1, rs]
            glast = bgt_ref[GDN_HEADS + hh:GDN_HEADS + hh + 1, (c + 1) * CHUNK - 1:(c + 1) * CHUNK]
            dec = jnp.where(ii >= jj, jnp.exp(jnp.minimum(gcol - grow, 0.0)), 0.0)
            kk = _dot_nt(k, k)
            qk = _dot_nt(q, k)
            xm = jnp.where(ii > jj, -(bcol * kk * dec), 0.0)
            t = eye + xm
            p = xm
            for _ in range(5):
                p = _dot(p, p, precision=HIGHEST)
                t = t + _dot(t, p, precision=HIGHEST)
            tb = t * brow
            u_s[hh, rs, :] = _dot(tb.astype(BF16), v)
            w_s[hh, rs, :] = _dot((tb * jnp.exp(grow)).astype(BF16), k)
            a_s[hh, rs, :] = jnp.where(ii >= jj, qk * dec, 0.0)
            qd_s[hh, rs, :] = q.astype(F32) * jnp.exp(gcol)
            kd_s[hh, rs, :] = k.astype(F32) * jnp.exp(glast - gcol)

    for c in range(nc):
        rs = slice(c * CHUNK, (c + 1) * CHUNK)
        for hh in range(GDN_HEADS):
            cs = slice(hh * LANES, (hh + 1) * LANES)
            st = state[hh]
            st_b = st.astype(BF16)
            glast = bgt_ref[GDN_HEADS + hh:GDN_HEADS + hh + 1, (c + 1) * CHUNK - 1:(c + 1) * CHUNK]
            vn = u_s[hh, rs, :] - _dot(w_s[hh, rs, :].astype(BF16), st_b)
            vn_b = vn.astype(BF16)
            o = _dot(qd_s[hh, rs, :].astype(BF16), st_b) + _dot(a_s[hh, rs, :].astype(BF16), vn_b)
            state[hh] = st * jnp.exp(glast) + _dot_tn(kd_s[hh, rs, :].astype(BF16), vn_b)
            on = o * lax.rsqrt(jnp.mean(o * o, axis=-1, keepdims=True) + NORM_EPS) * ng_ref[...]
            o_ref[rs, cs] = (on * gate_ref[rs, cs].astype(F32)).astype(BF16)


def _gdn(gq, gk, gv, agate, bg, bgt, ng_row):
    b, s, _ = gq.shape
    tc = GDN_TC
    row = lambda bi, si: (bi, si, 0)
    hw_spec = pl.BlockSpec((None, tc, HEADW), row)
    return pl.pallas_call(
        _gdn_kernel,
        grid=(b, s // tc),
        in_specs=[hw_spec, hw_spec, hw_spec, hw_spec,
                  pl.BlockSpec((None, tc, LANES), row),
                  pl.BlockSpec((None, 8, tc), lambda bi, si: (bi, 0, si)),
                  pl.BlockSpec((1, GDN_DV), lambda bi, si: (0, 0))],
        out_specs=hw_spec,
        out_shape=jax.ShapeDtypeStruct((b, s, HEADW), BF16),
        scratch_shapes=[pltpu.VMEM((GDN_HEADS, GDN_DK, GDN_DV), F32),
                        pltpu.VMEM((GDN_HEADS, tc, GDN_DV), F32),
                        pltpu.VMEM((GDN_HEADS, tc, GDN_DK), F32),
                        pltpu.VMEM((GDN_HEADS, tc, GDN_DK), F32),
                        pltpu.VMEM((GDN_HEADS, tc, GDN_DK), F32),
                        pltpu.VMEM((GDN_HEADS, tc, CHUNK), F32)],
        compiler_params=pltpu.CompilerParams(
            dimension_semantics=("arbitrary", "arbitrary"), vmem_limit_bytes=VMEM_LIMIT),
        name="gdn",
    )(gq, gk, gv, agate, bg, bgt, ng_row)


def _attn_kernel(q_ref, k_ref, v_ref, gate_ref, lamp_ref, sg_ref, o_ref, *, lambda_init):
    tq = q_ref.shape[0]
    tk = tq
    i = pl.program_id(2)
    q = q_ref[...]
    lane = lax.broadcasted_iota(jnp.int32, (1, 2 * DIFF_D), 1)
    lo = lane < DIFF_D

    def step(j, carry, diag):
        start = pl.multiple_of(j * tk, tk)
        kb = k_ref[pl.ds(start, tk), :]
        vb = v_ref[pl.ds(start, tk), :]
        zero = jnp.zeros_like(kb)
        new = []
        for sub in range(2):
            m, l, acc = carry[sub]
            ksub = jnp.where(lo if sub == 0 else jnp.logical_not(lo), kb, zero)
            s = _dot_nt(q, ksub)
            if diag:
                rr = lax.broadcasted_iota(jnp.int32, (tq, tk), 0) // CHUNK
                cc = lax.broadcasted_iota(jnp.int32, (tq, tk), 1) // CHUNK
                s = jnp.where(cc <= rr, s, NEG_BIG)
            m_new = jnp.maximum(m, jnp.max(s, axis=-1, keepdims=True))
            alpha = jnp.exp(m - m_new)
            p = jnp.exp(s - m_new)
            l = alpha * l + jnp.sum(p, axis=-1, keepdims=True)
            acc = alpha * acc + _dot(p.astype(BF16), vb)
            new.append((m_new, l, acc))
        return tuple(new)

    init = tuple((jnp.full((tq, 1), NEG_BIG, F32), jnp.zeros((tq, 1), F32),
                  jnp.zeros((tq, 2 * DIFF_D), F32)) for _ in range(2))
    carry = lax.fori_loop(0, i, lambda j, c: step(j, c, False), init)
    (_, l0, a0), (_, l1, a1) = step(i, carry, True)

    lp = lamp_ref[...]
    lam = (jnp.exp(jnp.sum(lp[0:1, :] * lp[1:2, :], axis=-1, keepdims=True))
           - jnp.exp(jnp.sum(lp[2:3, :] * lp[3:4, :], axis=-1, keepdims=True)) + lambda_init)
    o = a0 / l0 - lam * (a1 / l1)
    on = o * lax.rsqrt(jnp.mean(o * o, axis=-1, keepdims=True) + NORM_EPS) * sg_ref[...]
    on = on * (1.0 - lambda_init)
    o_ref[...] = (on * gate_ref[...].astype(F32)).astype(BF16)


def _attn(bq, bk, bv, bgate, lam_params, sg_row, lambda_init):
    b, s, _ = bq.shape
    tq = ATT_TQ
    w = 2 * DIFF_D
    qspec = pl.BlockSpec((None, tq, w), lambda bi, hi, qi: (bi, qi, hi))
    kvspec = pl.BlockSpec((None, s, w), lambda bi, hi, qi: (bi, 0, hi))
    return pl.pallas_call(
        functools.partial(_attn_kernel, lambda_init=lambda_init),
        grid=(b, DIFF_HEADS, s // tq),
        in_specs=[qspec, kvspec, kvspec, qspec,
                  pl.BlockSpec((8, LANES), lambda bi, hi, qi: (0, 0)),
                  pl.BlockSpec((1, w), lambda bi, hi, qi: (0, 0))],
        out_specs=qspec,
        out_shape=jax.ShapeDtypeStruct((b, s, HEADW), BF16),
        compiler_params=pltpu.CompilerParams(
            dimension_semantics=("arbitrary", "arbitrary", "arbitrary"),
            vmem_limit_bytes=VMEM_LIMIT),
        name="attn",
    )(bq, bk, bv, bgate, lam_params, sg_row)


def _out_kernel(x_ref, ma_ref, mb_ref, wa_ref, wb_ref, gate_ref, o_ref):
    y = _dot(ma_ref[...], wa_ref[...]) + _dot(mb_ref[...], wb_ref[...])
    o_ref[...] = x_ref[...] + gate_ref[...] * y


def _out(x, mixed_a, mixed_b, w_a, w_b, gate):
    b, s, d = x.shape
    tm = OUT_TM
    row = lambda bi, si: (bi, si, 0)
    const2 = lambda bi, si: (0, 0)
    return pl.pallas_call(
        _out_kernel,
        grid=(b, s // tm),
        in_specs=[pl.BlockSpec((None, tm, d), row),
                  pl.BlockSpec((None, tm, HEADW), row),
                  pl.BlockSpec((None, tm, HEADW), row),
                  pl.BlockSpec((HEADW, d), const2),
                  pl.BlockSpec((HEADW, d), const2),
                  pl.BlockSpec((None, 1, d), lambda bi, si: (bi, 0, 0))],
        out_specs=pl.BlockSpec((None, tm, d), row),
        out_shape=jax.ShapeDtypeStruct((b, s, d), F32),
        compiler_params=pltpu.CompilerParams(
            dimension_semantics=("arbitrary", "arbitrary"), vmem_limit_bytes=VMEM_LIMIT),
        name="out",
    )(x, mixed_a, mixed_b, w_a, w_b, gate)


def _constants():
    lane = jnp.arange(LANES)
    sub = lane % DIFF_D
    half = ROT_DIM // 2
    inv_freq = ROPE_THETA ** (-jnp.arange(0, ROT_DIM, 2, dtype=F32) / ROT_DIM)
    freq_row = jnp.where(sub < ROT_DIM, inv_freq[sub % half], 0.0).astype(F32)[None, :]
    sgn_rows = jnp.stack([jnp.where(sub < half, -1.0, 0.0),
                          jnp.where((sub >= half) & (sub < ROT_DIM), 1.0, 0.0)]).astype(F32)
    r = jnp.arange(PROJ_TM)
    lmat = ((r[:, None] // CHUNK == r[None, :] // CHUNK) & (r[None, :] <= r[:, None])).astype(BF16)
    g = jnp.arange(HEADW)
    gmat = jnp.where(g[:, None] // DIFF_D == g[None, :] // DIFF_D, 1.0 / DIFF_D, 0.0).astype(BF16)
    return freq_row, sgn_rows, lmat, gmat


def _pad_lanes(v, offset):
    return jnp.zeros((1, LANES), F32).at[0, offset:offset + v.shape[0]].set(v.astype(F32))


def _layer(x, c_pad, pos3, lambda_init, norm_g, w_ada, b_ada, w_in, conv_w, a_log, dt_bias, gdn_norm_g,
           q_norm_g, k_norm_g, lam_q1, lam_k1, lam_q2, lam_k2, subln_g, w_out):
    b, s, d = x.shape
    mod = _ada(c_pad, w_ada, b_ada[None, :])[:b]
    shift = mod[:, None, 0:d]
    scale = mod[:, None, d:2 * d]
    gate = mod[:, None, 2 * d:3 * d]

    nk = GDN_HEADS * GDN_DK
    o_beta = 2 * nk + GDN_HEADS * GDN_DV
    o_agate = o_beta + 2 * GDN_HEADS
    o_b = o_agate + GDN_HEADS * GDN_DV
    bd = jnp.pad(w_in[:, o_beta:o_agate], ((0, 0), (0, LANES - 2 * GDN_HEADS)))
    w_all = jnp.concatenate([
        w_in[:, 0:o_beta], w_in[:, o_agate:o_b],
        w_in[:, o_b:o_b + HEADW], w_in[:, o_b + HEADW:o_b + 2 * HEADW],
        w_in[:, o_b + 2 * HEADW:o_b + 3 * HEADW], w_in[:, o_b + 3 * HEADW:o_b + 4 * HEADW],
        bd], axis=1).astype(BF16)
    freq_row, sgn_rows, lmat, gmat = _constants()
    lam_params = jnp.zeros((8, LANES), F32)
    for r, v in enumerate((lam_q1, lam_k1, lam_q2, lam_k2)):
        lam_params = lam_params.at[r, 0:DIFF_D].set(v.astype(F32))

    gq, gk, gv, agate, bq, bk, bv, bgate, bg, bgt = _proj(
        x, pos3, shift, scale, norm_g[None, :], w_all, conv_w,
        _pad_lanes(a_log, GDN_HEADS), _pad_lanes(dt_bias, GDN_HEADS),
        jnp.tile(q_norm_g, 2 * DIFF_HEADS)[None, :], jnp.tile(k_norm_g, 2 * DIFF_HEADS)[None, :],
        lmat, gmat, freq_row, sgn_rows)
    mixed_a = _gdn(gq, gk, gv, agate, bg, bgt, gdn_norm_g[None, :])
    mixed_b = _attn(bq, bk, bv, bgate, lam_params, subln_g[None, :], lambda_init)
    w_o = w_out.astype(BF16)
    return _out(x, mixed_a, mixed_b, w_o[:HEADW], w_o[HEADW:], gate)


def kernel(x, c, positions, norm_g, w_ada, b_ada, w_in, conv_w, a_log, dt_bias, gdn_norm_g, q_norm_g,
           k_norm_g, lambda_q1, lambda_k1, lambda_q2, lambda_k2, subln_g, w_out):
    b = x.shape[0]
    c_pad = jnp.pad(c, ((0, 8 - b % 8 if b % 8 else 0), (0, 0)))
    pos3 = positions[:, :, None]
    for l in range(norm_g.shape[0]):
        lambda_init = 0.8 - 0.6 * math.exp(-0.3 * l)
        x = _layer(x, c_pad, pos3, lambda_init, norm_g[l], w_ada[l], b_ada[l], w_in[l], conv_w[l],
                   a_log[l], dt_bias[l], gdn_norm_g[l], q_norm_g[l], k_norm_g[l], lambda_q1[l],
                   lambda_k1[l], lambda_q2[l], lambda_k2[l], subln_g[l], w_out[l])
    return x
```

```python
import functools
import math

import jax
import jax.numpy as jnp
from jax import lax
from jax.experimental import pallas as pl
from jax.experimental.pallas import tpu as pltpu

F32 = jnp.float32
BF16 = jnp.bfloat16
HIGHEST = lax.Precision.HIGHEST

CHUNK = 64
GDN_HEADS = 4
GDN_DK = 128
GDN_DV = 128
CONV_K = 4
DIFF_HEADS = 4
DIFF_D = 64
ROT_DIM = 16
ROPE_THETA = 500000.0
NORM_EPS = 1e-6
LANES = 128
NEG_BIG = -1e30

GDN_QKV = 3 * GDN_HEADS * GDN_DK
HEADW = 512
OFF_GQKV = 0
OFF_AGATE = GDN_QKV
OFF_BQ = OFF_AGATE + HEADW
OFF_BK = OFF_BQ + HEADW
OFF_BV = OFF_BK + HEADW
OFF_BGATE = OFF_BV + HEADW
OFF_BD = OFF_BGATE + HEADW
W_COLS = OFF_BD + LANES

PROJ_TM = 512
GDN_TC = 256
ATT_TQ = 512
MAX_STATIC_SHIFT = 60.0
OUT_TM = 512
VMEM_LIMIT = 48 * 1024 * 1024


def _silu(x):
    return x / (1.0 + jnp.exp(-x))


def _dot(a, b, **kw):
    return jnp.dot(a, b, preferred_element_type=F32, **kw)


def _dot_nt(a, b, **kw):
    return lax.dot_general(a, b, (((1,), (1,)), ((), ())), preferred_element_type=F32, **kw)


def _dot_tn(a, b, **kw):
    return lax.dot_general(a, b, (((0,), (0,)), ((), ())), preferred_element_type=F32, **kw)


def _ada_kernel(c_ref, w_ref, b_ref, o_ref):
    o_ref[...] = _dot(_silu(c_ref[...]), w_ref[...], precision=HIGHEST) + b_ref[...]


def _ada(c_pad, w_ada, b_ada):
    rows, d = c_pad.shape
    n = w_ada.shape[1]
    tn = 512
    return pl.pallas_call(
        _ada_kernel,
        grid=(n // tn,),
        in_specs=[pl.BlockSpec((rows, d), lambda j: (0, 0)),
                  pl.BlockSpec((d, tn), lambda j: (0, j)),
                  pl.BlockSpec((1, tn), lambda j: (0, j))],
        out_specs=pl.BlockSpec((rows, tn), lambda j: (0, j)),
        out_shape=jax.ShapeDtypeStruct((rows, n), F32),
        name="ada",
    )(c_pad, w_ada, b_ada)


def _proj_kernel(x_ref, pos_ref, shift_ref, scale_ref, ng_ref, w_ref, convw_ref, alog_ref, dtb_ref,
                 qg_ref, kg_ref, lmat_ref, gmat_ref, freq_ref, sgn_ref,
                 gq_ref, gk_ref, gv_ref, ag_ref, bq_ref, bk_ref, bv_ref, bgate_ref, bg_ref, bgt_ref,
                 cbuf):
    tm = x_ref.shape[0]
    x = x_ref[...]
    ms = jnp.mean(x * x, axis=-1, keepdims=True)
    y = x * lax.rsqrt(ms + NORM_EPS) * ng_ref[...]
    h = (y * (1.0 + scale_ref[...]) + shift_ref[...]).astype(BF16)

    @pl.when(pl.program_id(1) == 0)
    def _():
        cbuf[0:8, :] = jnp.zeros((8, GDN_QKV), F32)

    for c0 in range(0, GDN_QKV, HEADW):
        cbuf[8:8 + tm, c0:c0 + HEADW] = _dot(h, w_ref[:, OFF_GQKV + c0:OFF_GQKV + c0 + HEADW])
    for g, out_ref in enumerate((gq_ref, gk_ref, gv_ref)):
        for hh in range(GDN_HEADS):
            cs = slice(g * HEADW + hh * LANES, g * HEADW + (hh + 1) * LANES)
            acc = convw_ref[3:4, cs] * cbuf[8:8 + tm, cs]
            for j in range(CONV_K - 1):
                acc = acc + convw_ref[j:j + 1, cs] * cbuf[5 + j:5 + j + tm, cs]
            a = _silu(acc)
            if g < 2:
                a = a * lax.rsqrt(jnp.sum(a * a, axis=-1, keepdims=True) + NORM_EPS)
                if g == 0:
                    a = a * (GDN_DK ** -0.5)
            out_ref[:, hh * LANES:(hh + 1) * LANES] = a.astype(BF16)
    cbuf[0:8, :] = cbuf[tm:tm + 8, :]

    ag_ref[...] = _silu(_dot(h, w_ref[:, OFF_AGATE:OFF_AGATE + HEADW])).astype(BF16)
    bgate_ref[...] = _silu(_dot(h, w_ref[:, OFF_BGATE:OFF_BGATE + HEADW])).astype(BF16)
    bv_ref[...] = _dot(h, w_ref[:, OFF_BV:OFF_BV + HEADW]).astype(BF16)

    ang = pos_ref[...].astype(F32) * freq_ref[...]
    cos_t = jnp.cos(ang)
    sin_t = jnp.sin(ang)
    sin_hi = sin_t * sgn_ref[0:1, :]
    sin_lo = sin_t * sgn_ref[1:2, :]
    for off, g_ref, out_ref, post in ((OFF_BQ, qg_ref, bq_ref, DIFF_D ** -0.5 * math.log2(math.e)),
                                      (OFF_BK, kg_ref, bk_ref, None)):
        z = _dot(h, w_ref[:, off:off + HEADW])
        gm = _dot((z * z).astype(BF16), gmat_ref[...])
        yn = z * lax.rsqrt(gm + NORM_EPS) * g_ref[...]
        for hh in range(DIFF_HEADS):
            ys = yn[:, hh * LANES:(hh + 1) * LANES]
            r = (ys * cos_t + pltpu.roll(ys, LANES - ROT_DIM // 2, axis=1) * sin_hi
                 + pltpu.roll(ys, ROT_DIM // 2, axis=1) * sin_lo)
            if post is not None:
                r = r * post
            out_ref[:, hh * LANES:(hh + 1) * LANES] = r.astype(BF16)

    z = _dot(h, w_ref[:, OFF_BD:OFF_BD + LANES])
    lane = lax.broadcasted_iota(jnp.int32, (1, LANES), 1)
    beta = 1.0 / (1.0 + jnp.exp(-z))
    zz = z + dtb_ref[...]
    softplus = jnp.maximum(zz, 0.0) + jnp.log(1.0 + jnp.exp(-jnp.abs(zz)))
    is_g = (lane >= GDN_HEADS) & (lane < 2 * GDN_HEADS)
    g = jnp.where(is_g, -jnp.exp(alog_ref[...]) * softplus, 0.0)
    g_hi = g.astype(BF16)
    g_lo = (g - g_hi.astype(F32)).astype(BF16)
    gc = _dot(lmat_ref[...], g_hi) + _dot(lmat_ref[...], g_lo)
    bg = jnp.where(lane < GDN_HEADS, beta, gc)
    bg_ref[...] = bg
    bgt_ref[...] = bg.T[0:8, :]


def _proj(x, pos3, shift, scale, norm_g, w_all, conv_w, alog_row, dtb_row, qg_row, kg_row,
          lmat, gmat, freq_row, sgn_rows):
    b, s, d = x.shape
    tm = PROJ_TM
    row = lambda bi, si: (bi, si, 0)
    const2 = lambda bi, si: (0, 0)
    per_b = lambda bi, si: (bi, 0, 0)
    hw = jax.ShapeDtypeStruct((b, s, HEADW), BF16)
    out_shape = [hw] * 8 + [jax.ShapeDtypeStruct((b, s, LANES), F32),
                            jax.ShapeDtypeStruct((b, 8, s), F32)]
    hw_spec = pl.BlockSpec((None, tm, HEADW), row)
    out_specs = [hw_spec] * 8 + [pl.BlockSpec((None, tm, LANES), row),
                                 pl.BlockSpec((None, 8, tm), lambda bi, si: (bi, 0, si))]
    in_specs = [
        pl.BlockSpec((None, tm, d), row),
        pl.BlockSpec((None, tm, 1), row),
        pl.BlockSpec((None, 1, d), per_b),
        pl.BlockSpec((None, 1, d), per_b),
        pl.BlockSpec((1, d), const2),
        pl.BlockSpec((d, W_COLS), const2),
        pl.BlockSpec((CONV_K, GDN_QKV), const2),
        pl.BlockSpec((1, LANES), const2),
        pl.BlockSpec((1, LANES), const2),
        pl.BlockSpec((1, HEADW), const2),
        pl.BlockSpec((1, HEADW), const2),
        pl.BlockSpec((tm, tm), const2),
        pl.BlockSpec((HEADW, HEADW), const2),
        pl.BlockSpec((1, LANES), const2),
        pl.BlockSpec((2, LANES), const2),
    ]
    return pl.pallas_call(
        _proj_kernel,
        grid=(b, s // tm),
        in_specs=in_specs,
        out_specs=out_specs,
        out_shape=out_shape,
        scratch_shapes=[pltpu.VMEM((tm + 8, GDN_QKV), F32)],
        compiler_params=pltpu.CompilerParams(
            dimension_semantics=("arbitrary", "arbitrary"), vmem_limit_bytes=VMEM_LIMIT),
        name="proj",
    )(x, pos3, shift, scale, norm_g, w_all, conv_w, alog_row, dtb_row, qg_row, kg_row,
      lmat, gmat, freq_row, sgn_rows)


def _gdn_kernel(q_ref, k_ref, v_ref, gate_ref, bg_ref, bgt_ref, ng_ref, o_ref,
                state, u_s, w_s, qd_s, kd_s, a_s):
    tc = q_ref.shape[0]
    nc = tc // CHUNK

    @pl.when(pl.program_id(1) == 0)
    def _():
        state[...] = jnp.zeros(state.shape, F32)

    ii = lax.broadcasted_iota(jnp.int32, (CHUNK, CHUNK), 0)
    jj = lax.broadcasted_iota(jnp.int32, (CHUNK, CHUNK), 1)
    eye = (ii == jj).astype(F32)

    for hh in range(GDN_HEADS):
        cs = slice(hh * LANES, (hh + 1) * LANES)
        for c in range(nc):
            rs = slice(c * CHUNK, (c + 1) * CHUNK)
            q = q_ref[rs, cs]
            k = k_ref[rs, cs]
            v = v_ref[rs, cs]
            bcol = bg_ref[rs, hh:hh + 1]
            gcol = bg_ref[rs, GDN_HEADS + hh:GDN_HEADS + hh + 1]
            brow = bgt_ref[hh:hh + 1, rs]
            grow = bgt_ref[GDN_HEADS + hh:GDN_HEADS + hh + 1, rs]
            glast = bgt_ref[GDN_HEADS + hh:GDN_HEADS + hh + 1, (c + 1) * CHUNK - 1:(c + 1) * CHUNK]
            dec = jnp.where(ii >= jj, jnp.exp(jnp.minimum(gcol - grow, 0.0)), 0.0)
            kk = _dot_nt(k, k)
            qk = _dot_nt(q, k)
            xm = jnp.where(ii > jj, -(bcol * kk * dec), 0.0)
            t = eye + xm
            p = xm
            for _ in range(5):
                p = _dot(p, p, precision=HIGHEST)
                t = t + _dot(t, p, precision=HIGHEST)
            tb = t * brow
            u_s[hh, rs, :] = _dot(tb.astype(BF16), v)
            w_s[hh, rs, :] = _dot((tb * jnp.exp(grow)).astype(BF16), k)
            a_s[hh, rs, :] = jnp.where(ii >= jj, qk * dec, 0.0)
            qd_s[hh, rs, :] = q.astype(F32) * jnp.exp(gcol)
            kd_s[hh, rs, :] = k.astype(F32) * jnp.exp(glast - gcol)

    for c in range(nc):
        rs = slice(c * CHUNK, (c + 1) * CHUNK)
        for hh in range(GDN_HEADS):
            cs = slice(hh * LANES, (hh + 1) * LANES)
            st = state[hh]
            st_b = st.astype(BF16)
            glast = bgt_ref[GDN_HEADS + hh:GDN_HEADS + hh + 1, (c + 1) * CHUNK - 1:(c + 1) * CHUNK]
            vn = u_s[hh, rs, :] - _dot(w_s[hh, rs, :].astype(BF16), st_b)
            vn_b = vn.astype(BF16)
            o = _dot(qd_s[hh, rs, :].astype(BF16), st_b) + _dot(a_s[hh, rs, :].astype(BF16), vn_b)
            state[hh] = st * jnp.exp(glast) + _dot_tn(kd_s[hh, rs, :].astype(BF16), vn_b)
            on = o * lax.rsqrt(jnp.mean(o * o, axis=-1, keepdims=True) + NORM_EPS) * ng_ref[...]
            o_ref[rs, cs] = (on * gate_ref[rs, cs].astype(F32)).astype(BF16)


def _gdn(gq, gk, gv, agate, bg, bgt, ng_row):
    b, s, _ = gq.shape
    tc = GDN_TC
    row = lambda bi, si: (bi, si, 0)
    hw_spec = pl.BlockSpec((None, tc, HEADW), row)
    return pl.pallas_call(
        _gdn_kernel,
        grid=(b, s // tc),
        in_specs=[hw_spec, hw_spec, hw_spec, hw_spec,
                  pl.BlockSpec((None, tc, LANES), row),
                  pl.BlockSpec((None, 8, tc), lambda bi, si: (bi, 0, si)),
                  pl.BlockSpec((1, GDN_DV), lambda bi, si: (0, 0))],
        out_specs=hw_spec,
        out_shape=jax.ShapeDtypeStruct((b, s, HEADW), BF16),
        scratch_shapes=[pltpu.VMEM((GDN_HEADS, GDN_DK, GDN_DV), F32),
                        pltpu.VMEM((GDN_HEADS, tc, GDN_DV), F32),
                        pltpu.VMEM((GDN_HEADS, tc, GDN_DK), F32),
                        pltpu.VMEM((GDN_HEADS, tc, GDN_DK), F32),
                        pltpu.VMEM((GDN_HEADS, tc, GDN_DK), F32),
                        pltpu.VMEM((GDN_HEADS, tc, CHUNK), F32)],
        compiler_params=pltpu.CompilerParams(
            dimension_semantics=("arbitrary", "arbitrary"), vmem_limit_bytes=VMEM_LIMIT),
        name="gdn",
    )(gq, gk, gv, agate, bg, bgt, ng_row)


def _attn_kernel(q_ref, k_ref, v_ref, gate_ref, lamp_ref, sg_ref, gqk_ref, o_ref, vext, acc,
                 *, lambda_init):
    tq = q_ref.shape[0]
    tk = tq
    w = 2 * DIFF_D
    i = pl.program_id(2)

    @pl.when(i == 0)
    def _():
        vext[:, 0:w] = v_ref[...]
        vext[:, w:2 * w] = jnp.ones((vext.shape[0], w), BF16)

    q = q_ref[...]
    lane = lax.broadcasted_iota(jnp.int32, (1, w), 1)
    lo = lax.broadcasted_iota(jnp.int32, (tk, w), 1) < DIFF_D
    hi = jnp.logical_not(lo)
    zero = jnp.zeros((tk, w), BF16)

    def aug_row(at_lane, value):
        row = jnp.where(lane == at_lane, value, 0.0).astype(BF16)
        return jnp.broadcast_to(row, (tk, w))

    rr = lax.broadcasted_iota(jnp.int32, (tq, tk), 0) // CHUNK
    cc = lax.broadcasted_iota(jnp.int32, (tq, tk), 1) // CHUNK
    causal = cc <= rr

    gqk = gqk_ref[...]
    bound = (jnp.max(jnp.abs(gqk[0:1, :])) * jnp.max(jnp.abs(gqk[1:2, :]))
             * (DIFF_D * DIFF_D ** -0.5 * math.log2(math.e) * 1.01))
    acc[...] = jnp.zeros(acc.shape, F32)

    @pl.when(bound <= MAX_STATIC_SHIFT)
    def _():
        qs = (jnp.where(lo, q, aug_row(DIFF_D, 1.0)), jnp.where(hi, q, aug_row(0, 1.0)))
        k_aug = (aug_row(DIFF_D, -bound), aug_row(0, -bound))

        def step(j, diag):
            start = pl.multiple_of(j * tk, tk)
            kb = k_ref[pl.ds(start, tk), :]
            ve = vext[pl.ds(start, tk), :]
            ks = (jnp.where(lo, kb, k_aug[0]), jnp.where(hi, kb, k_aug[1]))
            for sub in range(2):
                p = jnp.exp2(_dot_nt(qs[sub], ks[sub]))
                if diag:
                    p = jnp.where(causal, p, 0.0)
                acc[sub] += _dot(p.astype(BF16), ve)

        def body(j, carry):
            step(j, False)
            return carry

        lax.fori_loop(0, i, body, 0)
        step(i, True)

    @pl.when(bound > MAX_STATIC_SHIFT)
    def _():
        def step(j, carry, diag):
            start = pl.multiple_of(j * tk, tk)
            kb = k_ref[pl.ds(start, tk), :]
            ve = vext[pl.ds(start, tk), :]
            new = []
            for sub in range(2):
                m = carry[sub]
                s = _dot_nt(q, jnp.where(lo if sub == 0 else hi, kb, zero))
                if diag:
                    s = jnp.where(causal, s, NEG_BIG)
                m_new = jnp.maximum(m, jnp.max(s, axis=-1, keepdims=True))
                p = jnp.exp2(s - m_new)
                acc[sub] = jnp.exp2(m - m_new) * acc[sub] + _dot(p.astype(BF16), ve)
                new.append(m_new)
            return tuple(new)

        init = (jnp.full((tq, 1), NEG_BIG, F32), jnp.full((tq, 1), NEG_BIG, F32))
        carry = lax.fori_loop(0, i, lambda j, c: step(j, c, False), init)
        step(i, carry, True)

    lp = lamp_ref[...]
    lam = (jnp.exp(jnp.sum(lp[0:1, :] * lp[1:2, :], axis=-1, keepdims=True))
           - jnp.exp(jnp.sum(lp[2:3, :] * lp[3:4, :], axis=-1, keepdims=True)) + lambda_init)
    o = acc[0, :, 0:w] / acc[0, :, w:2 * w] - lam * (acc[1, :, 0:w] / acc[1, :, w:2 * w])
    on = o * lax.rsqrt(jnp.mean(o * o, axis=-1, keepdims=True) + NORM_EPS) * sg_ref[...]
    on = on * (1.0 - lambda_init)
    o_ref[...] = (on * gate_ref[...].astype(F32)).astype(BF16)


def _attn(bq, bk, bv, bgate, lam_params, sg_row, gqk_rows, lambda_init):
    b, s, _ = bq.shape
    tq = ATT_TQ
    w = 2 * DIFF_D
    qspec = pl.BlockSpec((None, tq, w), lambda bi, hi, qi: (bi, qi, hi))
    kvspec = pl.BlockSpec((None, s, w), lambda bi, hi, qi: (bi, 0, hi))
    return pl.pallas_call(
        functools.partial(_attn_kernel, lambda_init=lambda_init),
        grid=(b, DIFF_HEADS, s // tq),
        in_specs=[qspec, kvspec, kvspec, qspec,
                  pl.BlockSpec((8, LANES), lambda bi, hi, qi: (0, 0)),
                  pl.BlockSpec((1, w), lambda bi, hi, qi: (0, 0)),
                  pl.BlockSpec((2, LANES), lambda bi, hi, qi: (0, 0))],
        out_specs=qspec,
        out_shape=jax.ShapeDtypeStruct((b, s, HEADW), BF16),
        scratch_shapes=[pltpu.VMEM((s, 2 * w), BF16),
                        pltpu.VMEM((2, tq, 2 * w), F32)],
        compiler_params=pltpu.CompilerParams(
            dimension_semantics=("arbitrary", "arbitrary", "arbitrary"),
            vmem_limit_bytes=VMEM_LIMIT),
        name="attn",
    )(bq, bk, bv, bgate, lam_params, sg_row, gqk_rows)


def _out_kernel(x_ref, ma_ref, mb_ref, wa_ref, wb_ref, gate_ref, o_ref):
    y = _dot(ma_ref[...], wa_ref[...]) + _dot(mb_ref[...], wb_ref[...])
    o_ref[...] = x_ref[...] + gate_ref[...] * y


def _out(x, mixed_a, mixed_b, w_a, w_b, gate):
    b, s, d = x.shape
    tm = OUT_TM
    row = lambda bi, si: (bi, si, 0)
    const2 = lambda bi, si: (0, 0)
    return pl.pallas_call(
        _out_kernel,
        grid=(b, s // tm),
        in_specs=[pl.BlockSpec((None, tm, d), row),
                  pl.BlockSpec((None, tm, HEADW), row),
                  pl.BlockSpec((None, tm, HEADW), row),
                  pl.BlockSpec((HEADW, d), const2),
                  pl.BlockSpec((HEADW, d), const2),
                  pl.BlockSpec((None, 1, d), lambda bi, si: (bi, 0, 0))],
        out_specs=pl.BlockSpec((None, tm, d), row),
        out_shape=jax.ShapeDtypeStruct((b, s, d), F32),
        compiler_params=pltpu.CompilerParams(
            dimension_semantics=("arbitrary", "arbitrary"), vmem_limit_bytes=VMEM_LIMIT),
        name="out",
    )(x, mixed_a, mixed_b, w_a, w_b, gate)


def _constants():
    lane = jnp.arange(LANES)
    sub = lane % DIFF_D
    half = ROT_DIM // 2
    inv_freq = ROPE_THETA ** (-jnp.arange(0, ROT_DIM, 2, dtype=F32) / ROT_DIM)
    freq_row = jnp.where(sub < ROT_DIM, inv_freq[sub % half], 0.0).astype(F32)[None, :]
    sgn_rows = jnp.stack([jnp.where(sub < half, -1.0, 0.0),
                          jnp.where((sub >= half) & (sub < ROT_DIM), 1.0, 0.0)]).astype(F32)
    r = jnp.arange(PROJ_TM)
    lmat = ((r[:, None] // CHUNK == r[None, :] // CHUNK) & (r[None, :] <= r[:, None])).astype(BF16)
    g = jnp.arange(HEADW)
    gmat = jnp.where(g[:, None] // DIFF_D == g[None, :] // DIFF_D, 1.0 / DIFF_D, 0.0).astype(BF16)
    return freq_row, sgn_rows, lmat, gmat


def _pad_lanes(v, offset):
    return jnp.zeros((1, LANES), F32).at[0, offset:offset + v.shape[0]].set(v.astype(F32))


def _layer(x, c_pad, pos3, lambda_init, norm_g, w_ada, b_ada, w_in, conv_w, a_log, dt_bias, gdn_norm_g,
           q_norm_g, k_norm_g, lam_q1, lam_k1, lam_q2, lam_k2, subln_g, w_out):
    b, s, d = x.shape
    mod = _ada(c_pad, w_ada, b_ada[None, :])[:b]
    shift = mod[:, None, 0:d]
    scale = mod[:, None, d:2 * d]
    gate = mod[:, None, 2 * d:3 * d]

    nk = GDN_HEADS * GDN_DK
    o_beta = 2 * nk + GDN_HEADS * GDN_DV
    o_agate = o_beta + 2 * GDN_HEADS
    o_b = o_agate + GDN_HEADS * GDN_DV
    bd = jnp.pad(w_in[:, o_beta:o_agate], ((0, 0), (0, LANES - 2 * GDN_HEADS)))
    w_all = jnp.concatenate([
        w_in[:, 0:o_beta], w_in[:, o_agate:o_b],
        w_in[:, o_b:o_b + HEADW], w_in[:, o_b + HEADW:o_b + 2 * HEADW],
        w_in[:, o_b + 2 * HEADW:o_b + 3 * HEADW], w_in[:, o_b + 3 * HEADW:o_b + 4 * HEADW],
        bd], axis=1).astype(BF16)
    freq_row, sgn_rows, lmat, gmat = _constants()
    lam_params = jnp.zeros((8, LANES), F32)
    for r, v in enumerate((lam_q1, lam_k1, lam_q2, lam_k2)):
        lam_params = lam_params.at[r, 0:DIFF_D].set(v.astype(F32))

    gq, gk, gv, agate, bq, bk, bv, bgate, bg, bgt = _proj(
        x, pos3, shift, scale, norm_g[None, :], w_all, conv_w,
        _pad_lanes(a_log, GDN_HEADS), _pad_lanes(dt_bias, GDN_HEADS),
        jnp.tile(q_norm_g, 2 * DIFF_HEADS)[None, :], jnp.tile(k_norm_g, 2 * DIFF_HEADS)[None, :],
        lmat, gmat, freq_row, sgn_rows)
    mixed_a = _gdn(gq, gk, gv, agate, bg, bgt, gdn_norm_g[None, :])
    gqk_rows = jnp.concatenate([_pad_lanes(q_norm_g, 0), _pad_lanes(k_norm_g, 0)], axis=0)
    mixed_b = _attn(bq, bk, bv, bgate, lam_params, subln_g[None, :], gqk_rows, lambda_init)
    w_o = w_out.astype(BF16)
    return _out(x, mixed_a, mixed_b, w_o[:HEADW], w_o[HEADW:], gate)


def kernel(x, c, positions, norm_g, w_ada, b_ada, w_in, conv_w, a_log, dt_bias, gdn_norm_g, q_norm_g,
           k_norm_g, lambda_q1, lambda_k1, lambda_q2, lambda_k2, subln_g, w_out):
    b = x.shape[0]
    c_pad = jnp.pad(c, ((0, 8 - b % 8 if b % 8 else 0), (0, 0)))
    pos3 = positions[:, :, None]
    for l in range(norm_g.shape[0]):
        lambda_init = 0.8 - 0.6 * math.exp(-0.3 * l)
        x = _layer(x, c_pad, pos3, lambda_init, norm_g[l], w_ada[l], b_ada[l], w_in[l], conv_w[l],
                   a_log[l], dt_bias[l], gdn_norm_g[l], q_norm_g[l], k_norm_g[l], lambda_q1[l],
                   lambda_k1[l], lambda_q2[l], lambda_k2[l], subln_g[l], w_out[l])
    return x
```

```python
import functools
import math

import jax
import jax.numpy as jnp
from jax import lax
from jax.experimental import pallas as pl
from jax.experimental.pallas import tpu as pltpu

F32 = jnp.float32
BF16 = jnp.bfloat16
HIGHEST = lax.Precision.HIGHEST

CHUNK = 64
PAIR = 2 * CHUNK
GDN_HEADS = 4
GDN_DK = 128
GDN_DV = 128
CONV_K = 4
DIFF_HEADS = 4
DIFF_D = 64
ROT_DIM = 16
ROPE_THETA = 500000.0
NORM_EPS = 1e-6
LANES = 128
NEG_BIG = -1e30
MASK_BIG = 256.0

GDN_QKV = 3 * GDN_HEADS * GDN_DK
HEADW = 512
OFF_GQKV = 0
OFF_AGATE = GDN_QKV
OFF_BQ = OFF_AGATE + HEADW
OFF_BK = OFF_BQ + HEADW
OFF_BV = OFF_BK + HEADW
OFF_BGATE = OFF_BV + HEADW
OFF_BD = OFF_BGATE + HEADW
W_COLS = OFF_BD + LANES

PROJ_TM = 512
GDN_TC = 256
ATT_TQ = 512
MAX_STATIC_SHIFT = 60.0
OUT_TM = 512
VMEM_LIMIT = 48 * 1024 * 1024


def _silu(x):
    return x / (1.0 + jnp.exp(-x))


def _dot(a, b, **kw):
    return jnp.dot(a, b, preferred_element_type=F32, **kw)


def _dot_nt(a, b, **kw):
    return lax.dot_general(a, b, (((1,), (1,)), ((), ())), preferred_element_type=F32, **kw)


def _dot_tn(a, b, **kw):
    return lax.dot_general(a, b, (((0,), (0,)), ((), ())), preferred_element_type=F32, **kw)


def _ada_kernel(c_ref, w_ref, b_ref, o_ref):
    o_ref[...] = _dot(_silu(c_ref[...]), w_ref[...], precision=HIGHEST) + b_ref[...]


def _ada(c_pad, w_ada, b_ada):
    rows, d = c_pad.shape
    n = w_ada.shape[1]
    tn = 512
    return pl.pallas_call(
        _ada_kernel,
        grid=(n // tn,),
        in_specs=[pl.BlockSpec((rows, d), lambda j: (0, 0)),
                  pl.BlockSpec((d, tn), lambda j: (0, j)),
                  pl.BlockSpec((1, tn), lambda j: (0, j))],
        out_specs=pl.BlockSpec((rows, tn), lambda j: (0, j)),
        out_shape=jax.ShapeDtypeStruct((rows, n), F32),
        name="ada",
    )(c_pad, w_ada, b_ada)


def _proj_kernel(x_ref, pos_ref, shift_ref, scale_ref, ng_ref, w_ref, convw_ref, alog_ref, dtb_ref,
                 qg_ref, kg_ref, lmat_ref, gmat_ref, freq_ref, sgn_ref,
                 gq_ref, gk_ref, gv_ref, ag_ref, bq_ref, bk_ref, bv_ref, bgate_ref, bg_ref, bgt_ref,
                 cbuf):
    tm = x_ref.shape[0]
    x = x_ref[...]
    ms = jnp.mean(x * x, axis=-1, keepdims=True)
    y = x * lax.rsqrt(ms + NORM_EPS) * ng_ref[...]
    h = (y * (1.0 + scale_ref[...]) + shift_ref[...]).astype(BF16)

    @pl.when(pl.program_id(1) == 0)
    def _():
        cbuf[0:8, :] = jnp.zeros((8, GDN_QKV), F32)

    for c0 in range(0, GDN_QKV, HEADW):
        cbuf[8:8 + tm, c0:c0 + HEADW] = _dot(h, w_ref[:, OFF_GQKV + c0:OFF_GQKV + c0 + HEADW])
    for g, out_ref in enumerate((gq_ref, gk_ref, gv_ref)):
        for hh in range(GDN_HEADS):
            cs = slice(g * HEADW + hh * LANES, g * HEADW + (hh + 1) * LANES)
            acc = convw_ref[3:4, cs] * cbuf[8:8 + tm, cs]
            for j in range(CONV_K - 1):
                acc = acc + convw_ref[j:j + 1, cs] * cbuf[5 + j:5 + j + tm, cs]
            a = _silu(acc)
            if g < 2:
                a = a * lax.rsqrt(jnp.sum(a * a, axis=-1, keepdims=True) + NORM_EPS)
                if g == 0:
                    a = a * (GDN_DK ** -0.5)
            out_ref[:, hh * LANES:(hh + 1) * LANES] = a.astype(BF16)
    cbuf[0:8, :] = cbuf[tm:tm + 8, :]

    ag_ref[...] = _silu(_dot(h, w_ref[:, OFF_AGATE:OFF_AGATE + HEADW])).astype(BF16)
    bgate_ref[...] = _silu(_dot(h, w_ref[:, OFF_BGATE:OFF_BGATE + HEADW])).astype(BF16)
    bv_ref[...] = _dot(h, w_ref[:, OFF_BV:OFF_BV + HEADW]).astype(BF16)

    ang = pos_ref[...].astype(F32) * freq_ref[...]
    cos_t = jnp.cos(ang)
    sin_t = jnp.sin(ang)
    sin_hi = sin_t * sgn_ref[0:1, :]
    sin_lo = sin_t * sgn_ref[1:2, :]
    for off, g_ref, out_ref, post in ((OFF_BQ, qg_ref, bq_ref, DIFF_D ** -0.5 * math.log2(math.e)),
                                      (OFF_BK, kg_ref, bk_ref, None)):
        z = _dot(h, w_ref[:, off:off + HEADW])
        gm = _dot((z * z).astype(BF16), gmat_ref[...])
        yn = z * lax.rsqrt(gm + NORM_EPS) * g_ref[...]
        for hh in range(DIFF_HEADS):
            ys = yn[:, hh * LANES:(hh + 1) * LANES]
            r = (ys * cos_t + pltpu.roll(ys, LANES - ROT_DIM // 2, axis=1) * sin_hi
                 + pltpu.roll(ys, ROT_DIM // 2, axis=1) * sin_lo)
            if post is not None:
                r = r * post
            out_ref[:, hh * LANES:(hh + 1) * LANES] = r.astype(BF16)

    z = _dot(h, w_ref[:, OFF_BD:OFF_BD + LANES])
    lane = lax.broadcasted_iota(jnp.int32, (1, LANES), 1)
    beta = 1.0 / (1.0 + jnp.exp(-z))
    zz = z + dtb_ref[...]
    softplus = jnp.maximum(zz, 0.0) + jnp.log(1.0 + jnp.exp(-jnp.abs(zz)))
    is_g = (lane >= GDN_HEADS) & (lane < 2 * GDN_HEADS)
    g = jnp.where(is_g, -jnp.exp(alog_ref[...]) * softplus, 0.0)
    g_hi = g.astype(BF16)
    g_lo = (g - g_hi.astype(F32)).astype(BF16)
    gc = _dot(lmat_ref[...], g_hi) + _dot(lmat_ref[...], g_lo)
    bg = jnp.where(lane < GDN_HEADS, beta, gc)
    bg_ref[...] = bg
    bgt_ref[...] = bg.T[0:8, :]


def _proj(x, pos3, shift, scale, norm_g, w_all, conv_w, alog_row, dtb_row, qg_row, kg_row,
          lmat, gmat, freq_row, sgn_rows):
    b, s, d = x.shape
    tm = PROJ_TM
    row = lambda bi, si: (bi, si, 0)
    const2 = lambda bi, si: (0, 0)
    per_b = lambda bi, si: (bi, 0, 0)
    hw = jax.ShapeDtypeStruct((b, s, HEADW), BF16)
    out_shape = [hw] * 8 + [jax.ShapeDtypeStruct((b, s, LANES), F32),
                            jax.ShapeDtypeStruct((b, 8, s), F32)]
    hw_spec = pl.BlockSpec((None, tm, HEADW), row)
    out_specs = [hw_spec] * 8 + [pl.BlockSpec((None, tm, LANES), row),
                                 pl.BlockSpec((None, 8, tm), lambda bi, si: (bi, 0, si))]
    in_specs = [
        pl.BlockSpec((None, tm, d), row),
        pl.BlockSpec((None, tm, 1), row),
        pl.BlockSpec((None, 1, d), per_b),
        pl.BlockSpec((None, 1, d), per_b),
        pl.BlockSpec((1, d), const2),
        pl.BlockSpec((d, W_COLS), const2),
        pl.BlockSpec((CONV_K, GDN_QKV), const2),
        pl.BlockSpec((1, LANES), const2),
        pl.BlockSpec((1, LANES), const2),
        pl.BlockSpec((1, HEADW), const2),
        pl.BlockSpec((1, HEADW), const2),
        pl.BlockSpec((tm, tm), const2),
        pl.BlockSpec((HEADW, HEADW), const2),
        pl.BlockSpec((1, LANES), const2),
        pl.BlockSpec((2, LANES), const2),
    ]
    return pl.pallas_call(
        _proj_kernel,
        grid=(b, s // tm),
        in_specs=in_specs,
        out_specs=out_specs,
        out_shape=out_shape,
        scratch_shapes=[pltpu.VMEM((tm + 8, GDN_QKV), F32)],
        compiler_params=pltpu.CompilerParams(
            dimension_semantics=("arbitrary", "arbitrary"), vmem_limit_bytes=VMEM_LIMIT),
        name="proj",
    )(x, pos3, shift, scale, norm_g, w_all, conv_w, alog_row, dtb_row, qg_row, kg_row,
      lmat, gmat, freq_row, sgn_rows)


def _split(a):
    hi = a.astype(BF16)
    return hi, (a - hi.astype(F32)).astype(BF16)


def _gdn_kernel(q_ref, k_ref, v_ref, gate_ref, bg_ref, bgt_ref, ng_ref, o_ref,
                state, wq_s, u_s, a_s, kdt_s, vn_s):
    tc = q_ref.shape[0]
    npair = tc // PAIR
    nc = tc // CHUNK

    @pl.when(pl.program_id(1) == 0)
    def _():
        state[...] = jnp.zeros(state.shape, F32)
        vn_s[...] = jnp.zeros(vn_s.shape, BF16)

    ii = lax.broadcasted_iota(jnp.int32, (CHUNK, PAIR), 0)
    lane_w = lax.broadcasted_iota(jnp.int32, (CHUNK, PAIR), 1)
    jj = lane_w % CHUNK
    left_w = lane_w < CHUNK
    left_p = lax.broadcasted_iota(jnp.int32, (PAIR, PAIR), 1) < CHUNK
    top_col = lax.broadcasted_iota(jnp.int32, (PAIR, 1), 0) < CHUNK
    eye = (ii == jj).astype(F32)
    lower = ii >= jj
    strict = ii > jj

    def widen(m):
        return jnp.where(left_w, m[0:CHUNK], m[CHUNK:PAIR])

    def bdiag(wd):
        z = jnp.zeros_like(wd)
        return jnp.concatenate([jnp.where(left_w, wd, z), jnp.where(left_w, z, wd)], axis=0)

    def mm(a, b):
        a_hi, a_lo = _split(a)
        b_hi, b_lo = _split(bdiag(b))
        return _dot(a_hi, b_hi) + _dot(a_hi, b_lo) + _dot(a_lo, b_hi)

    items = [(hh, p) for hh in range(GDN_HEADS) for p in range(npair)]
    pre = []
    for hh, p in items:
        cs = slice(hh * LANES, (hh + 1) * LANES)
        rs = slice(p * PAIR, (p + 1) * PAIR)
        q2, k2 = q_ref[rs, cs], k_ref[rs, cs]
        gcol2 = bg_ref[rs, GDN_HEADS + hh:GDN_HEADS + hh + 1]
        bcol_w = widen(jnp.broadcast_to(bg_ref[rs, hh:hh + 1], (PAIR, PAIR)))
        gcol_w = widen(jnp.broadcast_to(gcol2, (PAIR, PAIR)))
        grow = bgt_ref[GDN_HEADS + hh:GDN_HEADS + hh + 1, rs]
        dec = jnp.where(lower, jnp.exp(jnp.minimum(gcol_w - grow, 0.0)), 0.0)
        xm = jnp.where(strict, -(bcol_w * widen(_dot_nt(k2, k2)) * dec), 0.0)
        a_w = jnp.where(lower, widen(_dot_nt(q2, k2)) * dec, 0.0)
        a_s[hh, p] = bdiag(a_w).astype(BF16)
        g_end = jnp.where(top_col, grow[:, CHUNK - 1:CHUNK], grow[:, PAIR - 1:PAIR])
        qd = (q2.astype(F32) * jnp.exp(gcol2)).astype(BF16)
        kdt = (k2.astype(F32) * jnp.exp(g_end - gcol2)).T
        kdt_s[hh, 2 * p] = jnp.where(left_p, kdt, 0.0).astype(BF16)
        kdt_s[hh, 2 * p + 1] = jnp.where(left_p, 0.0, kdt).astype(BF16)
        pre.append((xm, qd, grow))
    ts = [eye + xm for xm, _, _ in pre]
    ps = [xm for xm, _, _ in pre]
    for _ in range(5):
        ps = [mm(pw, pw) for pw in ps]
        ts = [t + mm(t, pw) for t, pw in zip(ts, ps)]
    for (hh, p), t, (_, qd, grow) in zip(items, ts, pre):
        cs = slice(hh * LANES, (hh + 1) * LANES)
        rs = slice(p * PAIR, (p + 1) * PAIR)
        tb = t * bgt_ref[hh:hh + 1, rs]
        u_s[hh, rs, :] = _dot(bdiag(tb).astype(BF16), v_ref[rs, cs])
        w2 = _dot(bdiag(tb * jnp.exp(grow)).astype(BF16), k_ref[rs, cs]).astype(BF16)
        for e in range(2):
            es = slice(e * CHUNK, (e + 1) * CHUNK)
            wq_s[hh, 2 * p + e] = jnp.concatenate([w2[es], qd[es]], axis=0)

    for c in range(nc):
        p, e = divmod(c, 2)
        rs = slice(c * CHUNK, (c + 1) * CHUNK)
        es = slice(e * CHUNK, (e + 1) * CHUNK)
        sts = [state[hh] for hh in range(GDN_HEADS)]
        rr = [_dot(wq_s[hh, c], sts[hh].astype(BF16)) for hh in range(GDN_HEADS)]
        for hh in range(GDN_HEADS):
            vn_s[hh, es, :] = (u_s[hh, rs, :] - rr[hh][0:CHUNK]).astype(BF16)
        for hh in range(GDN_HEADS):
            cs = slice(hh * LANES, (hh + 1) * LANES)
            vnp = vn_s[hh]
            g_end = bgt_ref[GDN_HEADS + hh:GDN_HEADS + hh + 1, (c + 1) * CHUNK - 1:(c + 1) * CHUNK]
            state[hh] = sts[hh] * jnp.exp(g_end) + _dot(kdt_s[hh, c], vnp)
            o = rr[hh][CHUNK:PAIR] + _dot(a_s[hh, p, es, :], vnp)
            on = o * lax.rsqrt(jnp.mean(o * o, axis=-1, keepdims=True) + NORM_EPS) * ng_ref[...]
            o_ref[rs, cs] = (on * gate_ref[rs, cs].astype(F32)).astype(BF16)


def _gdn(gq, gk, gv, agate, bg, bgt, ng_row):
    b, s, _ = gq.shape
    tc = GDN_TC
    row = lambda bi, si: (bi, si, 0)
    hw_spec = pl.BlockSpec((None, tc, HEADW), row)
    return pl.pallas_call(
        _gdn_kernel,
        grid=(b, s // tc),
        in_specs=[hw_spec, hw_spec, hw_spec, hw_spec,
                  pl.BlockSpec((None, tc, LANES), row),
                  pl.BlockSpec((None, 8, tc), lambda bi, si: (bi, 0, si)),
                  pl.BlockSpec((1, GDN_DV), lambda bi, si: (0, 0))],
        out_specs=hw_spec,
        out_shape=jax.ShapeDtypeStruct((b, s, HEADW), BF16),
        scratch_shapes=[pltpu.VMEM((GDN_HEADS, GDN_DK, GDN_DV), F32),
                        pltpu.VMEM((GDN_HEADS, tc // CHUNK, PAIR, GDN_DK), BF16),
                        pltpu.VMEM((GDN_HEADS, tc, GDN_DV), F32),
                        pltpu.VMEM((GDN_HEADS, tc // PAIR, PAIR, PAIR), BF16),
                        pltpu.VMEM((GDN_HEADS, tc // CHUNK, GDN_DK, PAIR), BF16),
                        pltpu.VMEM((GDN_HEADS, PAIR, GDN_DV), BF16)],
        compiler_params=pltpu.CompilerParams(
            dimension_semantics=("arbitrary", "arbitrary"), vmem_limit_bytes=VMEM_LIMIT),
        name="gdn",
    )(gq, gk, gv, agate, bg, bgt, ng_row)


def _attn_kernel(q_ref, k_ref, v_ref, gate_ref, lamp_ref, sg_ref, gqk_ref, o_ref,
                 vext, acc, pbuf, qs_s, kaug_s, *, lambda_init, tq):
    s_len = q_ref.shape[0]
    nq = s_len // tq
    tk = tq
    nch = tq // CHUNK
    w = 2 * DIFF_D

    vext[:, 0:w] = v_ref[...]
    vext[:, w:2 * w] = jnp.ones((s_len, w), BF16)
    acc[...] = jnp.zeros(acc.shape, F32)

    lane = lax.broadcasted_iota(jnp.int32, (tk, w), 1)
    rchunk = lax.broadcasted_iota(jnp.int32, (tk, w), 0) // CHUNK
    lo = lane < DIFF_D
    hi = jnp.logical_not(lo)
    half = (lo, hi)

    gqk = gqk_ref[...]
    bound = (jnp.max(jnp.abs(gqk[0:1, :])) * jnp.max(jnp.abs(gqk[1:2, :]))
             * (DIFF_D * DIFF_D ** -0.5 * math.log2(math.e) * 1.01))

    lp = lamp_ref[...]
    lam = (jnp.exp(jnp.sum(lp[0:1, :] * lp[1:2, :], axis=-1, keepdims=True))
           - jnp.exp(jnp.sum(lp[2:3, :] * lp[3:4, :], axis=-1, keepdims=True)) + lambda_init)

    def rows_of(blk):
        return pl.ds(pl.multiple_of(blk * tq, tq), tq)

    def finalize(qi):
        o = acc[0, :, 0:w] / acc[0, :, w:2 * w] - lam * (acc[1, :, 0:w] / acc[1, :, w:2 * w])
        on = o * lax.rsqrt(jnp.mean(o * o, axis=-1, keepdims=True) + NORM_EPS) * sg_ref[...]
        on = on * (1.0 - lambda_init)
        o_ref[rows_of(qi), :] = (on * gate_ref[rows_of(qi), :].astype(F32)).astype(BF16)
        acc[...] = jnp.zeros(acc.shape, F32)

    @pl.when(bound <= MAX_STATIC_SHIFT)
    def _():
        q_aug = []
        for sub, base in enumerate((DIFF_D, 0)):
            mlane = lane - (base + 1)
            in_mask = (mlane >= 0) & (mlane < nch)
            q_aug.append(jnp.where((lane == base) | (in_mask & (rchunk == mlane)), 1.0, 0.0).astype(BF16))
            plain = jnp.where(lane == base, -bound, 0.0)
            kaug_s[0, sub] = plain.astype(BF16)
            kaug_s[1, sub] = jnp.where(in_mask & (rchunk > mlane), -MASK_BIG, plain).astype(BF16)

        def prep_q(qi, qslot):
            q = q_ref[rows_of(qi), :]
            for sub in range(2):
                qs_s[qslot, sub] = jnp.where(half[sub], q, q_aug[sub])

        def scores(qslot, j, is_diag, pslot):
            kb = k_ref[rows_of(j), :]
            for sub in range(2):
                ks = jnp.where(half[sub], kb, kaug_s[is_diag, sub])
                pbuf[pslot, sub] = jnp.exp2(_dot_nt(qs_s[qslot, sub], ks)).astype(BF16)

        def accumulate(j, pslot):
            ve = vext[rows_of(j), :]
            for sub in range(2):
                acc[sub] += _dot(pbuf[pslot, sub], ve)

        prep_q(0, 0)
        scores(0, 0, 1, 0)

        def outer(qi, n):
            qslot = qi & 1

            def inner(j, n):
                accumulate(j - 1, n & 1)
                scores(qslot, j, (j == qi).astype(jnp.int32), (n + 1) & 1)
                return n + 1

            n = lax.fori_loop(1, qi + 1, inner, n)
            nxt = jnp.minimum(qi + 1, nq - 1)
            prep_q(nxt, 1 - qslot)
            accumulate(qi, n & 1)
            scores(1 - qslot, 0, (nxt == 0).astype(jnp.int32), (n + 1) & 1)
            finalize(qi)
            return n + 1

        lax.fori_loop(0, nq, outer, 0)

    @pl.when(bound > MAX_STATIC_SHIFT)
    def _():
        zero = jnp.zeros((tk, w), BF16)
        rr = lax.broadcasted_iota(jnp.int32, (tq, tk), 0) // CHUNK
        cc = lax.broadcasted_iota(jnp.int32, (tq, tk), 1) // CHUNK
        causal = cc <= rr

        def outer(qi, carry):
            q = q_ref[rows_of(qi), :]

            def step(j, ms, diag):
                kb = k_ref[rows_of(j), :]
                ve = vext[rows_of(j), :]
                new = []
                for sub in range(2):
                    s = _dot_nt(q, jnp.where(half[sub], kb, zero))
                    if diag:
                        s = jnp.where(causal, s, NEG_BIG)
                    m_new = jnp.maximum(ms[sub], jnp.max(s, axis=-1, keepdims=True))
                    p = jnp.exp2(s - m_new)
                    acc[sub] = jnp.exp2(ms[sub] - m_new) * acc[sub] + _dot(p.astype(BF16), ve)
                    new.append(m_new)
                return tuple(new)

            init = (jnp.full((tq, 1), NEG_BIG, F32), jnp.full((tq, 1), NEG_BIG, F32))
            ms = lax.fori_loop(0, qi, lambda j, c: step(j, c, False), init)
            step(qi, ms, True)
            finalize(qi)
            return carry

        lax.fori_loop(0, nq, outer, 0)


def _attn(bq, bk, bv, bgate, lam_params, sg_row, gqk_rows, lambda_init):
    b, s, _ = bq.shape
    tq = ATT_TQ
    w = 2 * DIFF_D
    hspec = pl.BlockSpec((None, s, w), lambda bi, hi: (bi, 0, hi))
    const = lambda bi, hi: (0, 0)
    return pl.pallas_call(
        functools.partial(_attn_kernel, lambda_init=lambda_init, tq=tq),
        grid=(b, DIFF_HEADS),
        in_specs=[hspec, hspec, hspec, hspec,
                  pl.BlockSpec((8, LANES), const),
                  pl.BlockSpec((1, w), const),
                  pl.BlockSpec((2, LANES), const)],
        out_specs=hspec,
        out_shape=jax.ShapeDtypeStruct((b, s, HEADW), BF16),
        scratch_shapes=[pltpu.VMEM((s, 2 * w), BF16),
                        pltpu.VMEM((2, tq, 2 * w), F32),
                        pltpu.VMEM((2, 2, tq, tq), BF16),
                        pltpu.VMEM((2, 2, tq, w), BF16),
                        pltpu.VMEM((2, 2, tq, w), BF16)],
        compiler_params=pltpu.CompilerParams(
            dimension_semantics=("arbitrary", "arbitrary"), vmem_limit_bytes=VMEM_LIMIT),
        name="attn",
    )(bq, bk, bv, bgate, lam_params, sg_row, gqk_rows)


def _out_kernel(x_ref, ma_ref, mb_ref, wa_ref, wb_ref, gate_ref, o_ref):
    y = _dot(ma_ref[...], wa_ref[...]) + _dot(mb_ref[...], wb_ref[...])
    o_ref[...] = x_ref[...] + gate_ref[...] * y


def _out(x, mixed_a, mixed_b, w_a, w_b, gate):
    b, s, d = x.shape
    tm = OUT_TM
    row = lambda bi, si: (bi, si, 0)
    const2 = lambda bi, si: (0, 0)
    return pl.pallas_call(
        _out_kernel,
        grid=(b, s // tm),
        in_specs=[pl.BlockSpec((None, tm, d), row),
                  pl.BlockSpec((None, tm, HEADW), row),
                  pl.BlockSpec((None, tm, HEADW), row),
                  pl.BlockSpec((HEADW, d), const2),
                  pl.BlockSpec((HEADW, d), const2),
                  pl.BlockSpec((None, 1, d), lambda bi, si: (bi, 0, 0))],
        out_specs=pl.BlockSpec((None, tm, d), row),
        out_shape=jax.ShapeDtypeStruct((b, s, d), F32),
        compiler_params=pltpu.CompilerParams(
            dimension_semantics=("arbitrary", "arbitrary"), vmem_limit_bytes=VMEM_LIMIT),
        name="out",
    )(x, mixed_a, mixed_b, w_a, w_b, gate)


def _constants():
    lane = jnp.arange(LANES)
    sub = lane % DIFF_D
    half = ROT_DIM // 2
    inv_freq = ROPE_THETA ** (-jnp.arange(0, ROT_DIM, 2, dtype=F32) / ROT_DIM)
    freq_row = jnp.where(sub < ROT_DIM, inv_freq[sub % half], 0.0).astype(F32)[None, :]
    sgn_rows = jnp.stack([jnp.where(sub < half, -1.0, 0.0),
                          jnp.where((sub >= half) & (sub < ROT_DIM), 1.0, 0.0)]).astype(F32)
    r = jnp.arange(PROJ_TM)
    lmat = ((r[:, None] // CHUNK == r[None, :] // CHUNK) & (r[None, :] <= r[:, None])).astype(BF16)
    g = jnp.arange(HEADW)
    gmat = jnp.where(g[:, None] // DIFF_D == g[None, :] // DIFF_D, 1.0 / DIFF_D, 0.0).astype(BF16)
    return freq_row, sgn_rows, lmat, gmat


def _pad_lanes(v, offset):
    return jnp.zeros((1, LANES), F32).at[0, offset:offset + v.shape[0]].set(v.astype(F32))


def _layer(x, c_pad, pos3, lambda_init, norm_g, w_ada, b_ada, w_in, conv_w, a_log, dt_bias, gdn_norm_g,
           q_norm_g, k_norm_g, lam_q1, lam_k1, lam_q2, lam_k2, subln_g, w_out):
    b, s, d = x.shape
    mod = _ada(c_pad, w_ada, b_ada[None, :])[:b]
    shift = mod[:, None, 0:d]
    scale = mod[:, None, d:2 * d]
    gate = mod[:, None, 2 * d:3 * d]

    nk = GDN_HEADS * GDN_DK
    o_beta = 2 * nk + GDN_HEADS * GDN_DV
    o_agate = o_beta + 2 * GDN_HEADS
    o_b = o_agate + GDN_HEADS * GDN_DV
    bd = jnp.pad(w_in[:, o_beta:o_agate], ((0, 0), (0, LANES - 2 * GDN_HEADS)))
    w_all = jnp.concatenate([
        w_in[:, 0:o_beta], w_in[:, o_agate:o_b],
        w_in[:, o_b:o_b + HEADW], w_in[:, o_b + HEADW:o_b + 2 * HEADW],
        w_in[:, o_b + 2 * HEADW:o_b + 3 * HEADW], w_in[:, o_b + 3 * HEADW:o_b + 4 * HEADW],
        bd], axis=1).astype(BF16)
    freq_row, sgn_rows, lmat, gmat = _constants()
    lam_params = jnp.zeros((8, LANES), F32)
    for r, v in enumerate((lam_q1, lam_k1, lam_q2, lam_k2)):
        lam_params = lam_params.at[r, 0:DIFF_D].set(v.astype(F32))

    gq, gk, gv, agate, bq, bk, bv, bgate, bg, bgt = _proj(
        x, pos3, shift, scale, norm_g[None, :], w_all, conv_w,
        _pad_lanes(a_log, GDN_HEADS), _pad_lanes(dt_bias, GDN_HEADS),
        jnp.tile(q_norm_g, 2 * DIFF_HEADS)[None, :], jnp.tile(k_norm_g, 2 * DIFF_HEADS)[None, :],
        lmat, gmat, freq_row, sgn_rows)
    mixed_a = _gdn(gq, gk, gv, agate, bg, bgt, gdn_norm_g[None, :])
    gqk_rows = jnp.concatenate([_pad_lanes(q_norm_g, 0), _pad_lanes(k_norm_g, 0)], axis=0)
    mixed_b = _attn(bq, bk, bv, bgate, lam_params, subln_g[None, :], gqk_rows, lambda_init)
    w_o = w_out.astype(BF16)
    return _out(x, mixed_a, mixed_b, w_o[:HEADW], w_o[HEADW:], gate)


def kernel(x, c, positions, norm_g, w_ada, b_ada, w_in, conv_w, a_log, dt_bias, gdn_norm_g, q_norm_g,
           k_norm_g, lambda_q1, lambda_k1, lambda_q2, lambda_k2, subln_g, w_out):
    b = x.shape[0]
    c_pad = jnp.pad(c, ((0, 8 - b % 8 if b % 8 else 0), (0, 0)))
    pos3 = positions[:, :, None]
    for l in range(norm_g.shape[0]):
        lambda_init = 0.8 - 0.6 * math.exp(-0.3 * l)
        x = _layer(x, c_pad, pos3, lambda_init, norm_g[l], w_ada[l], b_ada[l], w_in[l], conv_w[l],
                   a_log[l], dt_bias[l], gdn_norm_g[l], q_norm_g[l], k_norm_g[l], lambda_q1[l],
                   lambda_k1[l], lambda_q2[l], lambda_k2[l], subln_g[l], w_out[l])
    return x
```

```python
import functools
import math

import jax
import jax.numpy as jnp
from jax import lax
from jax.experimental import pallas as pl
from jax.experimental.pallas import tpu as pltpu

F32 = jnp.float32
BF16 = jnp.bfloat16
HIGHEST = lax.Precision.HIGHEST

CHUNK = 64
PAIR = 2 * CHUNK
GDN_HEADS = 4
GDN_DK = 128
GDN_DV = 128
CONV_K = 4
DIFF_HEADS = 4
DIFF_D = 64
ROT_DIM = 16
ROPE_THETA = 500000.0
NORM_EPS = 1e-6
LANES = 128
NEG_BIG = -1e30
MASK_BIG = 256.0

GDN_QKV = 3 * GDN_HEADS * GDN_DK
HEADW = 512
OFF_GQKV = 0
OFF_AGATE = GDN_QKV
OFF_BQ = OFF_AGATE + HEADW
OFF_BK = OFF_BQ + HEADW
OFF_BV = OFF_BK + HEADW
OFF_BGATE = OFF_BV + HEADW
OFF_BD = OFF_BGATE + HEADW
W_COLS = OFF_BD + LANES

PROJ_TM = 512
GMAT_W = 256
GDN_TC = 128
ATT_TQ = 512
MAX_STATIC_SHIFT = 60.0
OUT_TM = 512
VMEM_LIMIT = 48 * 1024 * 1024


def _silu(x):
    h = 0.5 * x
    return h + h * jnp.tanh(h)


def _dot(a, b, **kw):
    return jnp.dot(a, b, preferred_element_type=F32, **kw)


def _dot_nt(a, b, **kw):
    return lax.dot_general(a, b, (((1,), (1,)), ((), ())), preferred_element_type=F32, **kw)


def _dot_tn(a, b, **kw):
    return lax.dot_general(a, b, (((0,), (0,)), ((), ())), preferred_element_type=F32, **kw)


def _ada_kernel(c_ref, w_ref, b_ref, o_ref):
    o_ref[...] = _dot(_silu(c_ref[...]), w_ref[...], precision=HIGHEST) + b_ref[...]


def _ada(c_pad, w_ada, b_ada):
    rows, d = c_pad.shape
    n = w_ada.shape[1]
    tn = 512
    return pl.pallas_call(
        _ada_kernel,
        grid=(n // tn,),
        in_specs=[pl.BlockSpec((rows, d), lambda j: (0, 0)),
                  pl.BlockSpec((d, tn), lambda j: (0, j)),
                  pl.BlockSpec((1, tn), lambda j: (0, j))],
        out_specs=pl.BlockSpec((rows, tn), lambda j: (0, j)),
        out_shape=jax.ShapeDtypeStruct((rows, n), F32),
        name="ada",
    )(c_pad, w_ada, b_ada)


def _proj_kernel(x_ref, pos_ref, shift_ref, scale_ref, ng_ref, w_ref, convw_ref, alog_ref, dtb_ref,
                 qg_ref, kg_ref, lmat_ref, gmat_ref, freq_ref, sgn_ref,
                 gq_ref, gk_ref, gv_ref, ag_ref, bq_ref, bk_ref, bv_ref, bgate_ref, bg_ref, bgt_ref,
                 cbuf, h_s):
    tm = x_ref.shape[0]
    x = x_ref[...]
    ms = jnp.mean(x * x, axis=-1, keepdims=True)
    y = x * lax.rsqrt(ms + NORM_EPS) * ng_ref[...]
    h_s[...] = (y * (1.0 + scale_ref[...]) + shift_ref[...]).astype(BF16)

    @pl.when(pl.program_id(1) == 0)
    def _():
        cbuf[0:8, :] = jnp.zeros((8, GDN_QKV), F32)

    for c0 in range(0, GDN_QKV, HEADW):
        cbuf[8:8 + tm, c0:c0 + HEADW] = _dot(h_s[...],w_ref[:, OFF_GQKV + c0:OFF_GQKV + c0 + HEADW])
    for g, out_ref in enumerate((gq_ref, gk_ref, gv_ref)):
        for hh in range(GDN_HEADS):
            cs = slice(g * HEADW + hh * LANES, g * HEADW + (hh + 1) * LANES)
            acc = convw_ref[3:4, cs] * cbuf[8:8 + tm, cs]
            for j in range(CONV_K - 1):
                acc = acc + convw_ref[j:j + 1, cs] * cbuf[5 + j:5 + j + tm, cs]
            a = _silu(acc)
            if g < 2:
                a = a * lax.rsqrt(jnp.sum(a * a, axis=-1, keepdims=True) + NORM_EPS)
                if g == 0:
                    a = a * (GDN_DK ** -0.5)
            out_ref[:, hh * LANES:(hh + 1) * LANES] = a.astype(BF16)
    cbuf[0:8, :] = cbuf[tm:tm + 8, :]

    ag_ref[...] = _silu(_dot(h_s[...],w_ref[:, OFF_AGATE:OFF_AGATE + HEADW])).astype(BF16)
    bgate_ref[...] = _silu(_dot(h_s[...],w_ref[:, OFF_BGATE:OFF_BGATE + HEADW])).astype(BF16)
    bv_ref[...] = _dot(h_s[...],w_ref[:, OFF_BV:OFF_BV + HEADW]).astype(BF16)

    ang = pos_ref[...].astype(F32) * freq_ref[...]
    cos_t = jnp.cos(ang)
    sin_t = jnp.sin(ang)
    sin_hi = sin_t * sgn_ref[0:1, :]
    sin_lo = sin_t * sgn_ref[1:2, :]
    for off, g_ref, out_ref, post in ((OFF_BQ, qg_ref, bq_ref, DIFF_D ** -0.5 * math.log2(math.e)),
                                      (OFF_BK, kg_ref, bk_ref, None)):
        z = _dot(h_s[...],w_ref[:, off:off + HEADW])
        zz = (z * z).astype(BF16)
        gw = gmat_ref.shape[0]
        gm = jnp.concatenate([_dot(zz[:, c0:c0 + gw], gmat_ref[...]) for c0 in range(0, HEADW, gw)], axis=1)
        yn = z * lax.rsqrt(gm + NORM_EPS) * g_ref[...]
        for hh in range(DIFF_HEADS):
            ys = yn[:, hh * LANES:(hh + 1) * LANES]
            r = (ys * cos_t + pltpu.roll(ys, LANES - ROT_DIM // 2, axis=1) * sin_hi
                 + pltpu.roll(ys, ROT_DIM // 2, axis=1) * sin_lo)
            if post is not None:
                r = r * post
            out_ref[:, hh * LANES:(hh + 1) * LANES] = r.astype(BF16)

    z = _dot(h_s[...],w_ref[:, OFF_BD:OFF_BD + LANES])
    lane = lax.broadcasted_iota(jnp.int32, (1, LANES), 1)
    beta = 1.0 / (1.0 + jnp.exp(-z))
    zz = z + dtb_ref[...]
    softplus = jnp.maximum(zz, 0.0) + jnp.log(1.0 + jnp.exp(-jnp.abs(zz)))
    is_g = (lane >= GDN_HEADS) & (lane < 2 * GDN_HEADS)
    g = jnp.where(is_g, -jnp.exp(alog_ref[...]) * softplus, 0.0)
    g_hi = g.astype(BF16)
    g_lo = (g - g_hi.astype(F32)).astype(BF16)
    gc = _dot(lmat_ref[...], g_hi) + _dot(lmat_ref[...], g_lo)
    bg = jnp.where(lane < GDN_HEADS, beta, gc)
    bg_ref[...] = bg
    bgt_ref[...] = bg.T[0:8, :]


def _proj(x, pos3, shift, scale, norm_g, w_all, conv_w, alog_row, dtb_row, qg_row, kg_row,
          lmat, gmat, freq_row, sgn_rows):
    b, s, d = x.shape
    tm = PROJ_TM
    row = lambda bi, si: (bi, si, 0)
    const2 = lambda bi, si: (0, 0)
    per_b = lambda bi, si: (bi, 0, 0)
    hw = jax.ShapeDtypeStruct((b, s, HEADW), BF16)
    out_shape = [hw] * 8 + [jax.ShapeDtypeStruct((b, s, LANES), F32),
                            jax.ShapeDtypeStruct((b, 8, s), F32)]
    hw_spec = pl.BlockSpec((None, tm, HEADW), row)
    out_specs = [hw_spec] * 8 + [pl.BlockSpec((None, tm, LANES), row),
                                 pl.BlockSpec((None, 8, tm), lambda bi, si: (bi, 0, si))]
    in_specs = [
        pl.BlockSpec((None, tm, d), row),
        pl.BlockSpec((None, tm, 1), row),
        pl.BlockSpec((None, 1, d), per_b),
        pl.BlockSpec((None, 1, d), per_b),
        pl.BlockSpec((1, d), const2),
        pl.BlockSpec((d, W_COLS), const2),
        pl.BlockSpec((CONV_K, GDN_QKV), const2),
        pl.BlockSpec((1, LANES), const2),
        pl.BlockSpec((1, LANES), const2),
        pl.BlockSpec((1, HEADW), const2),
        pl.BlockSpec((1, HEADW), const2),
        pl.BlockSpec((tm, tm), const2),
        pl.BlockSpec((GMAT_W, GMAT_W), const2),
        pl.BlockSpec((1, LANES), const2),
        pl.BlockSpec((2, LANES), const2),
    ]
    return pl.pallas_call(
        _proj_kernel,
        grid=(b, s // tm),
        in_specs=in_specs,
        out_specs=out_specs,
        out_shape=out_shape,
        scratch_shapes=[pltpu.VMEM((tm + 8, GDN_QKV), F32), pltpu.VMEM((tm, d), BF16)],
        compiler_params=pltpu.CompilerParams(
            dimension_semantics=("arbitrary", "arbitrary"), vmem_limit_bytes=VMEM_LIMIT),
        name="proj",
    )(x, pos3, shift, scale, norm_g, w_all, conv_w, alog_row, dtb_row, qg_row, kg_row,
      lmat, gmat, freq_row, sgn_rows)


def _split(a):
    hi = a.astype(BF16)
    return hi, (a - hi.astype(F32)).astype(BF16)


def _gdn_kernel(q_ref, k_ref, v_ref, gate_ref, bg_ref, bgt_ref, ng_ref, o_ref,
                state, wq_s, u_s, kva_s, vn_s):
    nb, tc = q_ref.shape[0], q_ref.shape[1]
    npair = tc // PAIR
    nc = tc // CHUNK

    @pl.when(pl.program_id(0) == 0)
    def _():
        state[...] = jnp.zeros(state.shape, F32)
        vn_s[...] = jnp.zeros(vn_s.shape, BF16)

    ii = lax.broadcasted_iota(jnp.int32, (CHUNK, PAIR), 0)
    lane_w = lax.broadcasted_iota(jnp.int32, (CHUNK, PAIR), 1)
    jj = lane_w % CHUNK
    left_w = lane_w < CHUNK
    left_p = lax.broadcasted_iota(jnp.int32, (PAIR, PAIR), 1) < CHUNK
    top_col = lax.broadcasted_iota(jnp.int32, (PAIR, 1), 0) < CHUNK
    eye = (ii == jj).astype(F32)
    lower = ii >= jj
    strict = ii > jj

    def widen(m):
        return jnp.where(left_w, m[0:CHUNK], m[CHUNK:PAIR])

    def bdiag(wd):
        z = jnp.zeros_like(wd)
        return jnp.concatenate([jnp.where(left_w, wd, z), jnp.where(left_w, z, wd)], axis=0)

    def mm(lhs, rhs):
        n = len(lhs)
        b_hi, b_lo = _split(bdiag(rhs))
        parts = [_split(a) for a in lhs]
        his = [hi_ for hi_, _ in parts]
        r_hi = _dot(jnp.concatenate(his + [lo_ for _, lo_ in parts], axis=0), b_hi)
        r_lo = _dot(jnp.concatenate(his, axis=0) if n > 1 else his[0], b_lo)
        blk = lambda r, i: r[i * CHUNK:(i + 1) * CHUNK]
        return [blk(r_hi, i) + blk(r_hi, n + i) + blk(r_lo, i) for i in range(n)]

    items = [(bb, hh, p) for bb in range(nb) for hh in range(GDN_HEADS) for p in range(npair)]
    pre = []
    for bb, hh, p in items:
        cs = slice(hh * LANES, (hh + 1) * LANES)
        rs = slice(p * PAIR, (p + 1) * PAIR)
        q2, k2 = q_ref[bb, rs, cs], k_ref[bb, rs, cs]
        gcol2 = bg_ref[bb, rs, GDN_HEADS + hh:GDN_HEADS + hh + 1]
        bcol_w = widen(jnp.broadcast_to(bg_ref[bb, rs, hh:hh + 1], (PAIR, PAIR)))
        gcol_w = widen(jnp.broadcast_to(gcol2, (PAIR, PAIR)))
        grow = bgt_ref[bb, GDN_HEADS + hh:GDN_HEADS + hh + 1, rs]
        dec = jnp.where(lower, jnp.exp(jnp.minimum(gcol_w - grow, 0.0)), 0.0)
        gram = _dot_nt(jnp.concatenate([k2, q2], axis=0), k2)
        xm = jnp.where(strict, -(bcol_w * widen(gram[0:PAIR]) * dec), 0.0)
        a_w = jnp.where(lower, widen(gram[PAIR:2 * PAIR]) * dec, 0.0)
        g_end = jnp.where(top_col, grow[:, CHUNK - 1:CHUNK], grow[:, PAIR - 1:PAIR])
        e_g = jnp.exp(gcol2)
        k2f = k2.astype(F32)
        qd = (q2.astype(F32) * e_g).astype(BF16)
        kdt = (k2f * jnp.exp(g_end - gcol2)).T
        a_bd = bdiag(a_w)
        kva_s[bb, hh, 2 * p] = jnp.concatenate(
            [jnp.where(left_p, kdt, 0.0), a_bd[0:CHUNK]], axis=0).astype(BF16)
        kva_s[bb, hh, 2 * p + 1] = jnp.concatenate(
            [jnp.where(left_p, 0.0, kdt), a_bd[CHUNK:PAIR]], axis=0).astype(BF16)
        pre.append((xm, qd, (k2f * e_g).astype(BF16)))
    ts = [eye + xm for xm, _, _ in pre]
    ps = [mm([xm], xm)[0] for xm, _, _ in pre]
    for _ in range(4):
        both = [mm([t, pw], pw) for t, pw in zip(ts, ps)]
        ts = [t + tp for t, (tp, _) in zip(ts, both)]
        ps = [pp for _, pp in both]
    ts = [t + mm([t], pw)[0] for t, pw in zip(ts, ps)]
    for (bb, hh, p), t, (_, qd, kg) in zip(items, ts, pre):
        cs = slice(hh * LANES, (hh + 1) * LANES)
        rs = slice(p * PAIR, (p + 1) * PAIR)
        tb = bdiag(t * bgt_ref[bb, hh:hh + 1, rs]).astype(BF16)
        uw = _dot(tb, jnp.concatenate([v_ref[bb, rs, cs], kg], axis=1))
        u_s[bb, hh, rs, :] = uw[:, 0:GDN_DV]
        w2 = uw[:, GDN_DV:GDN_DV + GDN_DK].astype(BF16)
        for e in range(2):
            es = slice(e * CHUNK, (e + 1) * CHUNK)
            wq_s[bb, hh, 2 * p + e] = jnp.concatenate([w2[es], qd[es]], axis=0)

    chains = [(bb, hh) for bb in range(nb) for hh in range(GDN_HEADS)]
    for c in range(nc):
        e = c % 2
        rs = slice(c * CHUNK, (c + 1) * CHUNK)
        es = slice(e * CHUNK, (e + 1) * CHUNK)
        sts = [state[ch] for ch in chains]
        rr = [_dot(wq_s[bb, hh, c], st.astype(BF16)) for (bb, hh), st in zip(chains, sts)]
        for (bb, hh), r in zip(chains, rr):
            vn_s[bb, hh, es, :] = (u_s[bb, hh, rs, :] - r[0:CHUNK]).astype(BF16)
        for (bb, hh), st, r in zip(chains, sts, rr):
            cs = slice(hh * LANES, (hh + 1) * LANES)
            g_end = bgt_ref[bb, GDN_HEADS + hh:GDN_HEADS + hh + 1, (c + 1) * CHUNK - 1:(c + 1) * CHUNK]
            kva = _dot(kva_s[bb, hh, c], vn_s[bb, hh])
            state[bb, hh] = st * jnp.exp(g_end) + kva[0:GDN_DK]
            o = r[CHUNK:PAIR] + kva[GDN_DK:GDN_DK + CHUNK]
            on = o * lax.rsqrt(jnp.mean(o * o, axis=-1, keepdims=True) + NORM_EPS) * ng_ref[...]
            o_ref[bb, rs, cs] = (on * gate_ref[bb, rs, cs].astype(F32)).astype(BF16)


def _gdn(gq, gk, gv, agate, bg, bgt, ng_row):
    b, s, _ = gq.shape
    tc = GDN_TC
    row = lambda si: (0, si, 0)
    hw_spec = pl.BlockSpec((b, tc, HEADW), row)
    per_chain = (b, GDN_HEADS)
    return pl.pallas_call(
        _gdn_kernel,
        grid=(s // tc,),
        in_specs=[hw_spec, hw_spec, hw_spec, hw_spec,
                  pl.BlockSpec((b, tc, LANES), row),
                  pl.BlockSpec((b, 8, tc), lambda si: (0, 0, si)),
                  pl.BlockSpec((1, GDN_DV), lambda si: (0, 0))],
        out_specs=hw_spec,
        out_shape=jax.ShapeDtypeStruct((b, s, HEADW), BF16),
        scratch_shapes=[pltpu.VMEM(per_chain + (GDN_DK, GDN_DV), F32),
                        pltpu.VMEM(per_chain + (tc // CHUNK, PAIR, GDN_DK), BF16),
                        pltpu.VMEM(per_chain + (tc, GDN_DV), F32),
                        pltpu.VMEM(per_chain + (tc // CHUNK, GDN_DK + CHUNK, PAIR), BF16),
                        pltpu.VMEM(per_chain + (PAIR, GDN_DV), BF16)],
        compiler_params=pltpu.CompilerParams(
            dimension_semantics=("arbitrary",), vmem_limit_bytes=VMEM_LIMIT),
        name="gdn",
    )(gq, gk, gv, agate, bg, bgt, ng_row)


def _attn_kernel(q_ref, k_ref, v_ref, gate_ref, lamp_ref, sg_ref, gqk_ref, o_ref,
                 vext, acc, pbuf, qs_s, kaug_s, *, lambda_init, tq):
    s_len = q_ref.shape[0]
    nq = s_len // tq
    tk = tq
    nch = tq // CHUNK
    w = 2 * DIFF_D

    vext[:, 0:w] = v_ref[...]
    vext[:, w:2 * w] = jnp.ones((s_len, w), BF16)
    acc[...] = jnp.zeros(acc.shape, F32)

    lane = lax.broadcasted_iota(jnp.int32, (tk, w), 1)
    rchunk = lax.broadcasted_iota(jnp.int32, (tk, w), 0) // CHUNK
    lo = lane < DIFF_D
    hi = jnp.logical_not(lo)
    half = (lo, hi)

    gqk = gqk_ref[...]
    bound = (jnp.max(jnp.abs(gqk[0:1, :])) * jnp.max(jnp.abs(gqk[1:2, :]))
             * (DIFF_D * DIFF_D ** -0.5 * math.log2(math.e) * 1.01))

    lp = lamp_ref[...]
    lam = (jnp.exp(jnp.sum(lp[0:1, :] * lp[1:2, :], axis=-1, keepdims=True))
           - jnp.exp(jnp.sum(lp[2:3, :] * lp[3:4, :], axis=-1, keepdims=True)) + lambda_init)

    def rows_of(blk):
        return pl.ds(pl.multiple_of(blk * tq, tq), tq)

    def finalize(qi):
        o = acc[0, :, 0:w] / acc[0, :, w:2 * w] - lam * (acc[1, :, 0:w] / acc[1, :, w:2 * w])
        on = o * lax.rsqrt(jnp.mean(o * o, axis=-1, keepdims=True) + NORM_EPS) * sg_ref[...]
        on = on * (1.0 - lambda_init)
        o_ref[rows_of(qi), :] = (on * gate_ref[rows_of(qi), :].astype(F32)).astype(BF16)
        acc[...] = jnp.zeros(acc.shape, F32)

    @pl.when(bound <= MAX_STATIC_SHIFT)
    def _():
        q_aug = []
        for sub, base in enumerate((DIFF_D, 0)):
            mlane = lane - (base + 1)
            in_mask = (mlane >= 0) & (mlane < nch)
            q_aug.append(jnp.where((lane == base) | (in_mask & (rchunk == mlane)), 1.0, 0.0).astype(BF16))
            plain = jnp.where(lane == base, -bound, 0.0)
            kaug_s[0, sub] = plain.astype(BF16)
            kaug_s[1, sub] = jnp.where(in_mask & (rchunk > mlane), -MASK_BIG, plain).astype(BF16)

        def prep_q(qi, qslot):
            q = q_ref[rows_of(qi), :]
            for sub in range(2):
                qs_s[qslot, sub] = jnp.where(half[sub], q, q_aug[sub])

        def scores(qslot, j, is_diag, pslot):
            kb = k_ref[rows_of(j), :]
            for sub in range(2):
                ks = jnp.where(half[sub], kb, kaug_s[is_diag, sub])
                pbuf[pslot, sub] = jnp.exp2(_dot_nt(qs_s[qslot, sub], ks)).astype(BF16)

        def accumulate(j, pslot):
            ve = vext[rows_of(j), :]
            for sub in range(2):
                acc[sub] += _dot(pbuf[pslot, sub], ve)

        prep_q(0, 0)
        scores(0, 0, 1, 0)

        def outer(qi, n):
            qslot = qi & 1

            def advance(j, n):
                accumulate(j - 1, n & 1)
                scores(qslot, j, jnp.asarray(j == qi, jnp.int32), (n + 1) & 1)

            def pair(t, n):
                advance(2 * t + 1, n)
                advance(2 * t + 2, n + 1)
                return n + 2

            n = lax.fori_loop(0, qi >> 1, pair, n)

            @pl.when((qi & 1) == 1)
            def _():
                advance(qi, n)

            n = n + (qi & 1)
            nxt = jnp.minimum(qi + 1, nq - 1)
            prep_q(nxt, 1 - qslot)
            accumulate(qi, n & 1)
            scores(1 - qslot, 0, jnp.asarray(nxt == 0, jnp.int32), (n + 1) & 1)
            finalize(qi)
            return n + 1

        lax.fori_loop(0, nq, outer, 0)

    @pl.when(bound > MAX_STATIC_SHIFT)
    def _():
        zero = jnp.zeros((tk, w), BF16)
        rr = lax.broadcasted_iota(jnp.int32, (tq, tk), 0) // CHUNK
        cc = lax.broadcasted_iota(jnp.int32, (tq, tk), 1) // CHUNK
        causal = cc <= rr

        def outer(qi, carry):
            q = q_ref[rows_of(qi), :]

            def step(j, ms, diag):
                kb = k_ref[rows_of(j), :]
                ve = vext[rows_of(j), :]
                new = []
                for sub in range(2):
                    s = _dot_nt(q, jnp.where(half[sub], kb, zero))
                    if diag:
                        s = jnp.where(causal, s, NEG_BIG)
                    m_new = jnp.maximum(ms[sub], jnp.max(s, axis=-1, keepdims=True))
                    p = jnp.exp2(s - m_new)
                    acc[sub] = jnp.exp2(ms[sub] - m_new) * acc[sub] + _dot(p.astype(BF16), ve)
                    new.append(m_new)
                return tuple(new)

            init = (jnp.full((tq, 1), NEG_BIG, F32), jnp.full((tq, 1), NEG_BIG, F32))
            ms = lax.fori_loop(0, qi, lambda j, c: step(j, c, False), init)
            step(qi, ms, True)
            finalize(qi)
            return carry

        lax.fori_loop(0, nq, outer, 0)


def _attn(bq, bk, bv, bgate, lam_params, sg_row, gqk_rows, lambda_init):
    b, s, _ = bq.shape
    tq = ATT_TQ
    w = 2 * DIFF_D
    hspec = pl.BlockSpec((None, s, w), lambda bi, hi: (bi, 0, hi))
    const = lambda bi, hi: (0, 0)
    return pl.pallas_call(
        functools.partial(_attn_kernel, lambda_init=lambda_init, tq=tq),
        grid=(b, DIFF_HEADS),
        in_specs=[hspec, hspec, hspec, hspec,
                  pl.BlockSpec((8, LANES), const),
                  pl.BlockSpec((1, w), const),
                  pl.BlockSpec((2, LANES), const)],
        out_specs=hspec,
        out_shape=jax.ShapeDtypeStruct((b, s, HEADW), BF16),
        scratch_shapes=[pltpu.VMEM((s, 2 * w), BF16),
                        pltpu.VMEM((2, tq, 2 * w), F32),
                        pltpu.VMEM((2, 2, tq, tq), BF16),
                        pltpu.VMEM((2, 2, tq, w), BF16),
                        pltpu.VMEM((2, 2, tq, w), BF16)],
        compiler_params=pltpu.CompilerParams(
            dimension_semantics=("arbitrary", "arbitrary"), vmem_limit_bytes=VMEM_LIMIT),
        name="attn",
    )(bq, bk, bv, bgate, lam_params, sg_row, gqk_rows)


def _out_kernel(x_ref, ma_ref, mb_ref, wa_ref, wb_ref, gate_ref, o_ref):
    y = _dot(ma_ref[...], wa_ref[...]) + _dot(mb_ref[...], wb_ref[...])
    o_ref[...] = x_ref[...] + gate_ref[...] * y


def _out(x, mixed_a, mixed_b, w_a, w_b, gate):
    b, s, d = x.shape
    tm = OUT_TM
    row = lambda bi, si: (bi, si, 0)
    const2 = lambda bi, si: (0, 0)
    return pl.pallas_call(
        _out_kernel,
        grid=(b, s // tm),
        in_specs=[pl.BlockSpec((None, tm, d), row),
                  pl.BlockSpec((None, tm, HEADW), row),
                  pl.BlockSpec((None, tm, HEADW), row),
                  pl.BlockSpec((HEADW, d), const2),
                  pl.BlockSpec((HEADW, d), const2),
                  pl.BlockSpec((None, 1, d), lambda bi, si: (bi, 0, 0))],
        out_specs=pl.BlockSpec((None, tm, d), row),
        out_shape=jax.ShapeDtypeStruct((b, s, d), F32),
        compiler_params=pltpu.CompilerParams(
            dimension_semantics=("arbitrary", "arbitrary"), vmem_limit_bytes=VMEM_LIMIT),
        name="out",
    )(x, mixed_a, mixed_b, w_a, w_b, gate)


def _constants():
    lane = jnp.arange(LANES)
    sub = lane % DIFF_D
    half = ROT_DIM // 2
    inv_freq = ROPE_THETA ** (-jnp.arange(0, ROT_DIM, 2, dtype=F32) / ROT_DIM)
    freq_row = jnp.where(sub < ROT_DIM, inv_freq[sub % half], 0.0).astype(F32)[None, :]
    sgn_rows = jnp.stack([jnp.where(sub < half, -1.0, 0.0),
                          jnp.where((sub >= half) & (sub < ROT_DIM), 1.0, 0.0)]).astype(F32)
    r = jnp.arange(PROJ_TM)
    lmat = ((r[:, None] // CHUNK == r[None, :] // CHUNK) & (r[None, :] <= r[:, None])).astype(BF16)
    g = jnp.arange(GMAT_W)
    gmat = jnp.where(g[:, None] // DIFF_D == g[None, :] // DIFF_D, 1.0 / DIFF_D, 0.0).astype(BF16)
    return freq_row, sgn_rows, lmat, gmat


def _pad_lanes(v, offset):
    return jnp.zeros((1, LANES), F32).at[0, offset:offset + v.shape[0]].set(v.astype(F32))


def _layer(x, c_pad, pos3, lambda_init, norm_g, w_ada, b_ada, w_in, conv_w, a_log, dt_bias, gdn_norm_g,
           q_norm_g, k_norm_g, lam_q1, lam_k1, lam_q2, lam_k2, subln_g, w_out):
    b, s, d = x.shape
    mod = _ada(c_pad, w_ada, b_ada[None, :])[:b]
    shift = mod[:, None, 0:d]
    scale = mod[:, None, d:2 * d]
    gate = mod[:, None, 2 * d:3 * d]

    nk = GDN_HEADS * GDN_DK
    o_beta = 2 * nk + GDN_HEADS * GDN_DV
    o_agate = o_beta + 2 * GDN_HEADS
    o_b = o_agate + GDN_HEADS * GDN_DV
    bd = jnp.pad(w_in[:, o_beta:o_agate], ((0, 0), (0, LANES - 2 * GDN_HEADS)))
    w_all = jnp.concatenate([
        w_in[:, 0:o_beta], w_in[:, o_agate:o_b],
        w_in[:, o_b:o_b + HEADW], w_in[:, o_b + HEADW:o_b + 2 * HEADW],
        w_in[:, o_b + 2 * HEADW:o_b + 3 * HEADW], w_in[:, o_b + 3 * HEADW:o_b + 4 * HEADW],
        bd], axis=1).astype(BF16)
    freq_row, sgn_rows, lmat, gmat = _constants()
    lam_params = jnp.zeros((8, LANES), F32)
    for r, v in enumerate((lam_q1, lam_k1, lam_q2, lam_k2)):
        lam_params = lam_params.at[r, 0:DIFF_D].set(v.astype(F32))

    gq, gk, gv, agate, bq, bk, bv, bgate, bg, bgt = _proj(
        x, pos3, shift, scale, norm_g[None, :], w_all, conv_w,
        _pad_lanes(a_log, GDN_HEADS), _pad_lanes(dt_bias, GDN_HEADS),
        jnp.tile(q_norm_g, 2 * DIFF_HEADS)[None, :], jnp.tile(k_norm_g, 2 * DIFF_HEADS)[None, :],
        lmat, gmat, freq_row, sgn_rows)
    mixed_a = _gdn(gq, gk, gv, agate, bg, bgt, gdn_norm_g[None, :])
    gqk_rows = jnp.concatenate([_pad_lanes(q_norm_g, 0), _pad_lanes(k_norm_g, 0)], axis=0)
    mixed_b = _attn(bq, bk, bv, bgate, lam_params, subln_g[None, :], gqk_rows, lambda_init)
    w_o = w_out.astype(BF16)
    return _out(x, mixed_a, mixed_b, w_o[:HEADW], w_o[HEADW:], gate)


def kernel(x, c, positions, norm_g, w_ada, b_ada, w_in, conv_w, a_log, dt_bias, gdn_norm_g, q_norm_g,
           k_norm_g, lambda_q1, lambda_k1, lambda_q2, lambda_k2, subln_g, w_out):
    b = x.shape[0]
    c_pad = jnp.pad(c, ((0, 8 - b % 8 if b % 8 else 0), (0, 0)))
    pos3 = positions[:, :, None]
    for l in range(norm_g.shape[0]):
        lambda_init = 0.8 - 0.6 * math.exp(-0.3 * l)
        x = _layer(x, c_pad, pos3, lambda_init, norm_g[l], w_ada[l], b_ada[l], w_in[l], conv_w[l],
                   a_log[l], dt_bias[l], gdn_norm_g[l], q_norm_g[l], k_norm_g[l], lambda_q1[l],
                   lambda_k1[l], lambda_q2[l], lambda_k2[l], subln_g[l], w_out[l])
    return x
```

```python
import functools
import math

import jax
import jax.numpy as jnp
from jax import lax
from jax.experimental import pallas as pl
from jax.experimental.pallas import tpu as pltpu

F32 = jnp.float32
BF16 = jnp.bfloat16
HIGHEST = lax.Precision.HIGHEST

CHUNK = 64
PAIR = 2 * CHUNK
GDN_HEADS = 4
GDN_DK = 128
GDN_DV = 128
CONV_K = 4
DIFF_HEADS = 4
DIFF_D = 64
ROT_DIM = 16
ROPE_THETA = 500000.0
NORM_EPS = 1e-6
LANES = 128
NEG_BIG = -1e30
MASK_BIG = 256.0

GDN_QKV = 3 * GDN_HEADS * GDN_DK
HEADW = 512
OFF_GQKV = 0
OFF_AGATE = GDN_QKV
OFF_BQ = OFF_AGATE + HEADW
OFF_BK = OFF_BQ + HEADW
OFF_BV = OFF_BK + HEADW
OFF_BGATE = OFF_BV + HEADW
OFF_BD = OFF_BGATE + HEADW
W_COLS = OFF_BD + LANES

PROJ_TM = 512
PROJ_RB = 128
GMAT_W = 256
GDN_TC = 128
ATT_TQ = 512
ATT_UNROLL = 4
MAX_STATIC_SHIFT = 60.0
OUT_TM = 512
VMEM_LIMIT = 48 * 1024 * 1024


def _silu(x):
    h = 0.5 * x
    return h + h * jnp.tanh(h)


def _dot(a, b, **kw):
    return jnp.dot(a, b, preferred_element_type=F32, **kw)


def _dot_nt(a, b, **kw):
    return lax.dot_general(a, b, (((1,), (1,)), ((), ())), preferred_element_type=F32, **kw)


def _dot_tn(a, b, **kw):
    return lax.dot_general(a, b, (((0,), (0,)), ((), ())), preferred_element_type=F32, **kw)


def _ada_kernel(c_ref, w_ref, b_ref, o_ref):
    o_ref[...] = _dot(_silu(c_ref[...]), w_ref[...], precision=HIGHEST) + b_ref[...]


def _ada(c_pad, w_ada, b_ada):
    rows, d = c_pad.shape
    n = w_ada.shape[1]
    tn = 512
    return pl.pallas_call(
        _ada_kernel,
        grid=(n // tn,),
        in_specs=[pl.BlockSpec((rows, d), lambda j: (0, 0)),
                  pl.BlockSpec((d, tn), lambda j: (0, j)),
                  pl.BlockSpec((1, tn), lambda j: (0, j))],
        out_specs=pl.BlockSpec((rows, tn), lambda j: (0, j)),
        out_shape=jax.ShapeDtypeStruct((rows, n), F32),
        name="ada",
    )(c_pad, w_ada, b_ada)


def _proj_kernel(x_ref, pos_ref, shift_ref, scale_ref, ng_ref, w_ref, convw_ref, alog_ref, dtb_ref,
                 qg_ref, kg_ref, lmat_ref, gmat_ref, freq_ref, sgn_ref,
                 gq_ref, gk_ref, gv_ref, ag_ref, bq_ref, bk_ref, bv_ref, bgate_ref, bg_ref, bgt_ref,
                 cq, ck, cv, z0, z1, h_s, gm_s, tab):
    tm = x_ref.shape[0]
    x = x_ref[...]
    ms = jnp.mean(x * x, axis=-1, keepdims=True)
    gain = ng_ref[...] * (1.0 + scale_ref[...])
    h_s[...] = (x * lax.rsqrt(ms + NORM_EPS) * gain + shift_ref[...]).astype(BF16)

    cbufs = (cq, ck, cv)
    zbuf = (z0, z1)

    @pl.when(pl.program_id(1) == 0)
    def _():
        for cb in cbufs:
            cb[0:8, :] = jnp.zeros((8, HEADW), F32)

    def mm_conv(g):
        c0 = OFF_GQKV + g * HEADW
        cbufs[g][8:8 + tm, :] = _dot(h_s[...], w_ref[:, c0:c0 + HEADW])

    def mm_stage(off, width, slot):
        zbuf[slot][:, 0:width] = _dot(h_s[...], w_ref[:, off:off + width])

    def ep_conv(g, out_ref):
        cb = cbufs[g]
        for hh in range(GDN_HEADS):
            hs = slice(hh * LANES, (hh + 1) * LANES)
            cs = slice(g * HEADW + hh * LANES, g * HEADW + (hh + 1) * LANES)
            for r0 in range(0, tm, PROJ_RB):
                acc = convw_ref[3:4, cs] * cb[8 + r0:8 + r0 + PROJ_RB, hs]
                for j in range(CONV_K - 1):
                    acc = acc + convw_ref[j:j + 1, cs] * cb[5 + j + r0:5 + j + r0 + PROJ_RB, hs]
                a = _silu(acc)
                if g < 2:
                    a = a * lax.rsqrt(jnp.sum(a * a, axis=-1, keepdims=True) + NORM_EPS)
                    if g == 0:
                        a = a * (GDN_DK ** -0.5)
                out_ref[r0:r0 + PROJ_RB, hs] = a.astype(BF16)
        cb[0:8, :] = cb[tm:tm + 8, :]

    def ep_gate(slot, out_ref):
        for r0 in range(0, tm, PROJ_RB):
            rs = slice(r0, r0 + PROJ_RB)
            out_ref[rs, :] = _silu(zbuf[slot][rs, :]).astype(BF16)

    def ep_qk(slot, g_ref, out_ref, post):
        z = zbuf[slot][...]
        zz = (z * z).astype(BF16)
        gw = gmat_ref.shape[0]
        for c0 in range(0, HEADW, gw):
            gm_s[:, c0:c0 + gw] = _dot(zz[:, c0:c0 + gw], gmat_ref[...])
        for r0 in range(0, tm, PROJ_RB):
            rs = slice(r0, r0 + PROJ_RB)
            for hh in range(DIFF_HEADS):
                hs = slice(hh * LANES, (hh + 1) * LANES)
                ys = zbuf[slot][rs, hs] * lax.rsqrt(gm_s[rs, hs] + NORM_EPS) * g_ref[:, hs]
                r = (ys * tab[0, rs, :] + pltpu.roll(ys, LANES - ROT_DIM // 2, axis=1) * tab[1, rs, :]
                     + pltpu.roll(ys, ROT_DIM // 2, axis=1) * tab[2, rs, :])
                if post is not None:
                    r = r * post
                out_ref[rs, hs] = r.astype(BF16)

    def rope_tables():
        for r0 in range(0, tm, PROJ_RB):
            rs = slice(r0, r0 + PROJ_RB)
            ang = pos_ref[rs, :].astype(F32) * freq_ref[...]
            sin_t = jnp.sin(ang)
            tab[0, rs, :] = jnp.cos(ang)
            tab[1, rs, :] = sin_t * sgn_ref[0:1, :]
            tab[2, rs, :] = sin_t * sgn_ref[1:2, :]

    def ep_bd(slot):
        z = zbuf[slot][:, 0:LANES]
        lane = lax.broadcasted_iota(jnp.int32, (1, LANES), 1)
        beta = 1.0 / (1.0 + jnp.exp(-z))
        zz = z + dtb_ref[...]
        softplus = jnp.maximum(zz, 0.0) + jnp.log(1.0 + jnp.exp(-jnp.abs(zz)))
        is_g = (lane >= GDN_HEADS) & (lane < 2 * GDN_HEADS)
        g = jnp.where(is_g, -jnp.exp(alog_ref[...]) * softplus, 0.0)
        g_hi = g.astype(BF16)
        g_lo = (g - g_hi.astype(F32)).astype(BF16)
        gc = _dot(lmat_ref[...], g_hi) + _dot(lmat_ref[...], g_lo)
        bg = jnp.where(lane < GDN_HEADS, beta, gc)
        bg_ref[...] = bg
        bgt_ref[...] = bg.T[0:8, :]

    mm_conv(0)
    mm_stage(OFF_BD, LANES, 1)
    mm_conv(1)
    ep_conv(0, gq_ref)
    ep_bd(1)
    mm_conv(2)
    ep_conv(1, gk_ref)
    mm_stage(OFF_BQ, HEADW, 0)
    ep_conv(2, gv_ref)
    rope_tables()
    mm_stage(OFF_BK, HEADW, 1)
    ep_qk(0, qg_ref, bq_ref, DIFF_D ** -0.5 * math.log2(math.e))
    mm_stage(OFF_AGATE, HEADW, 0)
    ep_qk(1, kg_ref, bk_ref, None)
    mm_stage(OFF_BGATE, HEADW, 1)
    ep_gate(0, ag_ref)
    mm_stage(OFF_BV, HEADW, 0)
    ep_gate(1, bgate_ref)
    bv_ref[...] = zbuf[0][...].astype(BF16)


def _proj(x, pos3, shift, scale, norm_g, w_all, conv_w, alog_row, dtb_row, qg_row, kg_row,
          lmat, gmat, freq_row, sgn_rows):
    b, s, d = x.shape
    tm = PROJ_TM
    row = lambda bi, si: (bi, si, 0)
    const2 = lambda bi, si: (0, 0)
    per_b = lambda bi, si: (bi, 0, 0)
    hw = jax.ShapeDtypeStruct((b, s, HEADW), BF16)
    out_shape = [hw] * 8 + [jax.ShapeDtypeStruct((b, s, LANES), F32),
                            jax.ShapeDtypeStruct((b, 8, s), F32)]
    hw_spec = pl.BlockSpec((None, tm, HEADW), row)
    out_specs = [hw_spec] * 8 + [pl.BlockSpec((None, tm, LANES), row),
                                 pl.BlockSpec((None, 8, tm), lambda bi, si: (bi, 0, si))]
    in_specs = [
        pl.BlockSpec((None, tm, d), row),
        pl.BlockSpec((None, tm, 1), row),
        pl.BlockSpec((None, 1, d), per_b),
        pl.BlockSpec((None, 1, d), per_b),
        pl.BlockSpec((1, d), const2),
        pl.BlockSpec((d, W_COLS), const2),
        pl.BlockSpec((CONV_K, GDN_QKV), const2),
        pl.BlockSpec((1, LANES), const2),
        pl.BlockSpec((1, LANES), const2),
        pl.BlockSpec((1, HEADW), const2),
        pl.BlockSpec((1, HEADW), const2),
        pl.BlockSpec((tm, tm), const2),
        pl.BlockSpec((GMAT_W, GMAT_W), const2),
        pl.BlockSpec((1, LANES), const2),
        pl.BlockSpec((2, LANES), const2),
    ]
    return pl.pallas_call(
        _proj_kernel,
        grid=(b, s // tm),
        in_specs=in_specs,
        out_specs=out_specs,
        out_shape=out_shape,
        scratch_shapes=[pltpu.VMEM((tm + 8, HEADW), F32)] * 3 + [pltpu.VMEM((tm, HEADW), F32)] * 2
        + [pltpu.VMEM((tm, d), BF16), pltpu.VMEM((tm, HEADW), F32), pltpu.VMEM((3, tm, LANES), F32)],
        compiler_params=pltpu.CompilerParams(
            dimension_semantics=("arbitrary", "arbitrary"), vmem_limit_bytes=VMEM_LIMIT),
        name="proj",
    )(x, pos3, shift, scale, norm_g, w_all, conv_w, alog_row, dtb_row, qg_row, kg_row,
      lmat, gmat, freq_row, sgn_rows)


def _split(a):
    hi = a.astype(BF16)
    return hi, (a - hi.astype(F32)).astype(BF16)


def _gdn_kernel(q_ref, k_ref, v_ref, gate_ref, bg_ref, bgt_ref, ng_ref, o_ref,
                state, wq_s, u_s, kva_s, vn_s):
    nb, tc = q_ref.shape[0], q_ref.shape[1]
    npair = tc // PAIR
    nc = tc // CHUNK

    @pl.when(pl.program_id(0) == 0)
    def _():
        state[...] = jnp.zeros(state.shape, F32)
        vn_s[...] = jnp.zeros(vn_s.shape, BF16)

    ii = lax.broadcasted_iota(jnp.int32, (CHUNK, PAIR), 0)
    lane_w = lax.broadcasted_iota(jnp.int32, (CHUNK, PAIR), 1)
    jj = lane_w % CHUNK
    left_w = lane_w < CHUNK
    left_p = lax.broadcasted_iota(jnp.int32, (PAIR, PAIR), 1) < CHUNK
    top_col = lax.broadcasted_iota(jnp.int32, (PAIR, 1), 0) < CHUNK
    eye = (ii == jj).astype(F32)
    lower = ii >= jj
    strict = ii > jj

    def widen(m):
        return jnp.where(left_w, m[0:CHUNK], m[CHUNK:PAIR])

    def bdiag(wd):
        z = jnp.zeros_like(wd)
        return jnp.concatenate([jnp.where(left_w, wd, z), jnp.where(left_w, z, wd)], axis=0)

    def mm(lhs, rhs):
        n = len(lhs)
        b_hi, b_lo = _split(bdiag(rhs))
        parts = [_split(a) for a in lhs]
        his = [hi_ for hi_, _ in parts]
        r_hi = _dot(jnp.concatenate(his + [lo_ for _, lo_ in parts], axis=0), b_hi)
        r_lo = _dot(jnp.concatenate(his, axis=0) if n > 1 else his[0], b_lo)
        blk = lambda r, i: r[i * CHUNK:(i + 1) * CHUNK]
        return [blk(r_hi, i) + blk(r_hi, n + i) + blk(r_lo, i) for i in range(n)]

    items = [(bb, hh, p) for bb in range(nb) for hh in range(GDN_HEADS) for p in range(npair)]
    pre = []
    for bb, hh, p in items:
        cs = slice(hh * LANES, (hh + 1) * LANES)
        rs = slice(p * PAIR, (p + 1) * PAIR)
        q2, k2 = q_ref[bb, rs, cs], k_ref[bb, rs, cs]
        gcol2 = bg_ref[bb, rs, GDN_HEADS + hh:GDN_HEADS + hh + 1]
        bcol_w = widen(jnp.broadcast_to(bg_ref[bb, rs, hh:hh + 1], (PAIR, PAIR)))
        gcol_w = widen(jnp.broadcast_to(gcol2, (PAIR, PAIR)))
        grow = bgt_ref[bb, GDN_HEADS + hh:GDN_HEADS + hh + 1, rs]
        dec = jnp.where(lower, jnp.exp(jnp.minimum(gcol_w - grow, 0.0)), 0.0)
        gram = _dot_nt(jnp.concatenate([k2, q2], axis=0), k2)
        xm = jnp.where(strict, -(bcol_w * widen(gram[0:PAIR]) * dec), 0.0)
        a_w = jnp.where(lower, widen(gram[PAIR:2 * PAIR]) * dec, 0.0)
        g_end = jnp.where(top_col, grow[:, CHUNK - 1:CHUNK], grow[:, PAIR - 1:PAIR])
        e_g = jnp.exp(gcol2)
        k2f = k2.astype(F32)
        qd = (q2.astype(F32) * e_g).astype(BF16)
        kdt = (k2f * jnp.exp(g_end - gcol2)).T
        a_bd = bdiag(a_w)
        kva_s[bb, hh, 2 * p] = jnp.concatenate(
            [jnp.where(left_p, kdt, 0.0), a_bd[0:CHUNK]], axis=0).astype(BF16)
        kva_s[bb, hh, 2 * p + 1] = jnp.concatenate(
            [jnp.where(left_p, 0.0, kdt), a_bd[CHUNK:PAIR]], axis=0).astype(BF16)
        pre.append((xm, qd, (k2f * e_g).astype(BF16)))
    ts = [eye + xm for xm, _, _ in pre]
    ps = [mm([xm], xm)[0] for xm, _, _ in pre]
    for _ in range(4):
        both = [mm([t, pw], pw) for t, pw in zip(ts, ps)]
        ts = [t + tp for t, (tp, _) in zip(ts, both)]
        ps = [pp for _, pp in both]
    ts = [t + mm([t], pw)[0] for t, pw in zip(ts, ps)]
    for (bb, hh, p), t, (_, qd, kg) in zip(items, ts, pre):
        cs = slice(hh * LANES, (hh + 1) * LANES)
        rs = slice(p * PAIR, (p + 1) * PAIR)
        tb = bdiag(t * bgt_ref[bb, hh:hh + 1, rs]).astype(BF16)
        uw = _dot(tb, jnp.concatenate([v_ref[bb, rs, cs], kg], axis=1))
        u_s[bb, hh, rs, :] = uw[:, 0:GDN_DV]
        w2 = uw[:, GDN_DV:GDN_DV + GDN_DK].astype(BF16)
        for e in range(2):
            es = slice(e * CHUNK, (e + 1) * CHUNK)
            wq_s[bb, hh, 2 * p + e] = jnp.concatenate([w2[es], qd[es]], axis=0)

    chains = [(bb, hh) for bb in range(nb) for hh in range(GDN_HEADS)]
    for c in range(nc):
        e = c % 2
        rs = slice(c * CHUNK, (c + 1) * CHUNK)
        es = slice(e * CHUNK, (e + 1) * CHUNK)
        sts = [state[ch] for ch in chains]
        rr = [_dot(wq_s[bb, hh, c], st.astype(BF16)) for (bb, hh), st in zip(chains, sts)]
        for (bb, hh), r in zip(chains, rr):
            vn_s[bb, hh, es, :] = (u_s[bb, hh, rs, :] - r[0:CHUNK]).astype(BF16)
        for (bb, hh), st, r in zip(chains, sts, rr):
            cs = slice(hh * LANES, (hh + 1) * LANES)
            g_end = bgt_ref[bb, GDN_HEADS + hh:GDN_HEADS + hh + 1, (c + 1) * CHUNK - 1:(c + 1) * CHUNK]
            kva = _dot(kva_s[bb, hh, c], vn_s[bb, hh])
            state[bb, hh] = st * jnp.exp(g_end) + kva[0:GDN_DK]
            o = r[CHUNK:PAIR] + kva[GDN_DK:GDN_DK + CHUNK]
            on = o * lax.rsqrt(jnp.mean(o * o, axis=-1, keepdims=True) + NORM_EPS) * ng_ref[...]
            o_ref[bb, rs, cs] = (on * gate_ref[bb, rs, cs].astype(F32)).astype(BF16)


def _gdn(gq, gk, gv, agate, bg, bgt, ng_row):
    b, s, _ = gq.shape
    tc = GDN_TC
    row = lambda si: (0, si, 0)
    hw_spec = pl.BlockSpec((b, tc, HEADW), row)
    per_chain = (b, GDN_HEADS)
    return pl.pallas_call(
        _gdn_kernel,
        grid=(s // tc,),
        in_specs=[hw_spec, hw_spec, hw_spec, hw_spec,
                  pl.BlockSpec((b, tc, LANES), row),
                  pl.BlockSpec((b, 8, tc), lambda si: (0, 0, si)),
                  pl.BlockSpec((1, GDN_DV), lambda si: (0, 0))],
        out_specs=hw_spec,
        out_shape=jax.ShapeDtypeStruct((b, s, HEADW), BF16),
        scratch_shapes=[pltpu.VMEM(per_chain + (GDN_DK, GDN_DV), F32),
                        pltpu.VMEM(per_chain + (tc // CHUNK, PAIR, GDN_DK), BF16),
                        pltpu.VMEM(per_chain + (tc, GDN_DV), F32),
                        pltpu.VMEM(per_chain + (tc // CHUNK, GDN_DK + CHUNK, PAIR), BF16),
                        pltpu.VMEM(per_chain + (PAIR, GDN_DV), BF16)],
        compiler_params=pltpu.CompilerParams(
            dimension_semantics=("arbitrary",), vmem_limit_bytes=VMEM_LIMIT),
        name="gdn",
    )(gq, gk, gv, agate, bg, bgt, ng_row)


def _attn_kernel(q_ref, k_ref, v_ref, gate_ref, lamp_ref, sg_ref, gqk_ref, o_ref,
                 vext, acc, pbuf, qs_s, kaug_s, *, lambda_init, tq):
    s_len = q_ref.shape[0]
    nq = s_len // tq
    tk = tq
    nch = tq // CHUNK
    w = 2 * DIFF_D

    vext[:, 0:w] = v_ref[...]
    vext[:, w:2 * w] = jnp.ones((s_len, w), BF16)
    acc[...] = jnp.zeros(acc.shape, F32)

    lane = lax.broadcasted_iota(jnp.int32, (tk, w), 1)
    rchunk = lax.broadcasted_iota(jnp.int32, (tk, w), 0) // CHUNK
    lo = lane < DIFF_D
    hi = jnp.logical_not(lo)
    half = (lo, hi)

    gqk = gqk_ref[...]
    bound = (jnp.max(jnp.abs(gqk[0:1, :])) * jnp.max(jnp.abs(gqk[1:2, :]))
             * (DIFF_D * DIFF_D ** -0.5 * math.log2(math.e) * 1.01))

    lp = lamp_ref[...]
    lam = (jnp.exp(jnp.sum(lp[0:1, :] * lp[1:2, :], axis=-1, keepdims=True))
           - jnp.exp(jnp.sum(lp[2:3, :] * lp[3:4, :], axis=-1, keepdims=True)) + lambda_init)

    def rows_of(blk):
        return pl.ds(pl.multiple_of(blk * tq, tq), tq)

    def finalize(qi):
        o = acc[0, :, 0:w] / acc[0, :, w:2 * w] - lam * (acc[1, :, 0:w] / acc[1, :, w:2 * w])
        on = o * lax.rsqrt(jnp.mean(o * o, axis=-1, keepdims=True) + NORM_EPS) * sg_ref[...]
        on = on * (1.0 - lambda_init)
        o_ref[rows_of(qi), :] = (on * gate_ref[rows_of(qi), :].astype(F32)).astype(BF16)
        acc[...] = jnp.zeros(acc.shape, F32)

    @pl.when(bound <= MAX_STATIC_SHIFT)
    def _():
        q_aug = []
        for sub, base in enumerate((DIFF_D, 0)):
            mlane = lane - (base + 1)
            in_mask = (mlane >= 0) & (mlane < nch)
            q_aug.append(jnp.where((lane == base) | (in_mask & (rchunk == mlane)), 1.0, 0.0).astype(BF16))
            plain = jnp.where(lane == base, -bound, 0.0)
            kaug_s[0, sub] = plain.astype(BF16)
            kaug_s[1, sub] = jnp.where(in_mask & (rchunk > mlane), -MASK_BIG, plain).astype(BF16)

        def prep_q(qi, qslot):
            q = q_ref[rows_of(qi), :]
            for sub in range(2):
                qs_s[qslot, sub] = jnp.where(half[sub], q, q_aug[sub])

        def scores(qslot, j, is_diag, pslot):
            kb = k_ref[rows_of(j), :]
            for sub in range(2):
                ks = jnp.where(half[sub], kb, kaug_s[is_diag, sub])
                pbuf[pslot, sub] = jnp.exp2(_dot_nt(qs_s[qslot, sub], ks)).astype(BF16)

        def accumulate(j, pslot):
            ve = vext[rows_of(j), :]
            for sub in range(2):
                acc[sub] += _dot(pbuf[pslot, sub], ve)

        prep_q(0, 0)
        scores(0, 0, 1, 0)

        def outer(qi, n):
            qslot = qi & 1

            def advance(j, n):
                accumulate(j - 1, n & 1)
                scores(qslot, j, jnp.asarray(j == qi, jnp.int32), (n + 1) & 1)

            def run(j0, count, n):
                for u in range(count):
                    advance(j0 + u, n + u)

            def trip(t, n):
                run(ATT_UNROLL * t + 1, ATT_UNROLL, n)
                return n + ATT_UNROLL

            n = lax.fori_loop(0, qi // ATT_UNROLL, trip, n)
            j = (qi // ATT_UNROLL) * ATT_UNROLL + 1
            part = ATT_UNROLL // 2
            while part:
                @pl.when((qi & part) != 0)
                def _(j=j, n=n, part=part):
                    run(j, part, n)

                j = j + (qi & part)
                n = n + (qi & part)
                part //= 2
            nxt = jnp.minimum(qi + 1, nq - 1)
            prep_q(nxt, 1 - qslot)
            accumulate(qi, n & 1)
            scores(1 - qslot, 0, jnp.asarray(nxt == 0, jnp.int32), (n + 1) & 1)
            finalize(qi)
            return n + 1

        lax.fori_loop(0, nq, outer, 0)

    @pl.when(bound > MAX_STATIC_SHIFT)
    def _():
        zero = jnp.zeros((tk, w), BF16)
        rr = lax.broadcasted_iota(jnp.int32, (tq, tk), 0) // CHUNK
        cc = lax.broadcasted_iota(jnp.int32, (tq, tk), 1) // CHUNK
        causal = cc <= rr

        def outer(qi, carry):
            q = q_ref[rows_of(qi), :]

            def step(j, ms, diag):
                kb = k_ref[rows_of(j), :]
                ve = vext[rows_of(j), :]
                new = []
                for sub in range(2):
                    s = _dot_nt(q, jnp.where(half[sub], kb, zero))
                    if diag:
                        s = jnp.where(causal, s, NEG_BIG)
                    m_new = jnp.maximum(ms[sub], jnp.max(s, axis=-1, keepdims=True))
                    p = jnp.exp2(s - m_new)
                    acc[sub] = jnp.exp2(ms[sub] - m_new) * acc[sub] + _dot(p.astype(BF16), ve)
                    new.append(m_new)
                return tuple(new)

            init = (jnp.full((tq, 1), NEG_BIG, F32), jnp.full((tq, 1), NEG_BIG, F32))
            ms = lax.fori_loop(0, qi, lambda j, c: step(j, c, False), init)
            step(qi, ms, True)
            finalize(qi)
            return carry

        lax.fori_loop(0, nq, outer, 0)


def _attn(bq, bk, bv, bgate, lam_params, sg_row, gqk_rows, lambda_init):
    b, s, _ = bq.shape
    tq = ATT_TQ
    w = 2 * DIFF_D
    hspec = pl.BlockSpec((None, s, w), lambda bi, hi: (bi, 0, hi))
    const = lambda bi, hi: (0, 0)
    return pl.pallas_call(
        functools.partial(_attn_kernel, lambda_init=lambda_init, tq=tq),
        grid=(b, DIFF_HEADS),
        in_specs=[hspec, hspec, hspec, hspec,
                  pl.BlockSpec((8, LANES), const),
                  pl.BlockSpec((1, w), const),
                  pl.BlockSpec((2, LANES), const)],
        out_specs=hspec,
        out_shape=jax.ShapeDtypeStruct((b, s, HEADW), BF16),
        scratch_shapes=[pltpu.VMEM((s, 2 * w), BF16),
                        pltpu.VMEM((2, tq, 2 * w), F32),
                        pltpu.VMEM((2, 2, tq, tq), BF16),
                        pltpu.VMEM((2, 2, tq, w), BF16),
                        pltpu.VMEM((2, 2, tq, w), BF16)],
        compiler_params=pltpu.CompilerParams(
            dimension_semantics=("arbitrary", "arbitrary"), vmem_limit_bytes=VMEM_LIMIT),
        name="attn",
    )(bq, bk, bv, bgate, lam_params, sg_row, gqk_rows)


def _out_kernel(x_ref, ma_ref, mb_ref, wa_ref, wb_ref, gate_ref, o_ref):
    y = _dot(ma_ref[...], wa_ref[...]) + _dot(mb_ref[...], wb_ref[...])
    o_ref[...] = x_ref[...] + gate_ref[...] * y


def _out(x, mixed_a, mixed_b, w_a, w_b, gate):
    b, s, d = x.shape
    tm = OUT_TM
    row = lambda bi, si: (bi, si, 0)
    const2 = lambda bi, si: (0, 0)
    return pl.pallas_call(
        _out_kernel,
        grid=(b, s // tm),
        in_specs=[pl.BlockSpec((None, tm, d), row),
                  pl.BlockSpec((None, tm, HEADW), row),
                  pl.BlockSpec((None, tm, HEADW), row),
                  pl.BlockSpec((HEADW, d), const2),
                  pl.BlockSpec((HEADW, d), const2),
                  pl.BlockSpec((None, 1, d), lambda bi, si: (bi, 0, 0))],
        out_specs=pl.BlockSpec((None, tm, d), row),
        out_shape=jax.ShapeDtypeStruct((b, s, d), F32),
        compiler_params=pltpu.CompilerParams(
            dimension_semantics=("arbitrary", "arbitrary"), vmem_limit_bytes=VMEM_LIMIT),
        name="out",
    )(x, mixed_a, mixed_b, w_a, w_b, gate)


def _constants():
    lane = jnp.arange(LANES)
    sub = lane % DIFF_D
    half = ROT_DIM // 2
    inv_freq = ROPE_THETA ** (-jnp.arange(0, ROT_DIM, 2, dtype=F32) / ROT_DIM)
    freq_row = jnp.where(sub < ROT_DIM, inv_freq[sub % half], 0.0).astype(F32)[None, :]
    sgn_rows = jnp.stack([jnp.where(sub < half, -1.0, 0.0),
                          jnp.where((sub >= half) & (sub < ROT_DIM), 1.0, 0.0)]).astype(F32)
    r = jnp.arange(PROJ_TM)
    lmat = ((r[:, None] // CHUNK == r[None, :] // CHUNK) & (r[None, :] <= r[:, None])).astype(BF16)
    g = jnp.arange(GMAT_W)
    gmat = jnp.where(g[:, None] // DIFF_D == g[None, :] // DIFF_D, 1.0 / DIFF_D, 0.0).astype(BF16)
    return freq_row, sgn_rows, lmat, gmat


def _pad_lanes(v, offset):
    return jnp.zeros((1, LANES), F32).at[0, offset:offset + v.shape[0]].set(v.astype(F32))


def _layer(x, c_pad, pos3, lambda_init, norm_g, w_ada, b_ada, w_in, conv_w, a_log, dt_bias, gdn_norm_g,
           q_norm_g, k_norm_g, lam_q1, lam_k1, lam_q2, lam_k2, subln_g, w_out):
    b, s, d = x.shape
    mod = _ada(c_pad, w_ada, b_ada[None, :])[:b]
    shift = mod[:, None, 0:d]
    scale = mod[:, None, d:2 * d]
    gate = mod[:, None, 2 * d:3 * d]

    nk = GDN_HEADS * GDN_DK
    o_beta = 2 * nk + GDN_HEADS * GDN_DV
    o_agate = o_beta + 2 * GDN_HEADS
    o_b = o_agate + GDN_HEADS * GDN_DV
    bd = jnp.pad(w_in[:, o_beta:o_agate], ((0, 0), (0, LANES - 2 * GDN_HEADS)))
    w_all = jnp.concatenate([
        w_in[:, 0:o_beta], w_in[:, o_agate:o_b],
        w_in[:, o_b:o_b + HEADW], w_in[:, o_b + HEADW:o_b + 2 * HEADW],
        w_in[:, o_b + 2 * HEADW:o_b + 3 * HEADW], w_in[:, o_b + 3 * HEADW:o_b + 4 * HEADW],
        bd], axis=1).astype(BF16)
    freq_row, sgn_rows, lmat, gmat = _constants()
    lam_params = jnp.zeros((8, LANES), F32)
    for r, v in enumerate((lam_q1, lam_k1, lam_q2, lam_k2)):
        lam_params = lam_params.at[r, 0:DIFF_D].set(v.astype(F32))

    gq, gk, gv, agate, bq, bk, bv, bgate, bg, bgt = _proj(
        x, pos3, shift, scale, norm_g[None, :], w_all, conv_w,
        _pad_lanes(a_log, GDN_HEADS), _pad_lanes(dt_bias, GDN_HEADS),
        jnp.tile(q_norm_g, 2 * DIFF_HEADS)[None, :], jnp.tile(k_norm_g, 2 * DIFF_HEADS)[None, :],
        lmat, gmat, freq_row, sgn_rows)
    mixed_a = _gdn(gq, gk, gv, agate, bg, bgt, gdn_norm_g[None, :])
    gqk_rows = jnp.concatenate([_pad_lanes(q_norm_g, 0), _pad_lanes(k_norm_g, 0)], axis=0)
    mixed_b = _attn(bq, bk, bv, bgate, lam_params, subln_g[None, :], gqk_rows, lambda_init)
    w_o = w_out.astype(BF16)
    return _out(x, mixed_a, mixed_b, w_o[:HEADW], w_o[HEADW:], gate)


def kernel(x, c, positions, norm_g, w_ada, b_ada, w_in, conv_w, a_log, dt_bias, gdn_norm_g, q_norm_g,
           k_norm_g, lambda_q1, lambda_k1, lambda_q2, lambda_k2, subln_g, w_out):
    b = x.shape[0]
    c_pad = jnp.pad(c, ((0, 8 - b % 8 if b % 8 else 0), (0, 0)))
    pos3 = positions[:, :, None]
    for l in range(norm_g.shape[0]):
        lambda_init = 0.8 - 0.6 * math.exp(-0.3 * l)
        x = _layer(x, c_pad, pos3, lambda_init, norm_g[l], w_ada[l], b_ada[l], w_in[l], conv_w[l],
                   a_log[l], dt_bias[l], gdn_norm_g[l], q_norm_g[l], k_norm_g[l], lambda_q1[l],
                   lambda_k1[l], lambda_q2[l], lambda_k2[l], subln_g[l], w_out[l])
    return x
```

```python
import functools
import math

import jax
import jax.numpy as jnp
from jax import lax
from jax.experimental import pallas as pl
from jax.experimental.pallas import tpu as pltpu

F32 = jnp.float32
BF16 = jnp.bfloat16
HIGHEST = lax.Precision.HIGHEST

CHUNK = 64
PAIR = 2 * CHUNK
GDN_HEADS = 4
GDN_DK = 128
GDN_DV = 128
CONV_K = 4
DIFF_HEADS = 4
DIFF_D = 64
ROT_DIM = 16
ROPE_THETA = 500000.0
NORM_EPS = 1e-6
LANES = 128
NEG_BIG = -1e30
MASK_BIG = 256.0

GDN_QKV = 3 * GDN_HEADS * GDN_DK
HEADW = 512
OFF_GQKV = 0
OFF_AGATE = GDN_QKV
OFF_BQ = OFF_AGATE + HEADW
OFF_BK = OFF_BQ + HEADW
OFF_BV = OFF_BK + HEADW
OFF_BGATE = OFF_BV + HEADW
OFF_BD = OFF_BGATE + HEADW
W_COLS = OFF_BD + LANES

PROJ_TM = 512
PROJ_RB = 128
GMAT_W = 256
GDN_TC = 128
ATT_TQ = 512
ATT_UNROLL = 4
MAX_STATIC_SHIFT = 60.0
OUT_TM = 1024
VMEM_LIMIT = 48 * 1024 * 1024


def _silu(x):
    h = 0.5 * x
    return h + h * jnp.tanh(h)


def _dot(a, b, **kw):
    return jnp.dot(a, b, preferred_element_type=F32, **kw)


def _dot_nt(a, b, **kw):
    return lax.dot_general(a, b, (((1,), (1,)), ((), ())), preferred_element_type=F32, **kw)


def _dot_tn(a, b, **kw):
    return lax.dot_general(a, b, (((0,), (0,)), ((), ())), preferred_element_type=F32, **kw)


def _ada_kernel(c_ref, w_ref, b_ref, o_ref):
    o_ref[...] = _dot(_silu(c_ref[...]), w_ref[...], precision=HIGHEST) + b_ref[...]


def _ada(c_pad, w_ada, b_ada):
    rows, d = c_pad.shape
    n = w_ada.shape[1]
    tn = 512
    return pl.pallas_call(
        _ada_kernel,
        grid=(n // tn,),
        in_specs=[pl.BlockSpec((rows, d), lambda j: (0, 0)),
                  pl.BlockSpec((d, tn), lambda j: (0, j)),
                  pl.BlockSpec((1, tn), lambda j: (0, j))],
        out_specs=pl.BlockSpec((rows, tn), lambda j: (0, j)),
        out_shape=jax.ShapeDtypeStruct((rows, n), F32),
        name="ada",
    )(c_pad, w_ada, b_ada)


def _proj_kernel(x_ref, pos_ref, shift_ref, scale_ref, ng_ref, w_ref, convw_ref, alog_ref, dtb_ref,
                 qg_ref, kg_ref, lmat_ref, gmat_ref, freq_ref, sgn_ref,
                 gq_ref, gk_ref, gv_ref, ag_ref, bq_ref, bk_ref, bv_ref, bgate_ref, bg_ref, bgt_ref,
                 cq, ck, cv, z0, z1, h_s, gm_s, tab):
    tm = x_ref.shape[0]
    x = x_ref[...]
    ms = jnp.mean(x * x, axis=-1, keepdims=True)
    gain = ng_ref[...] * (1.0 + scale_ref[...])
    h_s[...] = (x * lax.rsqrt(ms + NORM_EPS) * gain + shift_ref[...]).astype(BF16)

    cbufs = (cq, ck, cv)
    zbuf = (z0, z1)

    @pl.when(pl.program_id(1) == 0)
    def _():
        for cb in cbufs:
            cb[0:8, :] = jnp.zeros((8, HEADW), F32)

    def mm_pieces(dst, row0, off, width):
        def piece(c0):
            c1 = min(c0 + GMAT_W, width)
            dst[row0:row0 + tm, c0:c1] = _dot(h_s[...], w_ref[:, off + c0:off + c1])
        return [functools.partial(piece, c0) for c0 in range(0, width, GMAT_W)]

    def conv_pieces(g, out_ref):
        cb = cbufs[g]

        def piece(hh):
            hs = slice(hh * LANES, (hh + 1) * LANES)
            cs = slice(g * HEADW + hh * LANES, g * HEADW + (hh + 1) * LANES)
            for r0 in range(0, tm, PROJ_RB):
                acc = convw_ref[3:4, cs] * cb[8 + r0:8 + r0 + PROJ_RB, hs]
                for j in range(CONV_K - 1):
                    acc = acc + convw_ref[j:j + 1, cs] * cb[5 + j + r0:5 + j + r0 + PROJ_RB, hs]
                a = _silu(acc)
                if g < 2:
                    a = a * lax.rsqrt(jnp.sum(a * a, axis=-1, keepdims=True) + NORM_EPS)
                    if g == 0:
                        a = a * (GDN_DK ** -0.5)
                out_ref[r0:r0 + PROJ_RB, hs] = a.astype(BF16)
            cb[0:8, hs] = cb[tm:tm + 8, hs]
        return [functools.partial(piece, hh) for hh in range(GDN_HEADS)]

    def gate_pieces(slot, out_ref):
        def piece(r0):
            rs = slice(r0, r0 + PROJ_RB)
            out_ref[rs, :] = _silu(zbuf[slot][rs, :]).astype(BF16)
        return [functools.partial(piece, r0) for r0 in range(0, tm, PROJ_RB)]

    def qk_pieces(slot, g_ref, out_ref, post):
        def piece(c0):
            z = zbuf[slot][:, c0:c0 + GMAT_W]
            gm_s[:, c0:c0 + GMAT_W] = _dot((z * z).astype(BF16), gmat_ref[...])
            for r0 in range(0, tm, PROJ_RB):
                rs = slice(r0, r0 + PROJ_RB)
                for h0 in range(c0, c0 + GMAT_W, LANES):
                    hs = slice(h0, h0 + LANES)
                    ys = zbuf[slot][rs, hs] * lax.rsqrt(gm_s[rs, hs] + NORM_EPS) * g_ref[:, hs]
                    r = (ys * tab[0, rs, :] + pltpu.roll(ys, LANES - ROT_DIM // 2, axis=1) * tab[1, rs, :]
                         + pltpu.roll(ys, ROT_DIM // 2, axis=1) * tab[2, rs, :])
                    if post is not None:
                        r = r * post
                    out_ref[rs, hs] = r.astype(BF16)
        return [functools.partial(piece, c0) for c0 in range(0, HEADW, GMAT_W)]

    def table_pieces():
        def piece(r0):
            rs = slice(r0, r0 + PROJ_RB)
            ang = pos_ref[rs, :].astype(F32) * freq_ref[...]
            sin_t = jnp.sin(ang)
            tab[0, rs, :] = jnp.cos(ang)
            tab[1, rs, :] = sin_t * sgn_ref[0:1, :]
            tab[2, rs, :] = sin_t * sgn_ref[1:2, :]
        return [functools.partial(piece, r0) for r0 in range(0, tm, PROJ_RB)]

    def ep_bd(slot):
        z = zbuf[slot][:, 0:LANES]
        lane = lax.broadcasted_iota(jnp.int32, (1, LANES), 1)
        beta = 1.0 / (1.0 + jnp.exp(-z))
        zz = z + dtb_ref[...]
        softplus = jnp.maximum(zz, 0.0) + jnp.log(1.0 + jnp.exp(-jnp.abs(zz)))
        is_g = (lane >= GDN_HEADS) & (lane < 2 * GDN_HEADS)
        g = jnp.where(is_g, -jnp.exp(alog_ref[...]) * softplus, 0.0)
        g_hi = g.astype(BF16)
        g_lo = (g - g_hi.astype(F32)).astype(BF16)
        gc = _dot(lmat_ref[...], g_hi) + _dot(lmat_ref[...], g_lo)
        bg = jnp.where(lane < GDN_HEADS, beta, gc)
        bg_ref[...] = bg
        bgt_ref[...] = bg.T[0:8, :]

    def cast_bv():
        bv_ref[...] = zbuf[0][...].astype(BF16)

    def emit(*lists):
        for k in range(max(len(pl_) for pl_ in lists)):
            for pl_ in lists:
                if k < len(pl_):
                    pl_[k]()

    q_scale = DIFF_D ** -0.5 * math.log2(math.e)
    stages = [
        (mm_pieces(cq, 8, OFF_GQKV, HEADW), conv_pieces(0, gq_ref)),
        (mm_pieces(z1, 0, OFF_BD, LANES) + mm_pieces(ck, 8, OFF_GQKV + HEADW, HEADW),
         conv_pieces(1, gk_ref) + [functools.partial(ep_bd, 1)]),
        (mm_pieces(cv, 8, OFF_GQKV + 2 * HEADW, HEADW), conv_pieces(2, gv_ref)),
        (mm_pieces(z0, 0, OFF_BQ, HEADW), qk_pieces(0, qg_ref, bq_ref, q_scale)),
        (mm_pieces(z1, 0, OFF_BK, HEADW), qk_pieces(1, kg_ref, bk_ref, None)),
        (mm_pieces(z0, 0, OFF_AGATE, HEADW), gate_pieces(0, ag_ref)),
        (mm_pieces(z1, 0, OFF_BGATE, HEADW), gate_pieces(1, bgate_ref)),
        (mm_pieces(z0, 0, OFF_BV, HEADW), [cast_bv]),
    ]
    emit(stages[0][0], table_pieces())
    for i, (_, epilogue) in enumerate(stages):
        emit(stages[i + 1][0] if i + 1 < len(stages) else [], epilogue)


def _proj(x, pos3, shift, scale, norm_g, w_all, conv_w, alog_row, dtb_row, qg_row, kg_row,
          lmat, gmat, freq_row, sgn_rows):
    b, s, d = x.shape
    tm = PROJ_TM
    row = lambda bi, si: (bi, si, 0)
    const2 = lambda bi, si: (0, 0)
    per_b = lambda bi, si: (bi, 0, 0)
    hw = jax.ShapeDtypeStruct((b, s, HEADW), BF16)
    out_shape = [hw] * 8 + [jax.ShapeDtypeStruct((b, s, LANES), F32),
                            jax.ShapeDtypeStruct((b, 8, s), F32)]
    hw_spec = pl.BlockSpec((None, tm, HEADW), row)
    out_specs = [hw_spec] * 8 + [pl.BlockSpec((None, tm, LANES), row),
                                 pl.BlockSpec((None, 8, tm), lambda bi, si: (bi, 0, si))]
    in_specs = [
        pl.BlockSpec((None, tm, d), row),
        pl.BlockSpec((None, tm, 1), row),
        pl.BlockSpec((None, 1, d), per_b),
        pl.BlockSpec((None, 1, d), per_b),
        pl.BlockSpec((1, d), const2),
        pl.BlockSpec((d, W_COLS), const2),
        pl.BlockSpec((CONV_K, GDN_QKV), const2),
        pl.BlockSpec((1, LANES), const2),
        pl.BlockSpec((1, LANES), const2),
        pl.BlockSpec((1, HEADW), const2),
        pl.BlockSpec((1, HEADW), const2),
        pl.BlockSpec((tm, tm), const2),
        pl.BlockSpec((GMAT_W, GMAT_W), const2),
        pl.BlockSpec((1, LANES), const2),
        pl.BlockSpec((2, LANES), const2),
    ]
    return pl.pallas_call(
        _proj_kernel,
        grid=(b, s // tm),
        in_specs=in_specs,
        out_specs=out_specs,
        out_shape=out_shape,
        scratch_shapes=[pltpu.VMEM((tm + 8, HEADW), F32)] * 3 + [pltpu.VMEM((tm, HEADW), F32)] * 2
        + [pltpu.VMEM((tm, d), BF16), pltpu.VMEM((tm, HEADW), F32), pltpu.VMEM((3, tm, LANES), F32)],
        compiler_params=pltpu.CompilerParams(
            dimension_semantics=("arbitrary", "arbitrary"), vmem_limit_bytes=VMEM_LIMIT),
        name="proj",
    )(x, pos3, shift, scale, norm_g, w_all, conv_w, alog_row, dtb_row, qg_row, kg_row,
      lmat, gmat, freq_row, sgn_rows)


def _split(a):
    hi = a.astype(BF16)
    return hi, (a - hi.astype(F32)).astype(BF16)


def _gdn_kernel(q_ref, k_ref, v_ref, gate_ref, bg_ref, bgt_ref, ng_ref, o_ref,
                state, wq_s, u_s, kva_s, vn_s):
    nb, tc = q_ref.shape[0], q_ref.shape[1]
    npair = tc // PAIR
    nc = tc // CHUNK

    @pl.when(pl.program_id(0) == 0)
    def _():
        state[...] = jnp.zeros(state.shape, F32)
        vn_s[...] = jnp.zeros(vn_s.shape, BF16)

    ii = lax.broadcasted_iota(jnp.int32, (CHUNK, PAIR), 0)
    lane_w = lax.broadcasted_iota(jnp.int32, (CHUNK, PAIR), 1)
    jj = lane_w % CHUNK
    left_w = lane_w < CHUNK
    left_p = lax.broadcasted_iota(jnp.int32, (PAIR, PAIR), 1) < CHUNK
    top_col = lax.broadcasted_iota(jnp.int32, (PAIR, 1), 0) < CHUNK
    eye = (ii == jj).astype(F32)
    lower = ii >= jj
    strict = ii > jj

    def widen(m):
        return jnp.where(left_w, m[0:CHUNK], m[CHUNK:PAIR])

    def bdiag(wd):
        z = jnp.zeros_like(wd)
        return jnp.concatenate([jnp.where(left_w, wd, z), jnp.where(left_w, z, wd)], axis=0)

    def mm(lhs, rhs):
        n = len(lhs)
        b_hi, b_lo = _split(bdiag(rhs))
        parts = [_split(a) for a in lhs]
        his = [hi_ for hi_, _ in parts]
        r_hi = _dot(jnp.concatenate(his + [lo_ for _, lo_ in parts], axis=0), b_hi)
        r_lo = _dot(jnp.concatenate(his, axis=0) if n > 1 else his[0], b_lo)
        blk = lambda r, i: r[i * CHUNK:(i + 1) * CHUNK]
        return [blk(r_hi, i) + blk(r_hi, n + i) + blk(r_lo, i) for i in range(n)]

    items = [(bb, hh, p) for bb in range(nb) for hh in range(GDN_HEADS) for p in range(npair)]
    pre = []
    for bb, hh, p in items:
        cs = slice(hh * LANES, (hh + 1) * LANES)
        rs = slice(p * PAIR, (p + 1) * PAIR)
        q2, k2 = q_ref[bb, rs, cs], k_ref[bb, rs, cs]
        gcol2 = bg_ref[bb, rs, GDN_HEADS + hh:GDN_HEADS + hh + 1]
        bcol_w = widen(jnp.broadcast_to(bg_ref[bb, rs, hh:hh + 1], (PAIR, PAIR)))
        gcol_w = widen(jnp.broadcast_to(gcol2, (PAIR, PAIR)))
        grow = bgt_ref[bb, GDN_HEADS + hh:GDN_HEADS + hh + 1, rs]
        dec = jnp.where(lower, jnp.exp(jnp.minimum(gcol_w - grow, 0.0)), 0.0)
        gram = _dot_nt(jnp.concatenate([k2, q2], axis=0), k2)
        xm = jnp.where(strict, -(bcol_w * widen(gram[0:PAIR]) * dec), 0.0)
        a_w = jnp.where(lower, widen(gram[PAIR:2 * PAIR]) * dec, 0.0)
        pre.append((xm, a_w))

    def decayed_operands(item, a_w):
        bb, hh, p = item
        cs = slice(hh * LANES, (hh + 1) * LANES)
        rs = slice(p * PAIR, (p + 1) * PAIR)
        q2, k2 = q_ref[bb, rs, cs], k_ref[bb, rs, cs]
        gcol2 = bg_ref[bb, rs, GDN_HEADS + hh:GDN_HEADS + hh + 1]
        grow = bgt_ref[bb, GDN_HEADS + hh:GDN_HEADS + hh + 1, rs]
        g_end = jnp.where(top_col, grow[:, CHUNK - 1:CHUNK], grow[:, PAIR - 1:PAIR])
        e_g = jnp.exp(gcol2)
        k2f = k2.astype(F32)
        qd = (q2.astype(F32) * e_g).astype(BF16)
        kdt = (k2f * jnp.exp(g_end - gcol2)).T
        a_bd = bdiag(a_w)
        kva_s[bb, hh, 2 * p] = jnp.concatenate(
            [jnp.where(left_p, kdt, 0.0), a_bd[0:CHUNK]], axis=0).astype(BF16)
        kva_s[bb, hh, 2 * p + 1] = jnp.concatenate(
            [jnp.where(left_p, 0.0, kdt), a_bd[CHUNK:PAIR]], axis=0).astype(BF16)
        return qd, (k2f * e_g).astype(BF16)

    levels = 4
    per_level = -(-len(items) // levels)
    side = []
    ts = [eye + xm for xm, _ in pre]
    ps = [mm([xm], xm)[0] for xm, _ in pre]
    for lvl in range(levels):
        both = [mm([t, pw], pw) for t, pw in zip(ts, ps)]
        for idx in range(lvl * per_level, min((lvl + 1) * per_level, len(items))):
            side.append(decayed_operands(items[idx], pre[idx][1]))
        ts = [t + tp for t, (tp, _) in zip(ts, both)]
        ps = [pp for _, pp in both]
    ts = [t + mm([t], pw)[0] for t, pw in zip(ts, ps)]
    for (bb, hh, p), t, (qd, kg) in zip(items, ts, side):
        cs = slice(hh * LANES, (hh + 1) * LANES)
        rs = slice(p * PAIR, (p + 1) * PAIR)
        tb = bdiag(t * bgt_ref[bb, hh:hh + 1, rs]).astype(BF16)
        uw = _dot(tb, jnp.concatenate([v_ref[bb, rs, cs], kg], axis=1))
        u_s[bb, hh, rs, :] = uw[:, 0:GDN_DV]
        w2 = uw[:, GDN_DV:GDN_DV + GDN_DK].astype(BF16)
        for e in range(2):
            es = slice(e * CHUNK, (e + 1) * CHUNK)
            wq_s[bb, hh, 2 * p + e] = jnp.concatenate([w2[es], qd[es]], axis=0)

    chains = [(bb, hh) for bb in range(nb) for hh in range(GDN_HEADS)]
    for c in range(nc):
        e = c % 2
        rs = slice(c * CHUNK, (c + 1) * CHUNK)
        es = slice(e * CHUNK, (e + 1) * CHUNK)
        sts = [state[ch] for ch in chains]
        rr = [_dot(wq_s[bb, hh, c], st.astype(BF16)) for (bb, hh), st in zip(chains, sts)]
        for (bb, hh), r in zip(chains, rr):
            vn_s[bb, hh, es, :] = (u_s[bb, hh, rs, :] - r[0:CHUNK]).astype(BF16)
        for (bb, hh), st, r in zip(chains, sts, rr):
            cs = slice(hh * LANES, (hh + 1) * LANES)
            g_end = bgt_ref[bb, GDN_HEADS + hh:GDN_HEADS + hh + 1, (c + 1) * CHUNK - 1:(c + 1) * CHUNK]
            kva = _dot(kva_s[bb, hh, c], vn_s[bb, hh])
            state[bb, hh] = st * jnp.exp(g_end) + kva[0:GDN_DK]
            o = r[CHUNK:PAIR] + kva[GDN_DK:GDN_DK + CHUNK]
            on = o * lax.rsqrt(jnp.mean(o * o, axis=-1, keepdims=True) + NORM_EPS) * ng_ref[...]
            o_ref[bb, rs, cs] = (on * gate_ref[bb, rs, cs].astype(F32)).astype(BF16)


def _gdn(gq, gk, gv, agate, bg, bgt, ng_row):
    b, s, _ = gq.shape
    tc = GDN_TC
    row = lambda si: (0, si, 0)
    hw_spec = pl.BlockSpec((b, tc, HEADW), row)
    per_chain = (b, GDN_HEADS)
    return pl.pallas_call(
        _gdn_kernel,
        grid=(s // tc,),
        in_specs=[hw_spec, hw_spec, hw_spec, hw_spec,
                  pl.BlockSpec((b, tc, LANES), row),
                  pl.BlockSpec((b, 8, tc), lambda si: (0, 0, si)),
                  pl.BlockSpec((1, GDN_DV), lambda si: (0, 0))],
        out_specs=hw_spec,
        out_shape=jax.ShapeDtypeStruct((b, s, HEADW), BF16),
        scratch_shapes=[pltpu.VMEM(per_chain + (GDN_DK, GDN_DV), F32),
                        pltpu.VMEM(per_chain + (tc // CHUNK, PAIR, GDN_DK), BF16),
                        pltpu.VMEM(per_chain + (tc, GDN_DV), F32),
                        pltpu.VMEM(per_chain + (tc // CHUNK, GDN_DK + CHUNK, PAIR), BF16),
                        pltpu.VMEM(per_chain + (PAIR, GDN_DV), BF16)],
        compiler_params=pltpu.CompilerParams(
            dimension_semantics=("arbitrary",), vmem_limit_bytes=VMEM_LIMIT),
        name="gdn",
    )(gq, gk, gv, agate, bg, bgt, ng_row)


def _attn_kernel(q_ref, k_ref, v_ref, gate_ref, lamp_ref, sg_ref, gqk_ref, o_ref,
                 vext, acc, pbuf, qs_s, kaug_s, *, lambda_init, tq):
    s_len = q_ref.shape[0]
    nq = s_len // tq
    tk = tq
    nch = tq // CHUNK
    w = 2 * DIFF_D

    vext[:, 0:w] = v_ref[...]
    vext[:, w:2 * w] = jnp.ones((s_len, w), BF16)
    acc[...] = jnp.zeros(acc.shape, F32)

    lane = lax.broadcasted_iota(jnp.int32, (tk, w), 1)
    rchunk = lax.broadcasted_iota(jnp.int32, (tk, w), 0) // CHUNK
    lo = lane < DIFF_D
    hi = jnp.logical_not(lo)
    half = (lo, hi)

    gqk = gqk_ref[...]
    bound = (jnp.max(jnp.abs(gqk[0:1, :])) * jnp.max(jnp.abs(gqk[1:2, :]))
             * (DIFF_D * DIFF_D ** -0.5 * math.log2(math.e) * 1.01))

    lp = lamp_ref[...]
    lam = (jnp.exp(jnp.sum(lp[0:1, :] * lp[1:2, :], axis=-1, keepdims=True))
           - jnp.exp(jnp.sum(lp[2:3, :] * lp[3:4, :], axis=-1, keepdims=True)) + lambda_init)

    def rows_of(blk):
        return pl.ds(pl.multiple_of(blk * tq, tq), tq)

    def finalize(qi):
        o = acc[0, :, 0:w] / acc[0, :, w:2 * w] - lam * (acc[1, :, 0:w] / acc[1, :, w:2 * w])
        on = o * lax.rsqrt(jnp.mean(o * o, axis=-1, keepdims=True) + NORM_EPS) * sg_ref[...]
        on = on * (1.0 - lambda_init)
        o_ref[rows_of(qi), :] = (on * gate_ref[rows_of(qi), :].astype(F32)).astype(BF16)
        acc[...] = jnp.zeros(acc.shape, F32)

    @pl.when(bound <= MAX_STATIC_SHIFT)
    def _():
        q_aug = []
        for sub, base in enumerate((DIFF_D, 0)):
            mlane = lane - (base + 1)
            in_mask = (mlane >= 0) & (mlane < nch)
            q_aug.append(jnp.where((lane == base) | (in_mask & (rchunk == mlane)), 1.0, 0.0).astype(BF16))
            plain = jnp.where(lane == base, -bound, 0.0)
            kaug_s[0, sub] = plain.astype(BF16)
            kaug_s[1, sub] = jnp.where(in_mask & (rchunk > mlane), -MASK_BIG, plain).astype(BF16)

        def prep_q(qi, qslot):
            q = q_ref[rows_of(qi), :]
            for sub in range(2):
                qs_s[qslot, sub] = jnp.where(half[sub], q, q_aug[sub])

        def scores(qslot, j, is_diag, pslot):
            kb = k_ref[rows_of(j), :]
            for sub in range(2):
                ks = jnp.where(half[sub], kb, kaug_s[is_diag, sub])
                pbuf[pslot, sub] = jnp.exp2(_dot_nt(qs_s[qslot, sub], ks)).astype(BF16)

        def accumulate(j, pslot):
            ve = vext[rows_of(j), :]
            for sub in range(2):
                acc[sub] += _dot(pbuf[pslot, sub], ve)

        prep_q(0, 0)
        scores(0, 0, 1, 0)

        def outer(qi, n):
            qslot = qi & 1

            def advance(j, n):
                accumulate(j - 1, n & 1)
                scores(qslot, j, jnp.asarray(j == qi, jnp.int32), (n + 1) & 1)

            def run(j0, count, n):
                for u in range(count):
                    advance(j0 + u, n + u)

            def trip(t, n):
                run(ATT_UNROLL * t + 1, ATT_UNROLL, n)
                return n + ATT_UNROLL

            n = lax.fori_loop(0, qi // ATT_UNROLL, trip, n)
            j = (qi // ATT_UNROLL) * ATT_UNROLL + 1
            part = ATT_UNROLL // 2
            while part:
                @pl.when((qi & part) != 0)
                def _(j=j, n=n, part=part):
                    run(j, part, n)

                j = j + (qi & part)
                n = n + (qi & part)
                part //= 2
            nxt = jnp.minimum(qi + 1, nq - 1)
            prep_q(nxt, 1 - qslot)
            accumulate(qi, n & 1)
            scores(1 - qslot, 0, jnp.asarray(nxt == 0, jnp.int32), (n + 1) & 1)
            finalize(qi)
            return n + 1

        lax.fori_loop(0, nq, outer, 0)

    @pl.when(bound > MAX_STATIC_SHIFT)
    def _():
        zero = jnp.zeros((tk, w), BF16)
        rr = lax.broadcasted_iota(jnp.int32, (tq, tk), 0) // CHUNK
        cc = lax.broadcasted_iota(jnp.int32, (tq, tk), 1) // CHUNK
        causal = cc <= rr

        def outer(qi, carry):
            q = q_ref[rows_of(qi), :]

            def step(j, ms, diag):
                kb = k_ref[rows_of(j), :]
                ve = vext[rows_of(j), :]
                new = []
                for sub in range(2):
                    s = _dot_nt(q, jnp.where(half[sub], kb, zero))
                    if diag:
                        s = jnp.where(causal, s, NEG_BIG)
                    m_new = jnp.maximum(ms[sub], jnp.max(s, axis=-1, keepdims=True))
                    p = jnp.exp2(s - m_new)
                    acc[sub] = jnp.exp2(ms[sub] - m_new) * acc[sub] + _dot(p.astype(BF16), ve)
                    new.append(m_new)
                return tuple(new)

            init = (jnp.full((tq, 1), NEG_BIG, F32), jnp.full((tq, 1), NEG_BIG, F32))
            ms = lax.fori_loop(0, qi, lambda j, c: step(j, c, False), init)
            step(qi, ms, True)
            finalize(qi)
            return carry

        lax.fori_loop(0, nq, outer, 0)


def _attn(bq, bk, bv, bgate, lam_params, sg_row, gqk_rows, lambda_init):
    b, s, _ = bq.shape
    tq = ATT_TQ
    w = 2 * DIFF_D
    hspec = pl.BlockSpec((None, s, w), lambda bi, hi: (bi, 0, hi))
    const = lambda bi, hi: (0, 0)
    return pl.pallas_call(
        functools.partial(_attn_kernel, lambda_init=lambda_init, tq=tq),
        grid=(b, DIFF_HEADS),
        in_specs=[hspec, hspec, hspec, hspec,
                  pl.BlockSpec((8, LANES), const),
                  pl.BlockSpec((1, w), const),
                  pl.BlockSpec((2, LANES), const)],
        out_specs=hspec,
        out_shape=jax.ShapeDtypeStruct((b, s, HEADW), BF16),
        scratch_shapes=[pltpu.VMEM((s, 2 * w), BF16),
                        pltpu.VMEM((2, tq, 2 * w), F32),
                        pltpu.VMEM((2, 2, tq, tq), BF16),
                        pltpu.VMEM((2, 2, tq, w), BF16),
                        pltpu.VMEM((2, 2, tq, w), BF16)],
        compiler_params=pltpu.CompilerParams(
            dimension_semantics=("arbitrary", "arbitrary"), vmem_limit_bytes=VMEM_LIMIT),
        name="attn",
    )(bq, bk, bv, bgate, lam_params, sg_row, gqk_rows)


def _out_kernel(x_ref, ma_ref, mb_ref, wa_ref, wb_ref, gate_ref, o_ref):
    y = _dot(ma_ref[...], wa_ref[...]) + _dot(mb_ref[...], wb_ref[...])
    o_ref[...] = x_ref[...] + gate_ref[...] * y


def _out(x, mixed_a, mixed_b, w_a, w_b, gate):
    b, s, d = x.shape
    tm = OUT_TM
    row = lambda bi, si: (bi, si, 0)
    const2 = lambda bi, si: (0, 0)
    return pl.pallas_call(
        _out_kernel,
        grid=(b, s // tm),
        in_specs=[pl.BlockSpec((None, tm, d), row),
                  pl.BlockSpec((None, tm, HEADW), row),
                  pl.BlockSpec((None, tm, HEADW), row),
                  pl.BlockSpec((HEADW, d), const2),
                  pl.BlockSpec((HEADW, d), const2),
                  pl.BlockSpec((None, 1, d), lambda bi, si: (bi, 0, 0))],
        out_specs=pl.BlockSpec((None, tm, d), row),
        out_shape=jax.ShapeDtypeStruct((b, s, d), F32),
        compiler_params=pltpu.CompilerParams(
            dimension_semantics=("arbitrary", "arbitrary"), vmem_limit_bytes=VMEM_LIMIT),
        name="out",
    )(x, mixed_a, mixed_b, w_a, w_b, gate)


def _constants():
    lane = jnp.arange(LANES)
    sub = lane % DIFF_D
    half = ROT_DIM // 2
    inv_freq = ROPE_THETA ** (-jnp.arange(0, ROT_DIM, 2, dtype=F32) / ROT_DIM)
    freq_row = jnp.where(sub < ROT_DIM, inv_freq[sub % half], 0.0).astype(F32)[None, :]
    sgn_rows = jnp.stack([jnp.where(sub < half, -1.0, 0.0),
                          jnp.where((sub >= half) & (sub < ROT_DIM), 1.0, 0.0)]).astype(F32)
    r = jnp.arange(PROJ_TM)
    lmat = ((r[:, None] // CHUNK == r[None, :] // CHUNK) & (r[None, :] <= r[:, None])).astype(BF16)
    g = jnp.arange(GMAT_W)
    gmat = jnp.where(g[:, None] // DIFF_D == g[None, :] // DIFF_D, 1.0 / DIFF_D, 0.0).astype(BF16)
    return freq_row, sgn_rows, lmat, gmat


def _pad_lanes(v, offset):
    return jnp.zeros((1, LANES), F32).at[0, offset:offset + v.shape[0]].set(v.astype(F32))


def _layer(x, c_pad, pos3, lambda_init, norm_g, w_ada, b_ada, w_in, conv_w, a_log, dt_bias, gdn_norm_g,
           q_norm_g, k_norm_g, lam_q1, lam_k1, lam_q2, lam_k2, subln_g, w_out):
    b, s, d = x.shape
    mod = _ada(c_pad, w_ada, b_ada[None, :])[:b]
    shift = mod[:, None, 0:d]
    scale = mod[:, None, d:2 * d]
    gate = mod[:, None, 2 * d:3 * d]

    nk = GDN_HEADS * GDN_DK
    o_beta = 2 * nk + GDN_HEADS * GDN_DV
    o_agate = o_beta + 2 * GDN_HEADS
    o_b = o_agate + GDN_HEADS * GDN_DV
    bd = jnp.pad(w_in[:, o_beta:o_agate], ((0, 0), (0, LANES - 2 * GDN_HEADS)))
    w_all = jnp.concatenate([
        w_in[:, 0:o_beta], w_in[:, o_agate:o_b],
        w_in[:, o_b:o_b + HEADW], w_in[:, o_b + HEADW:o_b + 2 * HEADW],
        w_in[:, o_b + 2 * HEADW:o_b + 3 * HEADW], w_in[:, o_b + 3 * HEADW:o_b + 4 * HEADW],
        bd], axis=1).astype(BF16)
    freq_row, sgn_rows, lmat, gmat = _constants()
    lam_params = jnp.zeros((8, LANES), F32)
    for r, v in enumerate((lam_q1, lam_k1, lam_q2, lam_k2)):
        lam_params = lam_params.at[r, 0:DIFF_D].set(v.astype(F32))

    gq, gk, gv, agate, bq, bk, bv, bgate, bg, bgt = _proj(
        x, pos3, shift, scale, norm_g[None, :], w_all, conv_w,
        _pad_lanes(a_log, GDN_HEADS), _pad_lanes(dt_bias, GDN_HEADS),
        jnp.tile(q_norm_g, 2 * DIFF_HEADS)[None, :], jnp.tile(k_norm_g, 2 * DIFF_HEADS)[None, :],
        lmat, gmat, freq_row, sgn_rows)
    mixed_a = _gdn(gq, gk, gv, agate, bg, bgt, gdn_norm_g[None, :])
    gqk_rows = jnp.concatenate([_pad_lanes(q_norm_g, 0), _pad_lanes(k_norm_g, 0)], axis=0)
    mixed_b = _attn(bq, bk, bv, bgate, lam_params, subln_g[None, :], gqk_rows, lambda_init)
    w_o = w_out.astype(BF16)
    return _out(x, mixed_a, mixed_b, w_o[:HEADW], w_o[HEADW:], gate)


def kernel(x, c, positions, norm_g, w_ada, b_ada, w_in, conv_w, a_log, dt_bias, gdn_norm_g, q_norm_g,
           k_norm_g, lambda_q1, lambda_k1, lambda_q2, lambda_k2, subln_g, w_out):
    b = x.shape[0]
    c_pad = jnp.pad(c, ((0, 8 - b % 8 if b % 8 else 0), (0, 0)))
    pos3 = positions[:, :, None]
    for l in range(norm_g.shape[0]):
        lambda_init = 0.8 - 0.6 * math.exp(-0.3 * l)
        x = _layer(x, c_pad, pos3, lambda_init, norm_g[l], w_ada[l], b_ada[l], w_in[l], conv_w[l],
                   a_log[l], dt_bias[l], gdn_norm_g[l], q_norm_g[l], k_norm_g[l], lambda_q1[l],
                   lambda_k1[l], lambda_q2[l], lambda_k2[l], subln_g[l], w_out[l])
    return x
```

```python
import functools
import math

import jax
import jax.numpy as jnp
from jax import lax
from jax.experimental import pallas as pl
from jax.experimental.pallas import tpu as pltpu

F32 = jnp.float32
BF16 = jnp.bfloat16
HIGHEST = lax.Precision.HIGHEST

CHUNK = 64
PAIR = 2 * CHUNK
GDN_HEADS = 4
GDN_DK = 128
GDN_DV = 128
CONV_K = 4
DIFF_HEADS = 4
DIFF_D = 64
ROT_DIM = 16
ROPE_THETA = 500000.0
NORM_EPS = 1e-6
LANES = 128
NEG_BIG = -1e30
MASK_BIG = 256.0

GDN_QKV = 3 * GDN_HEADS * GDN_DK
HEADW = 512
OFF_GQKV = 0
OFF_AGATE = GDN_QKV
OFF_BQ = OFF_AGATE + HEADW
OFF_BK = OFF_BQ + HEADW
OFF_BV = OFF_BK + HEADW
OFF_BGATE = OFF_BV + HEADW
OFF_BD = OFF_BGATE + HEADW
W_COLS = OFF_BD + LANES

PROJ_TM = 512
PROJ_RB = 128
GMAT_W = 256
GDN_TC = 128
ATT_TQ = 512
ATT_UNROLL = 8
MAX_STATIC_SHIFT = 60.0
OUT_TM = 1024
VMEM_LIMIT = 48 * 1024 * 1024


def _silu(x):
    h = 0.5 * x
    return h + h * jnp.tanh(h)


def _dot(a, b, **kw):
    return jnp.dot(a, b, preferred_element_type=F32, **kw)


def _dot_nt(a, b, **kw):
    return lax.dot_general(a, b, (((1,), (1,)), ((), ())), preferred_element_type=F32, **kw)


def _dot_tn(a, b, **kw):
    return lax.dot_general(a, b, (((0,), (0,)), ((), ())), preferred_element_type=F32, **kw)


def _ada_kernel(c_ref, w_ref, b_ref, o_ref):
    o_ref[...] = _dot(_silu(c_ref[...]), w_ref[...], precision=HIGHEST) + b_ref[...]


def _ada(c_pad, w_ada, b_ada):
    rows, d = c_pad.shape
    n = w_ada.shape[1]
    tn = 512
    return pl.pallas_call(
        _ada_kernel,
        grid=(n // tn,),
        in_specs=[pl.BlockSpec((rows, d), lambda j: (0, 0)),
                  pl.BlockSpec((d, tn), lambda j: (0, j)),
                  pl.BlockSpec((1, tn), lambda j: (0, j))],
        out_specs=pl.BlockSpec((rows, tn), lambda j: (0, j)),
        out_shape=jax.ShapeDtypeStruct((rows, n), F32),
        name="ada",
    )(c_pad, w_ada, b_ada)


def _proj_kernel(x_ref, pos_ref, shift_ref, scale_ref, ng_ref, w_ref, convw_ref, alog_ref, dtb_ref,
                 qg_ref, kg_ref, lmat_ref, gmat_ref, freq_ref, sgn_ref,
                 gq_ref, gk_ref, gv_ref, ag_ref, bq_ref, bk_ref, bv_ref, bgate_ref, bg_ref, bgt_ref,
                 cq, ck, cv, z0, z1, h_s, gm_s, tab):
    tm = x_ref.shape[0]
    x = x_ref[...]
    ms = jnp.mean(x * x, axis=-1, keepdims=True)
    gain = ng_ref[...] * (1.0 + scale_ref[...])
    h_s[...] = (x * lax.rsqrt(ms + NORM_EPS) * gain + shift_ref[...]).astype(BF16)

    cbufs = (cq, ck, cv)
    zbuf = (z0, z1)

    @pl.when(pl.program_id(1) == 0)
    def _():
        for cb in cbufs:
            cb[0:8, :] = jnp.zeros((8, HEADW), F32)

    def mm_pieces(dst, row0, off, width):
        def piece(c0):
            c1 = min(c0 + GMAT_W, width)
            dst[row0:row0 + tm, c0:c1] = _dot(h_s[...], w_ref[:, off + c0:off + c1])
        return [functools.partial(piece, c0) for c0 in range(0, width, GMAT_W)]

    def conv_pieces(g, out_ref):
        cb = cbufs[g]

        def piece(hh):
            hs = slice(hh * LANES, (hh + 1) * LANES)
            cs = slice(g * HEADW + hh * LANES, g * HEADW + (hh + 1) * LANES)
            for r0 in range(0, tm, PROJ_RB):
                acc = convw_ref[3:4, cs] * cb[8 + r0:8 + r0 + PROJ_RB, hs]
                for j in range(CONV_K - 1):
                    acc = acc + convw_ref[j:j + 1, cs] * cb[5 + j + r0:5 + j + r0 + PROJ_RB, hs]
                a = _silu(acc)
                if g < 2:
                    a = a * lax.rsqrt(jnp.sum(a * a, axis=-1, keepdims=True) + NORM_EPS)
                    if g == 0:
                        a = a * (GDN_DK ** -0.5)
                out_ref[r0:r0 + PROJ_RB, hs] = a.astype(BF16)
            cb[0:8, hs] = cb[tm:tm + 8, hs]
        return [functools.partial(piece, hh) for hh in range(GDN_HEADS)]

    def gate_pieces(slot, out_ref):
        def piece(r0):
            rs = slice(r0, r0 + PROJ_RB)
            out_ref[rs, :] = _silu(zbuf[slot][rs, :]).astype(BF16)
        return [functools.partial(piece, r0) for r0 in range(0, tm, PROJ_RB)]

    def qk_pieces(slot, g_ref, out_ref, post):
        def piece(c0):
            z = zbuf[slot][:, c0:c0 + GMAT_W]
            gm_s[:, c0:c0 + GMAT_W] = _dot((z * z).astype(BF16), gmat_ref[...])
            for r0 in range(0, tm, PROJ_RB):
                rs = slice(r0, r0 + PROJ_RB)
                for h0 in range(c0, c0 + GMAT_W, LANES):
                    hs = slice(h0, h0 + LANES)
                    ys = zbuf[slot][rs, hs] * lax.rsqrt(gm_s[rs, hs] + NORM_EPS) * g_ref[:, hs]
                    r = (ys * tab[0, rs, :] + pltpu.roll(ys, LANES - ROT_DIM // 2, axis=1) * tab[1, rs, :]
                         + pltpu.roll(ys, ROT_DIM // 2, axis=1) * tab[2, rs, :])
                    if post is not None:
                        r = r * post
                    out_ref[rs, hs] = r.astype(BF16)
        return [functools.partial(piece, c0) for c0 in range(0, HEADW, GMAT_W)]

    def table_pieces():
        def piece(r0):
            rs = slice(r0, r0 + PROJ_RB)
            ang = pos_ref[rs, :].astype(F32) * freq_ref[...]
            sin_t = jnp.sin(ang)
            tab[0, rs, :] = jnp.cos(ang)
            tab[1, rs, :] = sin_t * sgn_ref[0:1, :]
            tab[2, rs, :] = sin_t * sgn_ref[1:2, :]
        return [functools.partial(piece, r0) for r0 in range(0, tm, PROJ_RB)]

    def ep_bd(slot):
        z = zbuf[slot][:, 0:LANES]
        lane = lax.broadcasted_iota(jnp.int32, (1, LANES), 1)
        beta = 1.0 / (1.0 + jnp.exp(-z))
        zz = z + dtb_ref[...]
        softplus = jnp.maximum(zz, 0.0) + jnp.log(1.0 + jnp.exp(-jnp.abs(zz)))
        is_g = (lane >= GDN_HEADS) & (lane < 2 * GDN_HEADS)
        g = jnp.where(is_g, -jnp.exp(alog_ref[...]) * softplus, 0.0)
        g_hi = g.astype(BF16)
        g_lo = (g - g_hi.astype(F32)).astype(BF16)
        gc = _dot(lmat_ref[...], g_hi) + _dot(lmat_ref[...], g_lo)
        bg = jnp.where(lane < GDN_HEADS, beta, gc)
        bg_ref[...] = bg
        bgt_ref[...] = bg.T[0:8, :]

    def cast_bv():
        bv_ref[...] = zbuf[0][...].astype(BF16)

    def emit(*lists):
        for k in range(max(len(pl_) for pl_ in lists)):
            for pl_ in lists:
                if k < len(pl_):
                    pl_[k]()

    q_scale = DIFF_D ** -0.5 * math.log2(math.e)
    stages = [
        (mm_pieces(cq, 8, OFF_GQKV, HEADW), conv_pieces(0, gq_ref)),
        (mm_pieces(z1, 0, OFF_BD, LANES) + mm_pieces(ck, 8, OFF_GQKV + HEADW, HEADW),
         conv_pieces(1, gk_ref) + [functools.partial(ep_bd, 1)]),
        (mm_pieces(cv, 8, OFF_GQKV + 2 * HEADW, HEADW), conv_pieces(2, gv_ref)),
        (mm_pieces(z0, 0, OFF_BQ, HEADW), qk_pieces(0, qg_ref, bq_ref, q_scale)),
        (mm_pieces(z1, 0, OFF_BK, HEADW), qk_pieces(1, kg_ref, bk_ref, None)),
        (mm_pieces(z0, 0, OFF_AGATE, HEADW), gate_pieces(0, ag_ref)),
        (mm_pieces(z1, 0, OFF_BGATE, HEADW), gate_pieces(1, bgate_ref)),
        (mm_pieces(z0, 0, OFF_BV, HEADW), [cast_bv]),
    ]
    emit(stages[0][0], table_pieces())
    for i, (_, epilogue) in enumerate(stages):
        emit(stages[i + 1][0] if i + 1 < len(stages) else [], epilogue)


def _proj(x, pos3, shift, scale, norm_g, w_all, conv_w, alog_row, dtb_row, qg_row, kg_row,
          lmat, gmat, freq_row, sgn_rows):
    b, s, d = x.shape
    tm = PROJ_TM
    row = lambda bi, si: (bi, si, 0)
    const2 = lambda bi, si: (0, 0)
    per_b = lambda bi, si: (bi, 0, 0)
    hw = jax.ShapeDtypeStruct((b, s, HEADW), BF16)
    out_shape = [hw] * 8 + [jax.ShapeDtypeStruct((b, s, LANES), F32),
                            jax.ShapeDtypeStruct((b, 8, s), F32)]
    hw_spec = pl.BlockSpec((None, tm, HEADW), row)
    out_specs = [hw_spec] * 8 + [pl.BlockSpec((None, tm, LANES), row),
                                 pl.BlockSpec((None, 8, tm), lambda bi, si: (bi, 0, si))]
    in_specs = [
        pl.BlockSpec((None, tm, d), row),
        pl.BlockSpec((None, tm, 1), row),
        pl.BlockSpec((None, 1, d), per_b),
        pl.BlockSpec((None, 1, d), per_b),
        pl.BlockSpec((1, d), const2),
        pl.BlockSpec((d, W_COLS), const2),
        pl.BlockSpec((CONV_K, GDN_QKV), const2),
        pl.BlockSpec((1, LANES), const2),
        pl.BlockSpec((1, LANES), const2),
        pl.BlockSpec((1, HEADW), const2),
        pl.BlockSpec((1, HEADW), const2),
        pl.BlockSpec((tm, tm), const2),
        pl.BlockSpec((GMAT_W, GMAT_W), const2),
        pl.BlockSpec((1, LANES), const2),
        pl.BlockSpec((2, LANES), const2),
    ]
    return pl.pallas_call(
        _proj_kernel,
        grid=(b, s // tm),
        in_specs=in_specs,
        out_specs=out_specs,
        out_shape=out_shape,
        scratch_shapes=[pltpu.VMEM((tm + 8, HEADW), F32)] * 3 + [pltpu.VMEM((tm, HEADW), F32)] * 2
        + [pltpu.VMEM((tm, d), BF16), pltpu.VMEM((tm, HEADW), F32), pltpu.VMEM((3, tm, LANES), F32)],
        compiler_params=pltpu.CompilerParams(
            dimension_semantics=("arbitrary", "arbitrary"), vmem_limit_bytes=VMEM_LIMIT),
        name="proj",
    )(x, pos3, shift, scale, norm_g, w_all, conv_w, alog_row, dtb_row, qg_row, kg_row,
      lmat, gmat, freq_row, sgn_rows)


def _split(a):
    hi = a.astype(BF16)
    return hi, (a - hi.astype(F32)).astype(BF16)


def _gdn_kernel(q_ref, k_ref, v_ref, gate_ref, bg_ref, bgt_ref, ng_ref, o_ref,
                state, wq_s, u_s, kva_s, vn_s):
    nb, tc = q_ref.shape[0], q_ref.shape[1]
    npair = tc // PAIR
    nc = tc // CHUNK

    @pl.when(pl.program_id(0) == 0)
    def _():
        state[...] = jnp.zeros(state.shape, F32)
        vn_s[...] = jnp.zeros(vn_s.shape, BF16)

    ii = lax.broadcasted_iota(jnp.int32, (CHUNK, PAIR), 0)
    lane_w = lax.broadcasted_iota(jnp.int32, (CHUNK, PAIR), 1)
    jj = lane_w % CHUNK
    left_w = lane_w < CHUNK
    left_p = lax.broadcasted_iota(jnp.int32, (PAIR, PAIR), 1) < CHUNK
    top_col = lax.broadcasted_iota(jnp.int32, (PAIR, 1), 0) < CHUNK
    eye = (ii == jj).astype(F32)
    lower = ii >= jj
    strict = ii > jj

    def widen(m):
        return jnp.where(left_w, m[0:CHUNK], m[CHUNK:PAIR])

    def bdiag(wd):
        z = jnp.zeros_like(wd)
        return jnp.concatenate([jnp.where(left_w, wd, z), jnp.where(left_w, z, wd)], axis=0)

    def mm(lhs, rhs):
        n = len(lhs)
        b_hi, b_lo = _split(bdiag(rhs))
        parts = [_split(a) for a in lhs]
        his = [hi_ for hi_, _ in parts]
        r_hi = _dot(jnp.concatenate(his + [lo_ for _, lo_ in parts], axis=0), b_hi)
        r_lo = _dot(jnp.concatenate(his, axis=0) if n > 1 else his[0], b_lo)
        blk = lambda r, i: r[i * CHUNK:(i + 1) * CHUNK]
        return [blk(r_hi, i) + blk(r_hi, n + i) + blk(r_lo, i) for i in range(n)]

    items = [(bb, hh, p) for bb in range(nb) for hh in range(GDN_HEADS) for p in range(npair)]
    pre = []
    for bb, hh, p in items:
        cs = slice(hh * LANES, (hh + 1) * LANES)
        rs = slice(p * PAIR, (p + 1) * PAIR)
        q2, k2 = q_ref[bb, rs, cs], k_ref[bb, rs, cs]
        gcol2 = bg_ref[bb, rs, GDN_HEADS + hh:GDN_HEADS + hh + 1]
        bcol_w = widen(jnp.broadcast_to(bg_ref[bb, rs, hh:hh + 1], (PAIR, PAIR)))
        gcol_w = widen(jnp.broadcast_to(gcol2, (PAIR, PAIR)))
        grow = bgt_ref[bb, GDN_HEADS + hh:GDN_HEADS + hh + 1, rs]
        dec = jnp.where(lower, jnp.exp(jnp.minimum(gcol_w - grow, 0.0)), 0.0)
        gram = _dot_nt(jnp.concatenate([k2, q2], axis=0), k2)
        xm = jnp.where(strict, -(bcol_w * widen(gram[0:PAIR]) * dec), 0.0)
        a_w = jnp.where(lower, widen(gram[PAIR:2 * PAIR]) * dec, 0.0)
        pre.append((xm, a_w))

    def decayed_operands(item, a_w):
        bb, hh, p = item
        cs = slice(hh * LANES, (hh + 1) * LANES)
        rs = slice(p * PAIR, (p + 1) * PAIR)
        q2, k2 = q_ref[bb, rs, cs], k_ref[bb, rs, cs]
        gcol2 = bg_ref[bb, rs, GDN_HEADS + hh:GDN_HEADS + hh + 1]
        grow = bgt_ref[bb, GDN_HEADS + hh:GDN_HEADS + hh + 1, rs]
        g_end = jnp.where(top_col, grow[:, CHUNK - 1:CHUNK], grow[:, PAIR - 1:PAIR])
        e_g = jnp.exp(gcol2)
        k2f = k2.astype(F32)
        qd = (q2.astype(F32) * e_g).astype(BF16)
        kdt = (k2f * jnp.exp(g_end - gcol2)).T
        a_bd = bdiag(a_w)
        kva_s[bb, hh, 2 * p] = jnp.concatenate(
            [jnp.where(left_p, kdt, 0.0), a_bd[0:CHUNK]], axis=0).astype(BF16)
        kva_s[bb, hh, 2 * p + 1] = jnp.concatenate(
            [jnp.where(left_p, 0.0, kdt), a_bd[CHUNK:PAIR]], axis=0).astype(BF16)
        return qd, (k2f * e_g).astype(BF16)

    levels = 4
    per_level = -(-len(items) // levels)
    side = []
    ts = [eye + xm for xm, _ in pre]
    ps = [mm([xm], xm)[0] for xm, _ in pre]
    for lvl in range(levels):
        both = [mm([t, pw], pw) for t, pw in zip(ts, ps)]
        for idx in range(lvl * per_level, min((lvl + 1) * per_level, len(items))):
            side.append(decayed_operands(items[idx], pre[idx][1]))
        ts = [t + tp for t, (tp, _) in zip(ts, both)]
        ps = [pp for _, pp in both]
    ts = [t + mm([t], pw)[0] for t, pw in zip(ts, ps)]
    for (bb, hh, p), t, (qd, kg) in zip(items, ts, side):
        cs = slice(hh * LANES, (hh + 1) * LANES)
        rs = slice(p * PAIR, (p + 1) * PAIR)
        tb = bdiag(t * bgt_ref[bb, hh:hh + 1, rs]).astype(BF16)
        uw = _dot(tb, jnp.concatenate([v_ref[bb, rs, cs], kg], axis=1))
        u_s[bb, hh, rs, :] = uw[:, 0:GDN_DV]
        w2 = uw[:, GDN_DV:GDN_DV + GDN_DK].astype(BF16)
        for e in range(2):
            es = slice(e * CHUNK, (e + 1) * CHUNK)
            wq_s[bb, hh, 2 * p + e] = jnp.concatenate([w2[es], qd[es]], axis=0)

    chains = [(bb, hh) for bb in range(nb) for hh in range(GDN_HEADS)]
    for c in range(nc):
        e = c % 2
        rs = slice(c * CHUNK, (c + 1) * CHUNK)
        es = slice(e * CHUNK, (e + 1) * CHUNK)
        sts = [state[ch] for ch in chains]
        rr = [_dot(wq_s[bb, hh, c], st.astype(BF16)) for (bb, hh), st in zip(chains, sts)]
        for (bb, hh), r in zip(chains, rr):
            vn_s[bb, hh, es, :] = (u_s[bb, hh, rs, :] - r[0:CHUNK]).astype(BF16)
        for (bb, hh), st, r in zip(chains, sts, rr):
            cs = slice(hh * LANES, (hh + 1) * LANES)
            g_end = bgt_ref[bb, GDN_HEADS + hh:GDN_HEADS + hh + 1, (c + 1) * CHUNK - 1:(c + 1) * CHUNK]
            kva = _dot(kva_s[bb, hh, c], vn_s[bb, hh])
            state[bb, hh] = st * jnp.exp(g_end) + kva[0:GDN_DK]
            o = r[CHUNK:PAIR] + kva[GDN_DK:GDN_DK + CHUNK]
            on = o * lax.rsqrt(jnp.mean(o * o, axis=-1, keepdims=True) + NORM_EPS) * ng_ref[...]
            o_ref[bb, rs, cs] = (on * gate_ref[bb, rs, cs].astype(F32)).astype(BF16)


def _gdn(gq, gk, gv, agate, bg, bgt, ng_row):
    b, s, _ = gq.shape
    tc = GDN_TC
    row = lambda si: (0, si, 0)
    hw_spec = pl.BlockSpec((b, tc, HEADW), row)
    per_chain = (b, GDN_HEADS)
    return pl.pallas_call(
        _gdn_kernel,
        grid=(s // tc,),
        in_specs=[hw_spec, hw_spec, hw_spec, hw_spec,
                  pl.BlockSpec((b, tc, LANES), row),
                  pl.BlockSpec((b, 8, tc), lambda si: (0, 0, si)),
                  pl.BlockSpec((1, GDN_DV), lambda si: (0, 0))],
        out_specs=hw_spec,
        out_shape=jax.ShapeDtypeStruct((b, s, HEADW), BF16),
        scratch_shapes=[pltpu.VMEM(per_chain + (GDN_DK, GDN_DV), F32),
                        pltpu.VMEM(per_chain + (tc // CHUNK, PAIR, GDN_DK), BF16),
                        pltpu.VMEM(per_chain + (tc, GDN_DV), F32),
                        pltpu.VMEM(per_chain + (tc // CHUNK, GDN_DK + CHUNK, PAIR), BF16),
                        pltpu.VMEM(per_chain + (PAIR, GDN_DV), BF16)],
        compiler_params=pltpu.CompilerParams(
            dimension_semantics=("arbitrary",), vmem_limit_bytes=VMEM_LIMIT),
        name="gdn",
    )(gq, gk, gv, agate, bg, bgt, ng_row)


def _attn_kernel(q_ref, k_ref, v_ref, gate_ref, lamp_ref, sg_ref, gqk_ref, o_ref,
                 vext, acc, pbuf, kaug_s, *, lambda_init, tq):
    s_len = q_ref.shape[0]
    nq = s_len // tq
    tk = tq
    nch = tq // CHUNK
    w = 2 * DIFF_D

    vext[:, 0:w] = v_ref[...]
    vext[:, w:2 * w] = jnp.ones((s_len, w), BF16)

    @pl.when(jnp.logical_and(pl.program_id(0) == 0, pl.program_id(1) == 0))
    def _():
        acc[...] = jnp.zeros(acc.shape, F32)

    lane = lax.broadcasted_iota(jnp.int32, (tk, w), 1)
    rchunk = lax.broadcasted_iota(jnp.int32, (tk, w), 0) // CHUNK
    lo = lane < DIFF_D
    hi = jnp.logical_not(lo)
    half = (lo, hi)

    gqk = gqk_ref[...]
    bound = (jnp.max(jnp.abs(gqk[0:1, :])) * jnp.max(jnp.abs(gqk[1:2, :]))
             * (DIFF_D * DIFF_D ** -0.5 * math.log2(math.e) * 1.01))

    lp = lamp_ref[...]
    lam = (jnp.exp(jnp.sum(lp[0:1, :] * lp[1:2, :], axis=-1, keepdims=True))
           - jnp.exp(jnp.sum(lp[2:3, :] * lp[3:4, :], axis=-1, keepdims=True)) + lambda_init)

    def rows_of(blk):
        return pl.ds(pl.multiple_of(blk * tq, tq), tq)

    def write_rows(qi, a0, a1):
        o = a0[:, 0:w] / a0[:, w:2 * w] - lam * (a1[:, 0:w] / a1[:, w:2 * w])
        on = o * lax.rsqrt(jnp.mean(o * o, axis=-1, keepdims=True) + NORM_EPS) * sg_ref[...]
        on = on * (1.0 - lambda_init)
        o_ref[rows_of(qi), :] = (on * gate_ref[rows_of(qi), :].astype(F32)).astype(BF16)
        acc[qi] = jnp.zeros(acc.shape[1:], F32)


    @pl.when(bound <= MAX_STATIC_SHIFT)
    def _():
        q_aug = []
        for sub, base in enumerate((DIFF_D, 0)):
            mlane = lane - (base + 1)
            in_mask = (mlane >= 0) & (mlane < nch)
            q_aug.append(jnp.where((lane == base) | (in_mask & (rchunk == mlane)), 1.0, 0.0).astype(BF16))
            plain = jnp.where(lane == base, -bound, 0.0)
            kaug_s[0, sub] = plain.astype(BF16)
            kaug_s[1, sub] = jnp.where(in_mask & (rchunk > mlane), -MASK_BIG, plain).astype(BF16)

        def scores(qi, j, pslot):
            q = q_ref[rows_of(qi), :]
            kb = k_ref[rows_of(j), :]
            is_diag = jnp.asarray(j == qi, jnp.int32)
            for sub in range(2):
                qs = jnp.where(half[sub], q, q_aug[sub])
                ks = jnp.where(half[sub], kb, kaug_s[is_diag, sub])
                pbuf[pslot, sub] = jnp.exp2(_dot_nt(qs, ks)).astype(BF16)

        def accumulate(qi, j, pslot):
            ve = vext[rows_of(j), :]
            for sub in range(2):
                acc[qi, sub] += _dot(pbuf[pslot, sub], ve)

        def advance(st):
            qi, j, pqi, pj, n = st
            accumulate(pqi, pj, n & 1)
            scores(qi, j, (n + 1) & 1)
            wrap = j >= qi
            return (jnp.where(wrap, qi + 1, qi), jnp.where(wrap, 0, j + 1), qi, j, n + 1)

        todo = nq * (nq + 1) // 2 - 1
        unroll = max([u for u in range(ATT_UNROLL // 2, 2 * ATT_UNROLL) if todo % u == 0] or [ATT_UNROLL],
                     key=lambda u: -abs(u - ATT_UNROLL))

        def trip(t, st):
            for _ in range(unroll):
                st = advance(st)
            return st

        scores(0, 0, 0)
        one, zero = jnp.int32(1), jnp.int32(0)
        trips, rest = divmod(todo, unroll)
        st = lax.fori_loop(0, trips, trip, (one, zero, zero, zero, zero))
        for _ in range(rest):
            st = advance(st)
        accumulate(st[2], st[3], st[4] & 1)

        def write_block(qi, carry):
            write_rows(qi, acc[qi, 0], acc[qi, 1])
            return carry

        lax.fori_loop(0, nq, write_block, 0)

    @pl.when(bound > MAX_STATIC_SHIFT)
    def _():
        zero = jnp.zeros((tk, w), BF16)
        rr = lax.broadcasted_iota(jnp.int32, (tq, tk), 0) // CHUNK
        cc = lax.broadcasted_iota(jnp.int32, (tq, tk), 1) // CHUNK
        causal = cc <= rr

        def outer(qi, carry):
            q = q_ref[rows_of(qi), :]

            def step(j, ms, diag):
                kb = k_ref[rows_of(j), :]
                ve = vext[rows_of(j), :]
                new = []
                for sub in range(2):
                    s = _dot_nt(q, jnp.where(half[sub], kb, zero))
                    if diag:
                        s = jnp.where(causal, s, NEG_BIG)
                    m_new = jnp.maximum(ms[sub], jnp.max(s, axis=-1, keepdims=True))
                    p = jnp.exp2(s - m_new)
                    acc[qi, sub] = jnp.exp2(ms[sub] - m_new) * acc[qi, sub] + _dot(p.astype(BF16), ve)
                    new.append(m_new)
                return tuple(new)

            init = (jnp.full((tq, 1), NEG_BIG, F32), jnp.full((tq, 1), NEG_BIG, F32))
            ms = lax.fori_loop(0, qi, lambda j, c: step(j, c, False), init)
            step(qi, ms, True)
            write_rows(qi, acc[qi, 0], acc[qi, 1])
            return carry

        lax.fori_loop(0, nq, outer, 0)


def _attn(bq, bk, bv, bgate, lam_params, sg_row, gqk_rows, lambda_init):
    b, s, _ = bq.shape
    tq = ATT_TQ
    w = 2 * DIFF_D
    hspec = pl.BlockSpec((None, s, w), lambda bi, hi: (bi, 0, hi))
    const = lambda bi, hi: (0, 0)
    return pl.pallas_call(
        functools.partial(_attn_kernel, lambda_init=lambda_init, tq=tq),
        grid=(b, DIFF_HEADS),
        in_specs=[hspec, hspec, hspec, hspec,
                  pl.BlockSpec((8, LANES), const),
                  pl.BlockSpec((1, w), const),
                  pl.BlockSpec((2, LANES), const)],
        out_specs=hspec,
        out_shape=jax.ShapeDtypeStruct((b, s, HEADW), BF16),
        scratch_shapes=[pltpu.VMEM((s, 2 * w), BF16),
                        pltpu.VMEM((s // tq, 2, tq, 2 * w), F32),
                        pltpu.VMEM((2, 2, tq, tq), BF16),
                        pltpu.VMEM((2, 2, tq, w), BF16)],
        compiler_params=pltpu.CompilerParams(
            dimension_semantics=("arbitrary", "arbitrary"), vmem_limit_bytes=VMEM_LIMIT),
        name="attn",
    )(bq, bk, bv, bgate, lam_params, sg_row, gqk_rows)


def _out_kernel(x_ref, ma_ref, mb_ref, wa_ref, wb_ref, gate_ref, o_ref):
    y = _dot(ma_ref[...], wa_ref[...]) + _dot(mb_ref[...], wb_ref[...])
    o_ref[...] = x_ref[...] + gate_ref[...] * y


def _out(x, mixed_a, mixed_b, w_a, w_b, gate):
    b, s, d = x.shape
    tm = OUT_TM
    row = lambda bi, si: (bi, si, 0)
    const2 = lambda bi, si: (0, 0)
    return pl.pallas_call(
        _out_kernel,
        grid=(b, s // tm),
        in_specs=[pl.BlockSpec((None, tm, d), row),
                  pl.BlockSpec((None, tm, HEADW), row),
                  pl.BlockSpec((None, tm, HEADW), row),
                  pl.BlockSpec((HEADW, d), const2),
                  pl.BlockSpec((HEADW, d), const2),
                  pl.BlockSpec((None, 1, d), lambda bi, si: (bi, 0, 0))],
        out_specs=pl.BlockSpec((None, tm, d), row),
        out_shape=jax.ShapeDtypeStruct((b, s, d), F32),
        compiler_params=pltpu.CompilerParams(
            dimension_semantics=("arbitrary", "arbitrary"), vmem_limit_bytes=VMEM_LIMIT),
        name="out",
    )(x, mixed_a, mixed_b, w_a, w_b, gate)


def _constants():
    lane = jnp.arange(LANES)
    sub = lane % DIFF_D
    half = ROT_DIM // 2
    inv_freq = ROPE_THETA ** (-jnp.arange(0, ROT_DIM, 2, dtype=F32) / ROT_DIM)
    freq_row = jnp.where(sub < ROT_DIM, inv_freq[sub % half], 0.0).astype(F32)[None, :]
    sgn_rows = jnp.stack([jnp.where(sub < half, -1.0, 0.0),
                          jnp.where((sub >= half) & (sub < ROT_DIM), 1.0, 0.0)]).astype(F32)
    r = jnp.arange(PROJ_TM)
    lmat = ((r[:, None] // CHUNK == r[None, :] // CHUNK) & (r[None, :] <= r[:, None])).astype(BF16)
    g = jnp.arange(GMAT_W)
    gmat = jnp.where(g[:, None] // DIFF_D == g[None, :] // DIFF_D, 1.0 / DIFF_D, 0.0).astype(BF16)
    return freq_row, sgn_rows, lmat, gmat


def _pad_lanes(v, offset):
    return jnp.zeros((1, LANES), F32).at[0, offset:offset + v.shape[0]].set(v.astype(F32))


def _layer(x, c_pad, pos3, lambda_init, norm_g, w_ada, b_ada, w_in, conv_w, a_log, dt_bias, gdn_norm_g,
           q_norm_g, k_norm_g, lam_q1, lam_k1, lam_q2, lam_k2, subln_g, w_out):
    b, s, d = x.shape
    mod = _ada(c_pad, w_ada, b_ada[None, :])[:b]
    shift = mod[:, None, 0:d]
    scale = mod[:, None, d:2 * d]
    gate = mod[:, None, 2 * d:3 * d]

    nk = GDN_HEADS * GDN_DK
    o_beta = 2 * nk + GDN_HEADS * GDN_DV
    o_agate = o_beta + 2 * GDN_HEADS
    o_b = o_agate + GDN_HEADS * GDN_DV
    bd = jnp.pad(w_in[:, o_beta:o_agate], ((0, 0), (0, LANES - 2 * GDN_HEADS)))
    w_all = jnp.concatenate([
        w_in[:, 0:o_beta], w_in[:, o_agate:o_b],
        w_in[:, o_b:o_b + HEADW], w_in[:, o_b + HEADW:o_b + 2 * HEADW],
        w_in[:, o_b + 2 * HEADW:o_b + 3 * HEADW], w_in[:, o_b + 3 * HEADW:o_b + 4 * HEADW],
        bd], axis=1).astype(BF16)
    freq_row, sgn_rows, lmat, gmat = _constants()
    lam_params = jnp.zeros((8, LANES), F32)
    for r, v in enumerate((lam_q1, lam_k1, lam_q2, lam_k2)):
        lam_params = lam_params.at[r, 0:DIFF_D].set(v.astype(F32))

    gq, gk, gv, agate, bq, bk, bv, bgate, bg, bgt = _proj(
        x, pos3, shift, scale, norm_g[None, :], w_all, conv_w,
        _pad_lanes(a_log, GDN_HEADS), _pad_lanes(dt_bias, GDN_HEADS),
        jnp.tile(q_norm_g, 2 * DIFF_HEADS)[None, :], jnp.tile(k_norm_g, 2 * DIFF_HEADS)[None, :],
        lmat, gmat, freq_row, sgn_rows)
    mixed_a = _gdn(gq, gk, gv, agate, bg, bgt, gdn_norm_g[None, :])
    gqk_rows = jnp.concatenate([_pad_lanes(q_norm_g, 0), _pad_lanes(k_norm_g, 0)], axis=0)
    mixed_b = _attn(bq, bk, bv, bgate, lam_params, subln_g[None, :], gqk_rows, lambda_init)
    w_o = w_out.astype(BF16)
    return _out(x, mixed_a, mixed_b, w_o[:HEADW], w_o[HEADW:], gate)


def kernel(x, c, positions, norm_g, w_ada, b_ada, w_in, conv_w, a_log, dt_bias, gdn_norm_g, q_norm_g,
           k_norm_g, lambda_q1, lambda_k1, lambda_q2, lambda_k2, subln_g, w_out):
    b, s, d = x.shape
    for tile in (PROJ_TM, GDN_TC, ATT_TQ, OUT_TM):
        assert s % tile == 0, (s, tile)
    assert d == 2 * HEADW and w_in.shape[-1] == W_COLS - LANES + 2 * GDN_HEADS
    c_pad = jnp.pad(c, ((0, 8 - b % 8 if b % 8 else 0), (0, 0)))
    pos3 = positions[:, :, None]
    for l in range(norm_g.shape[0]):
        lambda_init = 0.8 - 0.6 * math.exp(-0.3 * l)
        x = _layer(x, c_pad, pos3, lambda_init, norm_g[l], w_ada[l], b_ada[l], w_in[l], conv_w[l],
                   a_log[l], dt_bias[l], gdn_norm_g[l], q_norm_g[l], k_norm_g[l], lambda_q1[l],
                   lambda_k1[l], lambda_q2[l], lambda_k2[l], subln_g[l], w_out[l])
    return x
```

```python
import functools
import math

import jax
import jax.numpy as jnp
from jax import lax
from jax.experimental import pallas as pl
from jax.experimental.pallas import tpu as pltpu

F32 = jnp.float32
BF16 = jnp.bfloat16
HIGHEST = lax.Precision.HIGHEST

CHUNK = 64
PAIR = 2 * CHUNK
GDN_HEADS = 4
GDN_DK = 128
GDN_DV = 128
CONV_K = 4
DIFF_HEADS = 4
DIFF_D = 64
ROT_DIM = 16
ROPE_THETA = 500000.0
NORM_EPS = 1e-6
LANES = 128
NEG_BIG = -1e30
MASK_BIG = 256.0

GDN_QKV = 3 * GDN_HEADS * GDN_DK
HEADW = 512
OFF_GQKV = 0
OFF_AGATE = GDN_QKV
OFF_BQ = OFF_AGATE + HEADW
OFF_BK = OFF_BQ + HEADW
OFF_BV = OFF_BK + HEADW
OFF_BGATE = OFF_BV + HEADW
OFF_BD = OFF_BGATE + HEADW
W_COLS = OFF_BD + LANES

PROJ_TM = 512
PROJ_RB = 128
GMAT_W = 256
GDN_TC = 128
ATT_TQ = 512
ATT_UNROLL = 8
MAX_STATIC_SHIFT = 60.0
OUT_TM = 1024
VMEM_LIMIT = 48 * 1024 * 1024


def _silu(x):
    h = 0.5 * x
    return h + h * jnp.tanh(h)


def _dot(a, b, **kw):
    return jnp.dot(a, b, preferred_element_type=F32, **kw)


def _dot_nt(a, b, **kw):
    return lax.dot_general(a, b, (((1,), (1,)), ((), ())), preferred_element_type=F32, **kw)


def _dot_tn(a, b, **kw):
    return lax.dot_general(a, b, (((0,), (0,)), ((), ())), preferred_element_type=F32, **kw)


def _ada_kernel(c_ref, w_ref, b_ref, o_ref):
    o_ref[...] = _dot(_silu(c_ref[...]), w_ref[...], precision=HIGHEST) + b_ref[...]


def _ada(c_pad, w_ada, b_ada):
    rows, d = c_pad.shape
    n = w_ada.shape[1]
    tn = 512
    return pl.pallas_call(
        _ada_kernel,
        grid=(n // tn,),
        in_specs=[pl.BlockSpec((rows, d), lambda j: (0, 0)),
                  pl.BlockSpec((d, tn), lambda j: (0, j)),
                  pl.BlockSpec((1, tn), lambda j: (0, j))],
        out_specs=pl.BlockSpec((rows, tn), lambda j: (0, j)),
        out_shape=jax.ShapeDtypeStruct((rows, n), F32),
        name="ada",
    )(c_pad, w_ada, b_ada)


def _proj_kernel(x_ref, pos_ref, shift_ref, scale_ref, ng_ref, w_ref, convw_ref, alog_ref, dtb_ref,
                 qg_ref, kg_ref, lmat_ref, gmat_ref, freq_ref, sgn_ref,
                 gq_ref, gk_ref, gv_ref, ag_ref, bq_ref, bk_ref, bv_ref, bgate_ref, bg_ref, bgt_ref,
                 cq, ck, cv, z0, z1, h_s, gm_s, tab):
    tm = x_ref.shape[0]
    x = x_ref[...]
    ms = jnp.mean(x * x, axis=-1, keepdims=True)
    gain = ng_ref[...] * (1.0 + scale_ref[...])
    h_s[...] = (x * lax.rsqrt(ms + NORM_EPS) * gain + shift_ref[...]).astype(BF16)

    cbufs = (cq, ck, cv)
    zbuf = (z0, z1)

    @pl.when(pl.program_id(1) == 0)
    def _():
        for cb in cbufs:
            cb[0:8, :] = jnp.zeros((8, HEADW), F32)

    def mm_pieces(dst, row0, off, width):
        def piece(c0):
            c1 = min(c0 + GMAT_W, width)
            dst[row0:row0 + tm, c0:c1] = _dot(h_s[...], w_ref[:, off + c0:off + c1])
        return [functools.partial(piece, c0) for c0 in range(0, width, GMAT_W)]

    def conv_pieces(g, out_ref):
        cb = cbufs[g]

        def piece(hh):
            hs = slice(hh * LANES, (hh + 1) * LANES)
            cs = slice(g * HEADW + hh * LANES, g * HEADW + (hh + 1) * LANES)
            for r0 in range(0, tm, PROJ_RB):
                acc = convw_ref[3:4, cs] * cb[8 + r0:8 + r0 + PROJ_RB, hs]
                for j in range(CONV_K - 1):
                    acc = acc + convw_ref[j:j + 1, cs] * cb[5 + j + r0:5 + j + r0 + PROJ_RB, hs]
                a = _silu(acc)
                if g < 2:
                    a = a * lax.rsqrt(jnp.sum(a * a, axis=-1, keepdims=True) + NORM_EPS)
                    if g == 0:
                        a = a * (GDN_DK ** -0.5)
                out_ref[r0:r0 + PROJ_RB, hs] = a.astype(BF16)
            cb[0:8, hs] = cb[tm:tm + 8, hs]
        return [functools.partial(piece, hh) for hh in range(GDN_HEADS)]

    def gate_pieces(slot, out_ref):
        def piece(r0):
            rs = slice(r0, r0 + PROJ_RB)
            out_ref[rs, :] = _silu(zbuf[slot][rs, :]).astype(BF16)
        return [functools.partial(piece, r0) for r0 in range(0, tm, PROJ_RB)]

    def qk_pieces(slot, g_ref, out_ref, post):
        def piece(c0):
            z = zbuf[slot][:, c0:c0 + GMAT_W]
            gm_s[:, c0:c0 + GMAT_W] = _dot((z * z).astype(BF16), gmat_ref[...])
            for r0 in range(0, tm, PROJ_RB):
                rs = slice(r0, r0 + PROJ_RB)
                for h0 in range(c0, c0 + GMAT_W, LANES):
                    hs = slice(h0, h0 + LANES)
                    ys = zbuf[slot][rs, hs] * lax.rsqrt(gm_s[rs, hs] + NORM_EPS) * g_ref[:, hs]
                    r = (ys * tab[0, rs, :] + pltpu.roll(ys, LANES - ROT_DIM // 2, axis=1) * tab[1, rs, :]
                         + pltpu.roll(ys, ROT_DIM // 2, axis=1) * tab[2, rs, :])
                    if post is not None:
                        r = r * post
                    out_ref[rs, hs] = r.astype(BF16)
        return [functools.partial(piece, c0) for c0 in range(0, HEADW, GMAT_W)]

    def table_pieces():
        def piece(r0):
            rs = slice(r0, r0 + PROJ_RB)
            ang = pos_ref[rs, :].astype(F32) * freq_ref[...]
            sin_t = jnp.sin(ang)
            tab[0, rs, :] = jnp.cos(ang)
            tab[1, rs, :] = sin_t * sgn_ref[0:1, :]
            tab[2, rs, :] = sin_t * sgn_ref[1:2, :]
        return [functools.partial(piece, r0) for r0 in range(0, tm, PROJ_RB)]

    def ep_bd(slot):
        z = zbuf[slot][:, 0:LANES]
        lane = lax.broadcasted_iota(jnp.int32, (1, LANES), 1)
        beta = 1.0 / (1.0 + jnp.exp(-z))
        zz = z + dtb_ref[...]
        softplus = jnp.maximum(zz, 0.0) + jnp.log(1.0 + jnp.exp(-jnp.abs(zz)))
        is_g = (lane >= GDN_HEADS) & (lane < 2 * GDN_HEADS)
        g = jnp.where(is_g, -jnp.exp(alog_ref[...]) * softplus, 0.0)
        g_hi = g.astype(BF16)
        g_lo = (g - g_hi.astype(F32)).astype(BF16)
        gc = _dot(lmat_ref[...], g_hi) + _dot(lmat_ref[...], g_lo)
        bg = jnp.where(lane < GDN_HEADS, beta, gc)
        bg_ref[...] = bg
        bgt_ref[...] = bg.T[0:8, :]

    def cast_bv():
        bv_ref[...] = zbuf[0][...].astype(BF16)

    def emit(*lists):
        for k in range(max(len(pl_) for pl_ in lists)):
            for pl_ in lists:
                if k < len(pl_):
                    pl_[k]()

    q_scale = DIFF_D ** -0.5 * math.log2(math.e)
    stages = [
        (mm_pieces(cq, 8, OFF_GQKV, HEADW), conv_pieces(0, gq_ref)),
        (mm_pieces(z1, 0, OFF_BD, LANES) + mm_pieces(ck, 8, OFF_GQKV + HEADW, HEADW),
         conv_pieces(1, gk_ref) + [functools.partial(ep_bd, 1)]),
        (mm_pieces(cv, 8, OFF_GQKV + 2 * HEADW, HEADW), conv_pieces(2, gv_ref)),
        (mm_pieces(z0, 0, OFF_BQ, HEADW), qk_pieces(0, qg_ref, bq_ref, q_scale)),
        (mm_pieces(z1, 0, OFF_BK, HEADW), qk_pieces(1, kg_ref, bk_ref, None)),
        (mm_pieces(z0, 0, OFF_AGATE, HEADW), gate_pieces(0, ag_ref)),
        (mm_pieces(z1, 0, OFF_BGATE, HEADW), gate_pieces(1, bgate_ref)),
        (mm_pieces(z0, 0, OFF_BV, HEADW), [cast_bv]),
    ]
    emit(stages[0][0], table_pieces())
    for i, (_, epilogue) in enumerate(stages):
        emit(stages[i + 1][0] if i + 1 < len(stages) else [], epilogue)


def _proj(x, pos3, shift, scale, norm_g, w_all, conv_w, alog_row, dtb_row, qg_row, kg_row,
          lmat, gmat, freq_row, sgn_rows):
    b, s, d = x.shape
    tm = PROJ_TM
    row = lambda bi, si: (bi, si, 0)
    const2 = lambda bi, si: (0, 0)
    per_b = lambda bi, si: (bi, 0, 0)
    hw = jax.ShapeDtypeStruct((b, s, HEADW), BF16)
    out_shape = [hw] * 8 + [jax.ShapeDtypeStruct((b, s, LANES), F32),
                            jax.ShapeDtypeStruct((b, 8, s), F32)]
    hw_spec = pl.BlockSpec((None, tm, HEADW), row)
    out_specs = [hw_spec] * 8 + [pl.BlockSpec((None, tm, LANES), row),
                                 pl.BlockSpec((None, 8, tm), lambda bi, si: (bi, 0, si))]
    in_specs = [
        pl.BlockSpec((None, tm, d), row),
        pl.BlockSpec((None, tm, 1), row),
        pl.BlockSpec((None, 1, d), per_b),
        pl.BlockSpec((None, 1, d), per_b),
        pl.BlockSpec((1, d), const2),
        pl.BlockSpec((d, W_COLS), const2),
        pl.BlockSpec((CONV_K, GDN_QKV), const2),
        pl.BlockSpec((1, LANES), const2),
        pl.BlockSpec((1, LANES), const2),
        pl.BlockSpec((1, HEADW), const2),
        pl.BlockSpec((1, HEADW), const2),
        pl.BlockSpec((tm, tm), const2),
        pl.BlockSpec((GMAT_W, GMAT_W), const2),
        pl.BlockSpec((1, LANES), const2),
        pl.BlockSpec((2, LANES), const2),
    ]
    return pl.pallas_call(
        _proj_kernel,
        grid=(b, s // tm),
        in_specs=in_specs,
        out_specs=out_specs,
        out_shape=out_shape,
        scratch_shapes=[pltpu.VMEM((tm + 8, HEADW), F32)] * 3 + [pltpu.VMEM((tm, HEADW), F32)] * 2
        + [pltpu.VMEM((tm, d), BF16), pltpu.VMEM((tm, HEADW), F32), pltpu.VMEM((3, tm, LANES), F32)],
        compiler_params=pltpu.CompilerParams(
            dimension_semantics=("arbitrary", "arbitrary"), vmem_limit_bytes=VMEM_LIMIT),
        name="proj",
    )(x, pos3, shift, scale, norm_g, w_all, conv_w, alog_row, dtb_row, qg_row, kg_row,
      lmat, gmat, freq_row, sgn_rows)


def _split(a):
    hi = a.astype(BF16)
    return hi, (a - hi.astype(F32)).astype(BF16)


def _gdn_kernel(q_ref, k_ref, v_ref, gate_ref, bg_ref, bgt_ref, ng_ref, o_ref,
                state, wq_s, u_s, kva_s, vn_s):
    nb, tc = q_ref.shape[0], q_ref.shape[1]
    npair = tc // PAIR
    nc = tc // CHUNK

    @pl.when(pl.program_id(0) == 0)
    def _():
        state[...] = jnp.zeros(state.shape, F32)
        vn_s[...] = jnp.zeros(vn_s.shape, BF16)

    ii = lax.broadcasted_iota(jnp.int32, (CHUNK, PAIR), 0)
    lane_w = lax.broadcasted_iota(jnp.int32, (CHUNK, PAIR), 1)
    jj = lane_w % CHUNK
    left_w = lane_w < CHUNK
    left_p = lax.broadcasted_iota(jnp.int32, (PAIR, PAIR), 1) < CHUNK
    top_col = lax.broadcasted_iota(jnp.int32, (PAIR, 1), 0) < CHUNK
    eye = (ii == jj).astype(F32)
    lower = ii >= jj
    strict = ii > jj

    def widen(m):
        return jnp.where(left_w, m[0:CHUNK], m[CHUNK:PAIR])

    def bdiag(wd):
        z = jnp.zeros_like(wd)
        return jnp.concatenate([jnp.where(left_w, wd, z), jnp.where(left_w, z, wd)], axis=0)

    def mm(parts, rhs):
        n = len(parts)
        b_hi, b_lo = bdiag(rhs[0]), bdiag(rhs[1])
        his = [hi_ for hi_, _ in parts]
        r_hi = _dot(jnp.concatenate(his + [lo_ for _, lo_ in parts], axis=0), b_hi)
        r_lo = _dot(jnp.concatenate(his, axis=0) if n > 1 else his[0], b_lo)
        blk = lambda r, i: r[i * CHUNK:(i + 1) * CHUNK]
        return [blk(r_hi, i) + blk(r_hi, n + i) + blk(r_lo, i) for i in range(n)]

    items = [(bb, hh, p) for bb in range(nb) for hh in range(GDN_HEADS) for p in range(npair)]
    pre = []
    for bb, hh, p in items:
        cs = slice(hh * LANES, (hh + 1) * LANES)
        rs = slice(p * PAIR, (p + 1) * PAIR)
        q2, k2 = q_ref[bb, rs, cs], k_ref[bb, rs, cs]
        gcol2 = bg_ref[bb, rs, GDN_HEADS + hh:GDN_HEADS + hh + 1]
        bcol_w = widen(jnp.broadcast_to(bg_ref[bb, rs, hh:hh + 1], (PAIR, PAIR)))
        gcol_w = widen(jnp.broadcast_to(gcol2, (PAIR, PAIR)))
        grow = bgt_ref[bb, GDN_HEADS + hh:GDN_HEADS + hh + 1, rs]
        dec = jnp.where(lower, jnp.exp(jnp.minimum(gcol_w - grow, 0.0)), 0.0)
        gram = _dot_nt(jnp.concatenate([k2, q2], axis=0), k2)
        xm = jnp.where(strict, -(bcol_w * widen(gram[0:PAIR]) * dec), 0.0)
        a_w = jnp.where(lower, widen(gram[PAIR:2 * PAIR]) * dec, 0.0)
        pre.append((xm, a_w))

    def decayed_operands(item, a_w):
        bb, hh, p = item
        cs = slice(hh * LANES, (hh + 1) * LANES)
        rs = slice(p * PAIR, (p + 1) * PAIR)
        q2, k2 = q_ref[bb, rs, cs], k_ref[bb, rs, cs]
        gcol2 = bg_ref[bb, rs, GDN_HEADS + hh:GDN_HEADS + hh + 1]
        grow = bgt_ref[bb, GDN_HEADS + hh:GDN_HEADS + hh + 1, rs]
        g_end = jnp.where(top_col, grow[:, CHUNK - 1:CHUNK], grow[:, PAIR - 1:PAIR])
        e_g = jnp.exp(gcol2)
        k2f = k2.astype(F32)
        qd = (q2.astype(F32) * e_g).astype(BF16)
        kdt = (k2f * jnp.exp(g_end - gcol2)).T
        a_bd = bdiag(a_w)
        kva_s[bb, hh, 2 * p] = jnp.concatenate(
            [jnp.where(left_p, kdt, 0.0), a_bd[0:CHUNK]], axis=0).astype(BF16)
        kva_s[bb, hh, 2 * p + 1] = jnp.concatenate(
            [jnp.where(left_p, 0.0, kdt), a_bd[CHUNK:PAIR]], axis=0).astype(BF16)
        return qd, (k2f * e_g).astype(BF16)

    levels = 4
    per_level = -(-len(items) // levels)
    side = []
    ts = [eye + xm for xm, _ in pre]
    xs = [_split(xm) for xm, _ in pre]
    ps = [_split(mm([xp], xp)[0]) for xp in xs]
    for lvl in range(levels):
        both = [mm([_split(t), pp], pp) for t, pp in zip(ts, ps)]
        for idx in range(lvl * per_level, min((lvl + 1) * per_level, len(items))):
            side.append(decayed_operands(items[idx], pre[idx][1]))
        ts = [t + tp for t, (tp, _) in zip(ts, both)]
        ps = [_split(sq) for _, sq in both]
    ts = [t + mm([_split(t)], pp)[0] for t, pp in zip(ts, ps)]
    for (bb, hh, p), t, (qd, kg) in zip(items, ts, side):
        cs = slice(hh * LANES, (hh + 1) * LANES)
        rs = slice(p * PAIR, (p + 1) * PAIR)
        tb = bdiag(t * bgt_ref[bb, hh:hh + 1, rs]).astype(BF16)
        uw = _dot(tb, jnp.concatenate([v_ref[bb, rs, cs], kg], axis=1))
        u_s[bb, hh, rs, :] = uw[:, 0:GDN_DV]
        w2 = uw[:, GDN_DV:GDN_DV + GDN_DK].astype(BF16)
        for e in range(2):
            es = slice(e * CHUNK, (e + 1) * CHUNK)
            wq_s[bb, hh, 2 * p + e] = jnp.concatenate([w2[es], qd[es]], axis=0)

    chains = [(bb, hh) for bb in range(nb) for hh in range(GDN_HEADS)]
    for c in range(nc):
        e = c % 2
        rs = slice(c * CHUNK, (c + 1) * CHUNK)
        es = slice(e * CHUNK, (e + 1) * CHUNK)
        sts = [state[ch] for ch in chains]
        rr = [_dot(wq_s[bb, hh, c], st.astype(BF16)) for (bb, hh), st in zip(chains, sts)]
        for (bb, hh), r in zip(chains, rr):
            vn_s[bb, hh, es, :] = (u_s[bb, hh, rs, :] - r[0:CHUNK]).astype(BF16)
        for (bb, hh), st, r in zip(chains, sts, rr):
            cs = slice(hh * LANES, (hh + 1) * LANES)
            g_end = bgt_ref[bb, GDN_HEADS + hh:GDN_HEADS + hh + 1, (c + 1) * CHUNK - 1:(c + 1) * CHUNK]
            kva = _dot(kva_s[bb, hh, c], vn_s[bb, hh])
            state[bb, hh] = st * jnp.exp(g_end) + kva[0:GDN_DK]
            o = r[CHUNK:PAIR] + kva[GDN_DK:GDN_DK + CHUNK]
            on = o * lax.rsqrt(jnp.mean(o * o, axis=-1, keepdims=True) + NORM_EPS) * ng_ref[...]
            o_ref[bb, rs, cs] = (on * gate_ref[bb, rs, cs].astype(F32)).astype(BF16)


def _gdn(gq, gk, gv, agate, bg, bgt, ng_row):
    b, s, _ = gq.shape
    tc = GDN_TC
    row = lambda si: (0, si, 0)
    hw_spec = pl.BlockSpec((b, tc, HEADW), row)
    per_chain = (b, GDN_HEADS)
    return pl.pallas_call(
        _gdn_kernel,
        grid=(s // tc,),
        in_specs=[hw_spec, hw_spec, hw_spec, hw_spec,
                  pl.BlockSpec((b, tc, LANES), row),
                  pl.BlockSpec((b, 8, tc), lambda si: (0, 0, si)),
                  pl.BlockSpec((1, GDN_DV), lambda si: (0, 0))],
        out_specs=hw_spec,
        out_shape=jax.ShapeDtypeStruct((b, s, HEADW), BF16),
        scratch_shapes=[pltpu.VMEM(per_chain + (GDN_DK, GDN_DV), F32),
                        pltpu.VMEM(per_chain + (tc // CHUNK, PAIR, GDN_DK), BF16),
                        pltpu.VMEM(per_chain + (tc, GDN_DV), F32),
                        pltpu.VMEM(per_chain + (tc // CHUNK, GDN_DK + CHUNK, PAIR), BF16),
                        pltpu.VMEM(per_chain + (PAIR, GDN_DV), BF16)],
        compiler_params=pltpu.CompilerParams(
            dimension_semantics=("arbitrary",), vmem_limit_bytes=VMEM_LIMIT),
        name="gdn",
    )(gq, gk, gv, agate, bg, bgt, ng_row)


def _attn_kernel(q_ref, k_ref, v_ref, gate_ref, lamp_ref, sg_ref, gqk_ref, o_ref,
                 vext, acc, pbuf, kaug_s, *, lambda_init, tq):
    s_len = q_ref.shape[0]
    nq = s_len // tq
    tk = tq
    nch = tq // CHUNK
    w = 2 * DIFF_D

    vext[:, 0:w] = v_ref[...]
    vext[:, w:2 * w] = jnp.ones((s_len, w), BF16)

    @pl.when(jnp.logical_and(pl.program_id(0) == 0, pl.program_id(1) == 0))
    def _():
        acc[...] = jnp.zeros(acc.shape, F32)

    lane = lax.broadcasted_iota(jnp.int32, (tk, w), 1)
    rchunk = lax.broadcasted_iota(jnp.int32, (tk, w), 0) // CHUNK
    lo = lane < DIFF_D
    hi = jnp.logical_not(lo)
    half = (lo, hi)

    gqk = gqk_ref[...]
    bound = (jnp.max(jnp.abs(gqk[0:1, :])) * jnp.max(jnp.abs(gqk[1:2, :]))
             * (DIFF_D * DIFF_D ** -0.5 * math.log2(math.e) * 1.01))

    lp = lamp_ref[...]
    lam = (jnp.exp(jnp.sum(lp[0:1, :] * lp[1:2, :], axis=-1, keepdims=True))
           - jnp.exp(jnp.sum(lp[2:3, :] * lp[3:4, :], axis=-1, keepdims=True)) + lambda_init)

    def rows_of(blk):
        return pl.ds(pl.multiple_of(blk * tq, tq), tq)

    def write_rows(qi, a0, a1):
        o = a0[:, 0:w] / a0[:, w:2 * w] - lam * (a1[:, 0:w] / a1[:, w:2 * w])
        on = o * lax.rsqrt(jnp.mean(o * o, axis=-1, keepdims=True) + NORM_EPS) * sg_ref[...]
        on = on * (1.0 - lambda_init)
        o_ref[rows_of(qi), :] = (on * gate_ref[rows_of(qi), :].astype(F32)).astype(BF16)
        acc[qi] = jnp.zeros(acc.shape[1:], F32)


    @pl.when(bound <= MAX_STATIC_SHIFT)
    def _():
        q_aug = []
        for sub, base in enumerate((DIFF_D, 0)):
            mlane = lane - (base + 1)
            in_mask = (mlane >= 0) & (mlane < nch)
            q_aug.append(jnp.where((lane == base) | (in_mask & (rchunk == mlane)), 1.0, 0.0).astype(BF16))
            plain = jnp.where(lane == base, -bound, 0.0)
            kaug_s[0, sub] = plain.astype(BF16)
            kaug_s[1, sub] = jnp.where(in_mask & (rchunk > mlane), -MASK_BIG, plain).astype(BF16)

        def scores(qi, j, pslot):
            q = q_ref[rows_of(qi), :]
            kb = k_ref[rows_of(j), :]
            is_diag = jnp.asarray(j == qi, jnp.int32)
            for sub in range(2):
                qs = jnp.where(half[sub], q, q_aug[sub])
                ks = jnp.where(half[sub], kb, kaug_s[is_diag, sub])
                pbuf[pslot, sub] = jnp.exp2(_dot_nt(qs, ks)).astype(BF16)

        def accumulate(qi, j, pslot):
            ve = vext[rows_of(j), :]
            for sub in range(2):
                acc[qi, sub] += _dot(pbuf[pslot, sub], ve)

        def advance(st):
            qi, j, pqi, pj, n = st
            accumulate(pqi, pj, n & 1)
            scores(qi, j, (n + 1) & 1)
            wrap = j >= qi
            return (jnp.where(wrap, qi + 1, qi), jnp.where(wrap, 0, j + 1), qi, j, n + 1)

        todo = nq * (nq + 1) // 2 - 1
        unroll = max([u for u in range(ATT_UNROLL // 2, 2 * ATT_UNROLL) if todo % u == 0] or [ATT_UNROLL],
                     key=lambda u: -abs(u - ATT_UNROLL))

        def trip(t, st):
            for _ in range(unroll):
                st = advance(st)
            return st

        scores(0, 0, 0)
        one, zero = jnp.int32(1), jnp.int32(0)
        trips, rest = divmod(todo, unroll)
        st = lax.fori_loop(0, trips, trip, (one, zero, zero, zero, zero))
        for _ in range(rest):
            st = advance(st)
        accumulate(st[2], st[3], st[4] & 1)

        group = 4 if nq % 4 == 0 else 1

        def write_blocks(g, carry):
            for u in range(group):
                qi = g * group + u
                write_rows(qi, acc[qi, 0], acc[qi, 1])
            return carry

        lax.fori_loop(0, nq // group, write_blocks, 0)

    @pl.when(bound > MAX_STATIC_SHIFT)
    def _():
        zero = jnp.zeros((tk, w), BF16)
        rr = lax.broadcasted_iota(jnp.int32, (tq, tk), 0) // CHUNK
        cc = lax.broadcasted_iota(jnp.int32, (tq, tk), 1) // CHUNK
        causal = cc <= rr

        def outer(qi, carry):
            q = q_ref[rows_of(qi), :]

            def step(j, ms, diag):
                kb = k_ref[rows_of(j), :]
                ve = vext[rows_of(j), :]
                new = []
                for sub in range(2):
                    s = _dot_nt(q, jnp.where(half[sub], kb, zero))
                    if diag:
                        s = jnp.where(causal, s, NEG_BIG)
                    m_new = jnp.maximum(ms[sub], jnp.max(s, axis=-1, keepdims=True))
                    p = jnp.exp2(s - m_new)
                    acc[qi, sub] = jnp.exp2(ms[sub] - m_new) * acc[qi, sub] + _dot(p.astype(BF16), ve)
                    new.append(m_new)
                return tuple(new)

            init = (jnp.full((tq, 1), NEG_BIG, F32), jnp.full((tq, 1), NEG_BIG, F32))
            ms = lax.fori_loop(0, qi, lambda j, c: step(j, c, False), init)
            step(qi, ms, True)
            write_rows(qi, acc[qi, 0], acc[qi, 1])
            return carry

        lax.fori_loop(0, nq, outer, 0)


def _attn(bq, bk, bv, bgate, lam_params, sg_row, gqk_rows, lambda_init):
    b, s, _ = bq.shape
    tq = ATT_TQ
    w = 2 * DIFF_D
    hspec = pl.BlockSpec((None, s, w), lambda bi, hi: (bi, 0, hi))
    const = lambda bi, hi: (0, 0)
    return pl.pallas_call(
        functools.partial(_attn_kernel, lambda_init=lambda_init, tq=tq),
        grid=(b, DIFF_HEADS),
        in_specs=[hspec, hspec, hspec, hspec,
                  pl.BlockSpec((8, LANES), const),
                  pl.BlockSpec((1, w), const),
                  pl.BlockSpec((2, LANES), const)],
        out_specs=hspec,
        out_shape=jax.ShapeDtypeStruct((b, s, HEADW), BF16),
        scratch_shapes=[pltpu.VMEM((s, 2 * w), BF16),
                        pltpu.VMEM((s // tq, 2, tq, 2 * w), F32),
                        pltpu.VMEM((2, 2, tq, tq), BF16),
                        pltpu.VMEM((2, 2, tq, w), BF16)],
        compiler_params=pltpu.CompilerParams(
            dimension_semantics=("arbitrary", "arbitrary"), vmem_limit_bytes=VMEM_LIMIT),
        name="attn",
    )(bq, bk, bv, bgate, lam_params, sg_row, gqk_rows)


def _out_kernel(x_ref, ma_ref, mb_ref, wa_ref, wb_ref, gate_ref, o_ref):
    y = _dot(ma_ref[...], wa_ref[...]) + _dot(mb_ref[...], wb_ref[...])
    o_ref[...] = x_ref[...] + gate_ref[...] * y


def _out(x, mixed_a, mixed_b, w_a, w_b, gate):
    b, s, d = x.shape
    tm = OUT_TM
    row = lambda bi, si: (bi, si, 0)
    const2 = lambda bi, si: (0, 0)
    return pl.pallas_call(
        _out_kernel,
        grid=(b, s // tm),
        in_specs=[pl.BlockSpec((None, tm, d), row),
                  pl.BlockSpec((None, tm, HEADW), row),
                  pl.BlockSpec((None, tm, HEADW), row),
                  pl.BlockSpec((HEADW, d), const2),
                  pl.BlockSpec((HEADW, d), const2),
                  pl.BlockSpec((None, 1, d), lambda bi, si: (bi, 0, 0))],
        out_specs=pl.BlockSpec((None, tm, d), row),
        out_shape=jax.ShapeDtypeStruct((b, s, d), F32),
        compiler_params=pltpu.CompilerParams(
            dimension_semantics=("arbitrary", "arbitrary"), vmem_limit_bytes=VMEM_LIMIT),
        name="out",
    )(x, mixed_a, mixed_b, w_a, w_b, gate)


def _constants():
    lane = jnp.arange(LANES)
    sub = lane % DIFF_D
    half = ROT_DIM // 2
    inv_freq = ROPE_THETA ** (-jnp.arange(0, ROT_DIM, 2, dtype=F32) / ROT_DIM)
    freq_row = jnp.where(sub < ROT_DIM, inv_freq[sub % half], 0.0).astype(F32)[None, :]
    sgn_rows = jnp.stack([jnp.where(sub < half, -1.0, 0.0),
                          jnp.where((sub >= half) & (sub < ROT_DIM), 1.0, 0.0)]).astype(F32)
    r = jnp.arange(PROJ_TM)
    lmat = ((r[:, None] // CHUNK == r[None, :] // CHUNK) & (r[None, :] <= r[:, None])).astype(BF16)
    g = jnp.arange(GMAT_W)
    gmat = jnp.where(g[:, None] // DIFF_D == g[None, :] // DIFF_D, 1.0 / DIFF_D, 0.0).astype(BF16)
    return freq_row, sgn_rows, lmat, gmat


def _pad_lanes(v, offset):
    return jnp.zeros((1, LANES), F32).at[0, offset:offset + v.shape[0]].set(v.astype(F32))


def _layer(x, c_pad, pos3, lambda_init, norm_g, w_ada, b_ada, w_in, conv_w, a_log, dt_bias, gdn_norm_g,
           q_norm_g, k_norm_g, lam_q1, lam_k1, lam_q2, lam_k2, subln_g, w_out):
    b, s, d = x.shape
    mod = _ada(c_pad, w_ada, b_ada[None, :])[:b]
    shift = mod[:, None, 0:d]
    scale = mod[:, None, d:2 * d]
    gate = mod[:, None, 2 * d:3 * d]

    nk = GDN_HEADS * GDN_DK
    o_beta = 2 * nk + GDN_HEADS * GDN_DV
    o_agate = o_beta + 2 * GDN_HEADS
    o_b = o_agate + GDN_HEADS * GDN_DV
    bd = jnp.pad(w_in[:, o_beta:o_agate], ((0, 0), (0, LANES - 2 * GDN_HEADS)))
    w_all = jnp.concatenate([
        w_in[:, 0:o_beta], w_in[:, o_agate:o_b],
        w_in[:, o_b:o_b + HEADW], w_in[:, o_b + HEADW:o_b + 2 * HEADW],
        w_in[:, o_b + 2 * HEADW:o_b + 3 * HEADW], w_in[:, o_b + 3 * HEADW:o_b + 4 * HEADW],
        bd], axis=1).astype(BF16)
    freq_row, sgn_rows, lmat, gmat = _constants()
    lam_params = jnp.zeros((8, LANES), F32)
    for r, v in enumerate((lam_q1, lam_k1, lam_q2, lam_k2)):
        lam_params = lam_params.at[r, 0:DIFF_D].set(v.astype(F32))

    gq, gk, gv, agate, bq, bk, bv, bgate, bg, bgt = _proj(
        x, pos3, shift, scale, norm_g[None, :], w_all, conv_w,
        _pad_lanes(a_log, GDN_HEADS), _pad_lanes(dt_bias, GDN_HEADS),
        jnp.tile(q_norm_g, 2 * DIFF_HEADS)[None, :], jnp.tile(k_norm_g, 2 * DIFF_HEADS)[None, :],
        lmat, gmat, freq_row, sgn_rows)
    mixed_a = _gdn(gq, gk, gv, agate, bg, bgt, gdn_norm_g[None, :])
    gqk_rows = jnp.concatenate([_pad_lanes(q_norm_g, 0), _pad_lanes(k_norm_g, 0)], axis=0)
    mixed_b = _attn(bq, bk, bv, bgate, lam_params, subln_g[None, :], gqk_rows, lambda_init)
    w_o = w_out.astype(BF16)
    return _out(x, mixed_a, mixed_b, w_o[:HEADW], w_o[HEADW:], gate)


def kernel(x, c, positions, norm_g, w_ada, b_ada, w_in, conv_w, a_log, dt_bias, gdn_norm_g, q_norm_g,
           k_norm_g, lambda_q1, lambda_k1, lambda_q2, lambda_k2, subln_g, w_out):
    b, s, d = x.shape
    for tile in (PROJ_TM, GDN_TC, ATT_TQ, OUT_TM):
        assert s % tile == 0, (s, tile)
    assert d == 2 * HEADW and w_in.shape[-1] == W_COLS - LANES + 2 * GDN_HEADS
    c_pad = jnp.pad(c, ((0, 8 - b % 8 if b % 8 else 0), (0, 0)))
    pos3 = positions[:, :, None]
    for l in range(norm_g.shape[0]):
        lambda_init = 0.8 - 0.6 * math.exp(-0.3 * l)
        x = _layer(x, c_pad, pos3, lambda_init, norm_g[l], w_ada[l], b_ada[l], w_in[l], conv_w[l],
                   a_log[l], dt_bias[l], gdn_norm_g[l], q_norm_g[l], k_norm_g[l], lambda_q1[l],
                   lambda_k1[l], lambda_q2[l], lambda_k2[l], subln_g[l], w_out[l])
    return x
```

```python
import functools
import math

import jax
import jax.numpy as jnp
import numpy as np
from jax import lax
from jax.experimental import pallas as pl
from jax.experimental.pallas import tpu as pltpu

F32 = jnp.float32
BF16 = jnp.bfloat16
HIGHEST = lax.Precision.HIGHEST

CHUNK = 64
PAIR = 2 * CHUNK
GDN_HEADS = 4
GDN_DK = 128
GDN_DV = 128
CONV_K = 4
DIFF_HEADS = 4
DIFF_D = 64
ROT_DIM = 16
ROPE_THETA = 500000.0
NORM_EPS = 1e-6
LANES = 128
NEG_BIG = -1e30
MASK_BIG = 256.0

GDN_QKV = 3 * GDN_HEADS * GDN_DK
HEADW = 512
OFF_GQKV = 0
OFF_AGATE = GDN_QKV
OFF_BQ = OFF_AGATE + HEADW
OFF_BK = OFF_BQ + HEADW
OFF_BV = OFF_BK + HEADW
OFF_BGATE = OFF_BV + HEADW
OFF_BD = OFF_BGATE + HEADW
W_COLS = OFF_BD + LANES

PROJ_TM = 512
PROJ_RB = 128
GMAT_W = 256
GDN_TC = 128
ATT_TQ = 512
ATT_UNROLL = 8
MAX_STATIC_SHIFT = 60.0
OUT_TM = 1024
VMEM_LIMIT = 48 * 1024 * 1024


def _silu(x):
    h = 0.5 * x
    return h + h * jnp.tanh(h)


def _dot(a, b, **kw):
    return jnp.dot(a, b, preferred_element_type=F32, **kw)


def _dot_nt(a, b, **kw):
    return lax.dot_general(a, b, (((1,), (1,)), ((), ())), preferred_element_type=F32, **kw)


def _dot_tn(a, b, **kw):
    return lax.dot_general(a, b, (((0,), (0,)), ((), ())), preferred_element_type=F32, **kw)


def _ada_kernel(c_ref, w_ref, b_ref, o_ref):
    o_ref[...] = _dot(_silu(c_ref[...]), w_ref[...], precision=HIGHEST) + b_ref[...]


def _ada(c_pad, w_ada, b_ada):
    rows, d = c_pad.shape
    n = w_ada.shape[1]
    tn = 512
    return pl.pallas_call(
        _ada_kernel,
        grid=(n // tn,),
        in_specs=[pl.BlockSpec((rows, d), lambda j: (0, 0)),
                  pl.BlockSpec((d, tn), lambda j: (0, j)),
                  pl.BlockSpec((1, tn), lambda j: (0, j))],
        out_specs=pl.BlockSpec((rows, tn), lambda j: (0, j)),
        out_shape=jax.ShapeDtypeStruct((rows, n), F32),
        name="ada",
    )(c_pad, w_ada, b_ada)


def _proj_kernel(x_ref, pos_ref, shift_ref, scale_ref, ng_ref, w_ref, convw_ref, alog_ref, dtb_ref,
                 qg_ref, kg_ref, lmat_ref, gmat_ref, freq_ref, sgn_ref, idx_ref,
                 gq_ref, gk_ref, gv_ref, ag_ref, bq_ref, bk_ref, bv_ref, bgate_ref, bg_ref, bgt_ref,
                 cq, ck, cv, z0, z1, h_s, gm_s, tab):
    tm = x_ref.shape[0]
    x = x_ref[...]
    ms = jnp.mean(x * x, axis=-1, keepdims=True)
    gain = ng_ref[...] * (1.0 + scale_ref[...])
    h_s[...] = (x * lax.rsqrt(ms + NORM_EPS) * gain + shift_ref[...]).astype(BF16)

    cbufs = (cq, ck, cv)
    zbuf = (z0, z1)

    @pl.when(pl.program_id(1) == 0)
    def _():
        for cb in cbufs:
            cb[0:8, :] = jnp.zeros((8, HEADW), F32)

    def mm_pieces(dst, row0, off, width):
        def piece(c0):
            c1 = min(c0 + GMAT_W, width)
            dst[row0:row0 + tm, c0:c1] = _dot(h_s[...], w_ref[:, off + c0:off + c1])
        return [functools.partial(piece, c0) for c0 in range(0, width, GMAT_W)]

    def conv_pieces(g, out_ref):
        cb = cbufs[g]

        def piece(hh):
            hs = slice(hh * LANES, (hh + 1) * LANES)
            cs = slice(g * HEADW + hh * LANES, g * HEADW + (hh + 1) * LANES)
            for r0 in range(0, tm, PROJ_RB):
                acc = convw_ref[3:4, cs] * cb[8 + r0:8 + r0 + PROJ_RB, hs]
                for j in range(CONV_K - 1):
                    acc = acc + convw_ref[j:j + 1, cs] * cb[5 + j + r0:5 + j + r0 + PROJ_RB, hs]
                a = _silu(acc)
                if g < 2:
                    a = a * lax.rsqrt(jnp.sum(a * a, axis=-1, keepdims=True) + NORM_EPS)
                    if g == 0:
                        a = a * (GDN_DK ** -0.5)
                out_ref[r0:r0 + PROJ_RB, hs] = a.astype(BF16)
            cb[0:8, hs] = cb[tm:tm + 8, hs]
        return [functools.partial(piece, hh) for hh in range(GDN_HEADS)]

    def gate_pieces(slot, out_ref):
        def piece(r0):
            rs = slice(r0, r0 + PROJ_RB)
            out_ref[rs, :] = _silu(zbuf[slot][rs, :]).astype(BF16)
        return [functools.partial(piece, r0) for r0 in range(0, tm, PROJ_RB)]

    def qk_pieces(slot, g_ref, out_ref, post):
        def piece(c0):
            z = zbuf[slot][:, c0:c0 + GMAT_W]
            gm_s[:, c0:c0 + GMAT_W] = _dot((z * z).astype(BF16), gmat_ref[...])
            for r0 in range(0, tm, PROJ_RB):
                rs = slice(r0, r0 + PROJ_RB)
                for h0 in range(c0, c0 + GMAT_W, LANES):
                    hs = slice(h0, h0 + LANES)
                    ys = zbuf[slot][rs, hs] * lax.rsqrt(gm_s[rs, hs] + NORM_EPS) * g_ref[:, hs]
                    r = (ys * tab[0, rs, :] + pltpu.roll(ys, LANES - ROT_DIM // 2, axis=1) * tab[1, rs, :]
                         + pltpu.roll(ys, ROT_DIM // 2, axis=1) * tab[2, rs, :])
                    if post is not None:
                        r = r * post
                    out_ref[rs, hs] = r.astype(BF16)
        return [functools.partial(piece, c0) for c0 in range(0, HEADW, GMAT_W)]

    def table_pieces():
        def piece(r0):
            rs = slice(r0, r0 + LANES)
            ang = freq_ref[...] * pos_ref[:, rs].astype(F32)
            pad = jnp.zeros((LANES - ang.shape[0], LANES), F32)
            idx = jnp.broadcast_to(idx_ref[...], (LANES, LANES))
            cos_t = jnp.take_along_axis(jnp.concatenate([jnp.cos(ang), pad], axis=0).T, idx, axis=1)
            sin_t = jnp.take_along_axis(jnp.concatenate([jnp.sin(ang), pad], axis=0).T, idx, axis=1)
            tab[0, rs, :] = cos_t
            tab[1, rs, :] = sin_t * sgn_ref[0:1, :]
            tab[2, rs, :] = sin_t * sgn_ref[1:2, :]
        return [functools.partial(piece, r0) for r0 in range(0, tm, LANES)]

    def ep_bd(slot):
        z = zbuf[slot][:, 0:LANES]
        lane = lax.broadcasted_iota(jnp.int32, (1, LANES), 1)
        beta = 1.0 / (1.0 + jnp.exp(-z))
        zz = z + dtb_ref[...]
        softplus = jnp.maximum(zz, 0.0) + jnp.log(1.0 + jnp.exp(-jnp.abs(zz)))
        is_g = (lane >= GDN_HEADS) & (lane < 2 * GDN_HEADS)
        g = jnp.where(is_g, -jnp.exp(alog_ref[...]) * softplus, 0.0)
        g_hi = g.astype(BF16)
        g_lo = (g - g_hi.astype(F32)).astype(BF16)
        gc = _dot(lmat_ref[...], g_hi) + _dot(lmat_ref[...], g_lo)
        bg = jnp.where(lane < GDN_HEADS, beta, gc)
        bg_ref[...] = bg
        bgt_ref[...] = bg.T[0:8, :]

    def cast_bv():
        bv_ref[...] = zbuf[0][...].astype(BF16)

    def emit(*lists):
        for k in range(max(len(pl_) for pl_ in lists)):
            for pl_ in lists:
                if k < len(pl_):
                    pl_[k]()

    q_scale = DIFF_D ** -0.5 * math.log2(math.e)
    stages = [
        (mm_pieces(cq, 8, OFF_GQKV, HEADW), conv_pieces(0, gq_ref)),
        (mm_pieces(z1, 0, OFF_BD, LANES) + mm_pieces(ck, 8, OFF_GQKV + HEADW, HEADW),
         conv_pieces(1, gk_ref) + [functools.partial(ep_bd, 1)]),
        (mm_pieces(cv, 8, OFF_GQKV + 2 * HEADW, HEADW), conv_pieces(2, gv_ref)),
        (mm_pieces(z0, 0, OFF_BQ, HEADW), qk_pieces(0, qg_ref, bq_ref, q_scale)),
        (mm_pieces(z1, 0, OFF_BK, HEADW), qk_pieces(1, kg_ref, bk_ref, None)),
        (mm_pieces(z0, 0, OFF_AGATE, HEADW), gate_pieces(0, ag_ref)),
        (mm_pieces(z1, 0, OFF_BGATE, HEADW), gate_pieces(1, bgate_ref)),
        (mm_pieces(z0, 0, OFF_BV, HEADW), [cast_bv]),
    ]
    emit(stages[0][0], table_pieces())
    for i, (_, epilogue) in enumerate(stages):
        emit(stages[i + 1][0] if i + 1 < len(stages) else [], epilogue)


def _proj(x, pos_row, shift, scale, norm_g, w_all, conv_w, alog_row, dtb_row, qg_row, kg_row,
          lmat, gmat, freq_rows, sgn_rows, idx_row):
    b, s, d = x.shape
    tm = PROJ_TM
    row = lambda bi, si: (bi, si, 0)
    const2 = lambda bi, si: (0, 0)
    per_b = lambda bi, si: (bi, 0, 0)
    hw = jax.ShapeDtypeStruct((b, s, HEADW), BF16)
    out_shape = [hw] * 8 + [jax.ShapeDtypeStruct((b, s, LANES), F32),
                            jax.ShapeDtypeStruct((b, 8, s), F32)]
    hw_spec = pl.BlockSpec((None, tm, HEADW), row)
    out_specs = [hw_spec] * 8 + [pl.BlockSpec((None, tm, LANES), row),
                                 pl.BlockSpec((None, 8, tm), lambda bi, si: (bi, 0, si))]
    in_specs = [
        pl.BlockSpec((None, tm, d), row),
        pl.BlockSpec((None, 1, tm), lambda bi, si: (bi, 0, si)),
        pl.BlockSpec((None, 1, d), per_b),
        pl.BlockSpec((None, 1, d), per_b),
        pl.BlockSpec((1, d), const2),
        pl.BlockSpec((d, W_COLS), const2),
        pl.BlockSpec((CONV_K, GDN_QKV), const2),
        pl.BlockSpec((1, LANES), const2),
        pl.BlockSpec((1, LANES), const2),
        pl.BlockSpec((1, HEADW), const2),
        pl.BlockSpec((1, HEADW), const2),
        pl.BlockSpec((tm, tm), const2),
        pl.BlockSpec((GMAT_W, GMAT_W), const2),
        pl.BlockSpec((2 * (ROT_DIM // 2), LANES), const2),
        pl.BlockSpec((2, LANES), const2),
        pl.BlockSpec((1, LANES), const2),
    ]
    return pl.pallas_call(
        _proj_kernel,
        grid=(b, s // tm),
        in_specs=in_specs,
        out_specs=out_specs,
        out_shape=out_shape,
        scratch_shapes=[pltpu.VMEM((tm + 8, HEADW), F32)] * 3 + [pltpu.VMEM((tm, HEADW), F32)] * 2
        + [pltpu.VMEM((tm, d), BF16), pltpu.VMEM((tm, HEADW), F32), pltpu.VMEM((3, tm, LANES), F32)],
        compiler_params=pltpu.CompilerParams(
            dimension_semantics=("arbitrary", "arbitrary"), vmem_limit_bytes=VMEM_LIMIT),
        name="proj",
    )(x, pos_row, shift, scale, norm_g, w_all, conv_w, alog_row, dtb_row, qg_row, kg_row,
      lmat, gmat, freq_rows, sgn_rows, idx_row)


def _split(a):
    hi = a.astype(BF16)
    return hi, (a - hi.astype(F32)).astype(BF16)


def _gdn_kernel(q_ref, k_ref, v_ref, gate_ref, bg_ref, bgt_ref, ng_ref, o_ref,
                state, wq_s, u_s, kva_s, vn_s):
    nb, tc = q_ref.shape[0], q_ref.shape[1]
    npair = tc // PAIR
    nc = tc // CHUNK

    @pl.when(pl.program_id(0) == 0)
    def _():
        state[...] = jnp.zeros(state.shape, F32)
        vn_s[...] = jnp.zeros(vn_s.shape, BF16)

    ii = lax.broadcasted_iota(jnp.int32, (CHUNK, PAIR), 0)
    lane_w = lax.broadcasted_iota(jnp.int32, (CHUNK, PAIR), 1)
    jj = lane_w % CHUNK
    left_w = lane_w < CHUNK
    left_p = lax.broadcasted_iota(jnp.int32, (PAIR, PAIR), 1) < CHUNK
    top_col = lax.broadcasted_iota(jnp.int32, (PAIR, 1), 0) < CHUNK
    eye = (ii == jj).astype(F32)
    lower = ii >= jj
    strict = ii > jj

    def widen(m):
        return jnp.where(left_w, m[0:CHUNK], m[CHUNK:PAIR])

    def bdiag(wd):
        z = jnp.zeros_like(wd)
        return jnp.concatenate([jnp.where(left_w, wd, z), jnp.where(left_w, z, wd)], axis=0)

    def mm(parts, rhs):
        n = len(parts)
        b_hi, b_lo = bdiag(rhs[0]), bdiag(rhs[1])
        his = [hi_ for hi_, _ in parts]
        r_hi = _dot(jnp.concatenate(his + [lo_ for _, lo_ in parts], axis=0), b_hi)
        r_lo = _dot(jnp.concatenate(his, axis=0) if n > 1 else his[0], b_lo)
        blk = lambda r, i: r[i * CHUNK:(i + 1) * CHUNK]
        return [blk(r_hi, i) + blk(r_hi, n + i) + blk(r_lo, i) for i in range(n)]

    items = [(bb, hh, p) for bb in range(nb) for hh in range(GDN_HEADS) for p in range(npair)]
    pre = []
    for bb, hh, p in items:
        cs = slice(hh * LANES, (hh + 1) * LANES)
        rs = slice(p * PAIR, (p + 1) * PAIR)
        q2, k2 = q_ref[bb, rs, cs], k_ref[bb, rs, cs]
        gcol2 = bg_ref[bb, rs, GDN_HEADS + hh:GDN_HEADS + hh + 1]
        bcol_w = widen(jnp.broadcast_to(bg_ref[bb, rs, hh:hh + 1], (PAIR, PAIR)))
        gcol_w = widen(jnp.broadcast_to(gcol2, (PAIR, PAIR)))
        grow = bgt_ref[bb, GDN_HEADS + hh:GDN_HEADS + hh + 1, rs]
        dec = jnp.where(lower, jnp.exp(jnp.minimum(gcol_w - grow, 0.0)), 0.0)
        gram = _dot_nt(jnp.concatenate([k2, q2], axis=0), k2)
        xm = jnp.where(strict, -(bcol_w * widen(gram[0:PAIR]) * dec), 0.0)
        a_w = jnp.where(lower, widen(gram[PAIR:2 * PAIR]) * dec, 0.0)
        pre.append((xm, a_w))

    def decayed_operands(item, a_w):
        bb, hh, p = item
        cs = slice(hh * LANES, (hh + 1) * LANES)
        rs = slice(p * PAIR, (p + 1) * PAIR)
        q2, k2 = q_ref[bb, rs, cs], k_ref[bb, rs, cs]
        gcol2 = bg_ref[bb, rs, GDN_HEADS + hh:GDN_HEADS + hh + 1]
        grow = bgt_ref[bb, GDN_HEADS + hh:GDN_HEADS + hh + 1, rs]
        g_end = jnp.where(top_col, grow[:, CHUNK - 1:CHUNK], grow[:, PAIR - 1:PAIR])
        e_g = jnp.exp(gcol2)
        k2f = k2.astype(F32)
        qd = (q2.astype(F32) * e_g).astype(BF16)
        kdt = (k2f * jnp.exp(g_end - gcol2)).T
        a_bd = bdiag(a_w)
        kva_s[bb, hh, 2 * p] = jnp.concatenate(
            [jnp.where(left_p, kdt, 0.0), a_bd[0:CHUNK]], axis=0).astype(BF16)
        kva_s[bb, hh, 2 * p + 1] = jnp.concatenate(
            [jnp.where(left_p, 0.0, kdt), a_bd[CHUNK:PAIR]], axis=0).astype(BF16)
        return qd, (k2f * e_g).astype(BF16)

    levels = 4
    per_level = -(-len(items) // levels)
    side = []
    ts = [eye + xm for xm, _ in pre]
    xs = [_split(xm) for xm, _ in pre]
    ps = [_split(mm([xp], xp)[0]) for xp in xs]
    for lvl in range(levels):
        both = [mm([_split(t), pp], pp) for t, pp in zip(ts, ps)]
        for idx in range(lvl * per_level, min((lvl + 1) * per_level, len(items))):
            side.append(decayed_operands(items[idx], pre[idx][1]))
        ts = [t + tp for t, (tp, _) in zip(ts, both)]
        ps = [_split(sq) for _, sq in both]
    ts = [t + mm([_split(t)], pp)[0] for t, pp in zip(ts, ps)]
    for (bb, hh, p), t, (qd, kg) in zip(items, ts, side):
        cs = slice(hh * LANES, (hh + 1) * LANES)
        rs = slice(p * PAIR, (p + 1) * PAIR)
        tb = bdiag(t * bgt_ref[bb, hh:hh + 1, rs]).astype(BF16)
        uw = _dot(tb, jnp.concatenate([v_ref[bb, rs, cs], kg], axis=1))
        u_s[bb, hh, rs, :] = uw[:, 0:GDN_DV]
        w2 = uw[:, GDN_DV:GDN_DV + GDN_DK].astype(BF16)
        for e in range(2):
            es = slice(e * CHUNK, (e + 1) * CHUNK)
            wq_s[bb, hh, 2 * p + e] = jnp.concatenate([w2[es], qd[es]], axis=0)

    chains = [(bb, hh) for bb in range(nb) for hh in range(GDN_HEADS)]
    for c in range(nc):
        e = c % 2
        rs = slice(c * CHUNK, (c + 1) * CHUNK)
        es = slice(e * CHUNK, (e + 1) * CHUNK)
        sts = [state[ch] for ch in chains]
        rr = [_dot(wq_s[bb, hh, c], st.astype(BF16)) for (bb, hh), st in zip(chains, sts)]
        for (bb, hh), r in zip(chains, rr):
            vn_s[bb, hh, es, :] = (u_s[bb, hh, rs, :] - r[0:CHUNK]).astype(BF16)
        for (bb, hh), st, r in zip(chains, sts, rr):
            cs = slice(hh * LANES, (hh + 1) * LANES)
            g_end = bgt_ref[bb, GDN_HEADS + hh:GDN_HEADS + hh + 1, (c + 1) * CHUNK - 1:(c + 1) * CHUNK]
            kva = _dot(kva_s[bb, hh, c], vn_s[bb, hh])
            state[bb, hh] = st * jnp.exp(g_end) + kva[0:GDN_DK]
            o = r[CHUNK:PAIR] + kva[GDN_DK:GDN_DK + CHUNK]
            on = o * lax.rsqrt(jnp.mean(o * o, axis=-1, keepdims=True) + NORM_EPS) * ng_ref[...]
            o_ref[bb, rs, cs] = (on * gate_ref[bb, rs, cs].astype(F32)).astype(BF16)


def _gdn(gq, gk, gv, agate, bg, bgt, ng_row):
    b, s, _ = gq.shape
    tc = GDN_TC
    row = lambda si: (0, si, 0)
    hw_spec = pl.BlockSpec((b, tc, HEADW), row)
    per_chain = (b, GDN_HEADS)
    return pl.pallas_call(
        _gdn_kernel,
        grid=(s // tc,),
        in_specs=[hw_spec, hw_spec, hw_spec, hw_spec,
                  pl.BlockSpec((b, tc, LANES), row),
                  pl.BlockSpec((b, 8, tc), lambda si: (0, 0, si)),
                  pl.BlockSpec((1, GDN_DV), lambda si: (0, 0))],
        out_specs=hw_spec,
        out_shape=jax.ShapeDtypeStruct((b, s, HEADW), BF16),
        scratch_shapes=[pltpu.VMEM(per_chain + (GDN_DK, GDN_DV), F32),
                        pltpu.VMEM(per_chain + (tc // CHUNK, PAIR, GDN_DK), BF16),
                        pltpu.VMEM(per_chain + (tc, GDN_DV), F32),
                        pltpu.VMEM(per_chain + (tc // CHUNK, GDN_DK + CHUNK, PAIR), BF16),
                        pltpu.VMEM(per_chain + (PAIR, GDN_DV), BF16)],
        compiler_params=pltpu.CompilerParams(
            dimension_semantics=("arbitrary",), vmem_limit_bytes=VMEM_LIMIT),
        name="gdn",
    )(gq, gk, gv, agate, bg, bgt, ng_row)


def _attn_kernel(q_ref, k_ref, v_ref, gate_ref, lamp_ref, sg_ref, gqk_ref, o_ref,
                 vext, acc, pbuf, kaug_s, *, lambda_init, tq):
    s_len = q_ref.shape[0]
    nq = s_len // tq
    tk = tq
    nch = tq // CHUNK
    w = 2 * DIFF_D

    vext[:, 0:w] = v_ref[...]
    vext[:, w:2 * w] = jnp.ones((s_len, w), BF16)

    @pl.when(jnp.logical_and(pl.program_id(0) == 0, pl.program_id(1) == 0))
    def _():
        acc[...] = jnp.zeros(acc.shape, F32)

    lane = lax.broadcasted_iota(jnp.int32, (tk, w), 1)
    rchunk = lax.broadcasted_iota(jnp.int32, (tk, w), 0) // CHUNK
    lo = lane < DIFF_D
    hi = jnp.logical_not(lo)
    half = (lo, hi)

    gqk = gqk_ref[...]
    bound = (jnp.max(jnp.abs(gqk[0:1, :])) * jnp.max(jnp.abs(gqk[1:2, :]))
             * (DIFF_D * DIFF_D ** -0.5 * math.log2(math.e) * 1.01))

    lp = lamp_ref[...]
    lam = (jnp.exp(jnp.sum(lp[0:1, :] * lp[1:2, :], axis=-1, keepdims=True))
           - jnp.exp(jnp.sum(lp[2:3, :] * lp[3:4, :], axis=-1, keepdims=True)) + lambda_init)

    def rows_of(blk):
        return pl.ds(pl.multiple_of(blk * tq, tq), tq)

    def write_rows(qi, a0, a1):
        o = a0[:, 0:w] / a0[:, w:2 * w] - lam * (a1[:, 0:w] / a1[:, w:2 * w])
        on = o * lax.rsqrt(jnp.mean(o * o, axis=-1, keepdims=True) + NORM_EPS) * sg_ref[...]
        on = on * (1.0 - lambda_init)
        o_ref[rows_of(qi), :] = (on * gate_ref[rows_of(qi), :].astype(F32)).astype(BF16)
        acc[qi] = jnp.zeros(acc.shape[1:], F32)


    @pl.when(bound <= MAX_STATIC_SHIFT)
    def _():
        q_aug = []
        for sub, base in enumerate((DIFF_D, 0)):
            mlane = lane - (base + 1)
            in_mask = (mlane >= 0) & (mlane < nch)
            q_aug.append(jnp.where((lane == base) | (in_mask & (rchunk == mlane)), 1.0, 0.0).astype(BF16))
            plain = jnp.where(lane == base, -bound, 0.0)
            kaug_s[0, sub] = plain.astype(BF16)
            kaug_s[1, sub] = jnp.where(in_mask & (rchunk > mlane), -MASK_BIG, plain).astype(BF16)

        def scores(qi, j, pslot):
            q = q_ref[rows_of(qi), :]
            kb = k_ref[rows_of(j), :]
            is_diag = jnp.asarray(j == qi, jnp.int32)
            for sub in range(2):
                qs = jnp.where(half[sub], q, q_aug[sub])
                ks = jnp.where(half[sub], kb, kaug_s[is_diag, sub])
                pbuf[pslot, sub] = jnp.exp2(_dot_nt(qs, ks)).astype(BF16)

        def accumulate(qi, j, pslot):
            ve = vext[rows_of(j), :]
            for sub in range(2):
                acc[qi, sub] += _dot(pbuf[pslot, sub], ve)

        def advance(st):
            qi, j, pqi, pj, n = st
            accumulate(pqi, pj, n & 1)
            scores(qi, j, (n + 1) & 1)
            wrap = j >= qi
            return (jnp.where(wrap, qi + 1, qi), jnp.where(wrap, 0, j + 1), qi, j, n + 1)

        todo = nq * (nq + 1) // 2 - 1
        unroll = max([u for u in range(ATT_UNROLL // 2, 2 * ATT_UNROLL) if todo % u == 0] or [ATT_UNROLL],
                     key=lambda u: -abs(u - ATT_UNROLL))

        def trip(t, st):
            for _ in range(unroll):
                st = advance(st)
            return st

        scores(0, 0, 0)
        one, zero = jnp.int32(1), jnp.int32(0)
        trips, rest = divmod(todo, unroll)
        st = lax.fori_loop(0, trips, trip, (one, zero, zero, zero, zero))
        for _ in range(rest):
            st = advance(st)
        accumulate(st[2], st[3], st[4] & 1)

        group = 4 if nq % 4 == 0 else 1

        def write_blocks(g, carry):
            for u in range(group):
                qi = g * group + u
                write_rows(qi, acc[qi, 0], acc[qi, 1])
            return carry

        lax.fori_loop(0, nq // group, write_blocks, 0)

    @pl.when(bound > MAX_STATIC_SHIFT)
    def _():
        zero = jnp.zeros((tk, w), BF16)
        rr = lax.broadcasted_iota(jnp.int32, (tq, tk), 0) // CHUNK
        cc = lax.broadcasted_iota(jnp.int32, (tq, tk), 1) // CHUNK
        causal = cc <= rr

        def outer(qi, carry):
            q = q_ref[rows_of(qi), :]

            def step(j, ms, diag):
                kb = k_ref[rows_of(j), :]
                ve = vext[rows_of(j), :]
                new = []
                for sub in range(2):
                    s = _dot_nt(q, jnp.where(half[sub], kb, zero))
                    if diag:
                        s = jnp.where(causal, s, NEG_BIG)
                    m_new = jnp.maximum(ms[sub], jnp.max(s, axis=-1, keepdims=True))
                    p = jnp.exp2(s - m_new)
                    acc[qi, sub] = jnp.exp2(ms[sub] - m_new) * acc[qi, sub] + _dot(p.astype(BF16), ve)
                    new.append(m_new)
                return tuple(new)

            init = (jnp.full((tq, 1), NEG_BIG, F32), jnp.full((tq, 1), NEG_BIG, F32))
            ms = lax.fori_loop(0, qi, lambda j, c: step(j, c, False), init)
            step(qi, ms, True)
            write_rows(qi, acc[qi, 0], acc[qi, 1])
            return carry

        lax.fori_loop(0, nq, outer, 0)


def _attn(bq, bk, bv, bgate, lam_params, sg_row, gqk_rows, lambda_init):
    b, s, _ = bq.shape
    tq = ATT_TQ
    w = 2 * DIFF_D
    hspec = pl.BlockSpec((None, s, w), lambda bi, hi: (bi, 0, hi))
    const = lambda bi, hi: (0, 0)
    return pl.pallas_call(
        functools.partial(_attn_kernel, lambda_init=lambda_init, tq=tq),
        grid=(b, DIFF_HEADS),
        in_specs=[hspec, hspec, hspec, hspec,
                  pl.BlockSpec((8, LANES), const),
                  pl.BlockSpec((1, w), const),
                  pl.BlockSpec((2, LANES), const)],
        out_specs=hspec,
        out_shape=jax.ShapeDtypeStruct((b, s, HEADW), BF16),
        scratch_shapes=[pltpu.VMEM((s, 2 * w), BF16),
                        pltpu.VMEM((s // tq, 2, tq, 2 * w), F32),
                        pltpu.VMEM((2, 2, tq, tq), BF16),
                        pltpu.VMEM((2, 2, tq, w), BF16)],
        compiler_params=pltpu.CompilerParams(
            dimension_semantics=("arbitrary", "arbitrary"), vmem_limit_bytes=VMEM_LIMIT),
        name="attn",
    )(bq, bk, bv, bgate, lam_params, sg_row, gqk_rows)


def _out_kernel(x_ref, ma_ref, mb_ref, wa_ref, wb_ref, gate_ref, o_ref):
    y = _dot(ma_ref[...], wa_ref[...]) + _dot(mb_ref[...], wb_ref[...])
    o_ref[...] = x_ref[...] + gate_ref[...] * y


def _out(x, mixed_a, mixed_b, w_a, w_b, gate):
    b, s, d = x.shape
    tm = OUT_TM
    row = lambda bi, si: (bi, si, 0)
    const2 = lambda bi, si: (0, 0)
    return pl.pallas_call(
        _out_kernel,
        grid=(b, s // tm),
        in_specs=[pl.BlockSpec((None, tm, d), row),
                  pl.BlockSpec((None, tm, HEADW), row),
                  pl.BlockSpec((None, tm, HEADW), row),
                  pl.BlockSpec((HEADW, d), const2),
                  pl.BlockSpec((HEADW, d), const2),
                  pl.BlockSpec((None, 1, d), lambda bi, si: (bi, 0, 0))],
        out_specs=pl.BlockSpec((None, tm, d), row),
        out_shape=jax.ShapeDtypeStruct((b, s, d), F32),
        compiler_params=pltpu.CompilerParams(
            dimension_semantics=("arbitrary", "arbitrary"), vmem_limit_bytes=VMEM_LIMIT),
        name="out",
    )(x, mixed_a, mixed_b, w_a, w_b, gate)


def _constants():
    sub = np.arange(LANES) % DIFF_D
    half = ROT_DIM // 2
    inv_freq = ROPE_THETA ** (-np.arange(0, ROT_DIM, 2, dtype=np.float32) / ROT_DIM)
    freq_rows = np.zeros((2 * half, LANES), np.float32)
    freq_rows[:half] = inv_freq[:, None]
    idx_row = np.where(sub < ROT_DIM, sub % half, half).astype(np.int32)[None, :]
    sgn_rows = np.stack([np.where(sub < half, -1.0, 0.0),
                         np.where((sub >= half) & (sub < ROT_DIM), 1.0, 0.0)]).astype(np.float32)
    r = np.arange(PROJ_TM)
    lmat = (r[:, None] // CHUNK == r[None, :] // CHUNK) & (r[None, :] <= r[:, None])
    g = np.arange(GMAT_W)
    gmat = np.where(g[:, None] // DIFF_D == g[None, :] // DIFF_D, 1.0 / DIFF_D, 0.0)
    return (jnp.asarray(freq_rows), jnp.asarray(sgn_rows), jnp.asarray(idx_row),
            jnp.asarray(lmat, BF16), jnp.asarray(gmat, BF16))


def _pad_lanes(v, offset):
    return jnp.zeros((1, LANES), F32).at[0, offset:offset + v.shape[0]].set(v.astype(F32))


def _layer(x, c_pad, pos_row, lambda_init, norm_g, w_ada, b_ada, w_in, conv_w, a_log, dt_bias, gdn_norm_g,
           q_norm_g, k_norm_g, lam_q1, lam_k1, lam_q2, lam_k2, subln_g, w_out):
    b, s, d = x.shape
    mod = _ada(c_pad, w_ada, b_ada[None, :])[:b]
    shift = mod[:, None, 0:d]
    scale = mod[:, None, d:2 * d]
    gate = mod[:, None, 2 * d:3 * d]

    nk = GDN_HEADS * GDN_DK
    o_beta = 2 * nk + GDN_HEADS * GDN_DV
    o_agate = o_beta + 2 * GDN_HEADS
    o_b = o_agate + GDN_HEADS * GDN_DV
    bd = jnp.pad(w_in[:, o_beta:o_agate], ((0, 0), (0, LANES - 2 * GDN_HEADS)))
    w_all = jnp.concatenate([
        w_in[:, 0:o_beta], w_in[:, o_agate:o_b],
        w_in[:, o_b:o_b + HEADW], w_in[:, o_b + HEADW:o_b + 2 * HEADW],
        w_in[:, o_b + 2 * HEADW:o_b + 3 * HEADW], w_in[:, o_b + 3 * HEADW:o_b + 4 * HEADW],
        bd], axis=1).astype(BF16)
    freq_rows, sgn_rows, idx_row, lmat, gmat = _constants()
    lam_params = jnp.zeros((8, LANES), F32)
    for r, v in enumerate((lam_q1, lam_k1, lam_q2, lam_k2)):
        lam_params = lam_params.at[r, 0:DIFF_D].set(v.astype(F32))

    gq, gk, gv, agate, bq, bk, bv, bgate, bg, bgt = _proj(
        x, pos_row, shift, scale, norm_g[None, :], w_all, conv_w,
        _pad_lanes(a_log, GDN_HEADS), _pad_lanes(dt_bias, GDN_HEADS),
        jnp.tile(q_norm_g, 2 * DIFF_HEADS)[None, :], jnp.tile(k_norm_g, 2 * DIFF_HEADS)[None, :],
        lmat, gmat, freq_rows, sgn_rows, idx_row)
    mixed_a = _gdn(gq, gk, gv, agate, bg, bgt, gdn_norm_g[None, :])
    gqk_rows = jnp.concatenate([_pad_lanes(q_norm_g, 0), _pad_lanes(k_norm_g, 0)], axis=0)
    mixed_b = _attn(bq, bk, bv, bgate, lam_params, subln_g[None, :], gqk_rows, lambda_init)
    w_o = w_out.astype(BF16)
    return _out(x, mixed_a, mixed_b, w_o[:HEADW], w_o[HEADW:], gate)


def kernel(x, c, positions, norm_g, w_ada, b_ada, w_in, conv_w, a_log, dt_bias, gdn_norm_g, q_norm_g,
           k_norm_g, lambda_q1, lambda_k1, lambda_q2, lambda_k2, subln_g, w_out):
    b, s, d = x.shape
    for tile in (PROJ_TM, GDN_TC, ATT_TQ, OUT_TM):
        assert s % tile == 0, (s, tile)
    assert d == 2 * HEADW and w_in.shape[-1] == W_COLS - LANES + 2 * GDN_HEADS
    c_pad = jnp.pad(c, ((0, 8 - b % 8 if b % 8 else 0), (0, 0)))
    pos_row = positions[:, None, :]
    for l in range(norm_g.shape[0]):
        lambda_init = 0.8 - 0.6 * math.exp(-0.3 * l)
        x = _layer(x, c_pad, pos_row, lambda_init, norm_g[l], w_ada[l], b_ada[l], w_in[l], conv_w[l],
                   a_log[l], dt_bias[l], gdn_norm_g[l], q_norm_g[l], k_norm_g[l], lambda_q1[l],
                   lambda_k1[l], lambda_q2[l], lambda_k2[l], subln_g[l], w_out[l])
    return x
```

```python
import functools
import math

import jax
import jax.numpy as jnp
import numpy as np
from jax import lax
from jax.experimental import pallas as pl
from jax.experimental.pallas import tpu as pltpu

F32 = jnp.float32
BF16 = jnp.bfloat16
HIGHEST = lax.Precision.HIGHEST

CHUNK = 64
PAIR = 2 * CHUNK
GDN_HEADS = 4
GDN_DK = 128
GDN_DV = 128
CONV_K = 4
DIFF_HEADS = 4
DIFF_D = 64
ROT_DIM = 16
ROPE_THETA = 500000.0
NORM_EPS = 1e-6
LANES = 128
NEG_BIG = -1e30
MASK_BIG = 256.0

GDN_QKV = 3 * GDN_HEADS * GDN_DK
HEADW = 512
OFF_GQKV = 0
OFF_AGATE = GDN_QKV
OFF_BQ = OFF_AGATE + HEADW
OFF_BK = OFF_BQ + HEADW
OFF_BV = OFF_BK + HEADW
OFF_BGATE = OFF_BV + HEADW
OFF_BD = OFF_BGATE + HEADW
W_COLS = OFF_BD + LANES

PROJ_TM = 512
PROJ_RB = 128
GMAT_W = 256
GDN_TC = 128
ATT_TQ = 512
ATT_UNROLL = 16
MAX_STATIC_SHIFT = 60.0
OUT_TM = 1024
VMEM_LIMIT = 48 * 1024 * 1024


def _silu(x):
    h = 0.5 * x
    return h + h * jnp.tanh(h)


def _dot(a, b, **kw):
    return jnp.dot(a, b, preferred_element_type=F32, **kw)


def _dot_nt(a, b, **kw):
    return lax.dot_general(a, b, (((1,), (1,)), ((), ())), preferred_element_type=F32, **kw)


def _dot_tn(a, b, **kw):
    return lax.dot_general(a, b, (((0,), (0,)), ((), ())), preferred_element_type=F32, **kw)


def _ada_kernel(c_ref, w_ref, b_ref, o_ref):
    o_ref[...] = _dot(_silu(c_ref[...]), w_ref[...], precision=HIGHEST) + b_ref[...]


def _ada(c_pad, w_ada_layers, layer, b_ada):
    rows, d = c_pad.shape
    n = w_ada_layers.shape[2]
    tn = 512
    return pl.pallas_call(
        _ada_kernel,
        grid=(n // tn,),
        in_specs=[pl.BlockSpec((rows, d), lambda j: (0, 0)),
                  pl.BlockSpec((None, d, tn), lambda j: (layer, 0, j)),
                  pl.BlockSpec((1, tn), lambda j: (0, j))],
        out_specs=pl.BlockSpec((rows, tn), lambda j: (0, j)),
        out_shape=jax.ShapeDtypeStruct((rows, n), F32),
        name="ada",
    )(c_pad, w_ada_layers, b_ada)


def _proj_kernel(x_ref, pos_ref, shift_ref, scale_ref, ng_ref, w_ref, convw_ref, alog_ref, dtb_ref,
                 qg_ref, kg_ref, lmat_ref, gmat_ref, freq_ref, sgn_ref, idx_ref,
                 gq_ref, gk_ref, gv_ref, ag_ref, bq_ref, bk_ref, bv_ref, bgate_ref, bg_ref, bgt_ref,
                 cq, ck, cv, z0, z1, h_s, gm_s, tab):
    tm = x_ref.shape[0]
    x = x_ref[...]
    ms = jnp.mean(x * x, axis=-1, keepdims=True)
    gain = ng_ref[...] * (1.0 + scale_ref[...])
    h_s[...] = (x * lax.rsqrt(ms + NORM_EPS) * gain + shift_ref[...]).astype(BF16)

    cbufs = (cq, ck, cv)
    zbuf = (z0, z1)

    @pl.when(pl.program_id(1) == 0)
    def _():
        for cb in cbufs:
            cb[0:8, :] = jnp.zeros((8, HEADW), F32)

    def mm_pieces(dst, row0, off, width):
        def piece(c0):
            c1 = min(c0 + GMAT_W, width)
            dst[row0:row0 + tm, c0:c1] = _dot(h_s[...], w_ref[:, off + c0:off + c1])
        return [functools.partial(piece, c0) for c0 in range(0, width, GMAT_W)]

    def conv_pieces(g, out_ref):
        cb = cbufs[g]

        def piece(hh):
            hs = slice(hh * LANES, (hh + 1) * LANES)
            cs = slice(g * HEADW + hh * LANES, g * HEADW + (hh + 1) * LANES)
            for r0 in range(0, tm, PROJ_RB):
                acc = convw_ref[3:4, cs] * cb[8 + r0:8 + r0 + PROJ_RB, hs]
                for j in range(CONV_K - 1):
                    acc = acc + convw_ref[j:j + 1, cs] * cb[5 + j + r0:5 + j + r0 + PROJ_RB, hs]
                a = _silu(acc)
                if g < 2:
                    a = a * lax.rsqrt(jnp.sum(a * a, axis=-1, keepdims=True) + NORM_EPS)
                    if g == 0:
                        a = a * (GDN_DK ** -0.5)
                out_ref[r0:r0 + PROJ_RB, hs] = a.astype(BF16)
            cb[0:8, hs] = cb[tm:tm + 8, hs]
        return [functools.partial(piece, hh) for hh in range(GDN_HEADS)]

    def gate_pieces(slot, out_ref):
        def piece(r0):
            rs = slice(r0, r0 + PROJ_RB)
            out_ref[rs, :] = _silu(zbuf[slot][rs, :]).astype(BF16)
        return [functools.partial(piece, r0) for r0 in range(0, tm, PROJ_RB)]

    def qk_pieces(slot, g_ref, out_ref, post):
        def piece(c0):
            z = zbuf[slot][:, c0:c0 + GMAT_W]
            gm_s[:, c0:c0 + GMAT_W] = _dot((z * z).astype(BF16), gmat_ref[...])
            for r0 in range(0, tm, PROJ_RB):
                rs = slice(r0, r0 + PROJ_RB)
                for h0 in range(c0, c0 + GMAT_W, LANES):
                    hs = slice(h0, h0 + LANES)
                    ys = zbuf[slot][rs, hs] * lax.rsqrt(gm_s[rs, hs] + NORM_EPS) * g_ref[:, hs]
                    r = (ys * tab[0, rs, :] + pltpu.roll(ys, LANES - ROT_DIM // 2, axis=1) * tab[1, rs, :]
                         + pltpu.roll(ys, ROT_DIM // 2, axis=1) * tab[2, rs, :])
                    if post is not None:
                        r = r * post
                    out_ref[rs, hs] = r.astype(BF16)
        return [functools.partial(piece, c0) for c0 in range(0, HEADW, GMAT_W)]

    def table_pieces():
        def piece(r0):
            rs = slice(r0, r0 + LANES)
            ang = freq_ref[...] * pos_ref[:, rs].astype(F32)
            pad = jnp.zeros((LANES - ang.shape[0], LANES), F32)
            idx = jnp.broadcast_to(idx_ref[...], (LANES, LANES))
            cos_t = jnp.take_along_axis(jnp.concatenate([jnp.cos(ang), pad], axis=0).T, idx, axis=1)
            sin_t = jnp.take_along_axis(jnp.concatenate([jnp.sin(ang), pad], axis=0).T, idx, axis=1)
            tab[0, rs, :] = cos_t
            tab[1, rs, :] = sin_t * sgn_ref[0:1, :]
            tab[2, rs, :] = sin_t * sgn_ref[1:2, :]
        return [functools.partial(piece, r0) for r0 in range(0, tm, LANES)]

    def ep_bd(slot):
        z = zbuf[slot][:, 0:LANES]
        lane = lax.broadcasted_iota(jnp.int32, (1, LANES), 1)
        beta = 1.0 / (1.0 + jnp.exp(-z))
        zz = z + dtb_ref[...]
        softplus = jnp.maximum(zz, 0.0) + jnp.log(1.0 + jnp.exp(-jnp.abs(zz)))
        is_g = (lane >= GDN_HEADS) & (lane < 2 * GDN_HEADS)
        g = jnp.where(is_g, -jnp.exp(alog_ref[...]) * softplus, 0.0)
        g_hi = g.astype(BF16)
        g_lo = (g - g_hi.astype(F32)).astype(BF16)
        gc = _dot(lmat_ref[...], g_hi) + _dot(lmat_ref[...], g_lo)
        bg = jnp.where(lane < GDN_HEADS, beta, gc)
        bg_ref[...] = bg
        bgt_ref[...] = bg.T[0:8, :]

    def cast_bv():
        bv_ref[...] = zbuf[0][...].astype(BF16)

    def emit(*lists):
        for k in range(max(len(pl_) for pl_ in lists)):
            for pl_ in lists:
                if k < len(pl_):
                    pl_[k]()

    q_scale = DIFF_D ** -0.5 * math.log2(math.e)
    stages = [
        (mm_pieces(cq, 8, OFF_GQKV, HEADW), conv_pieces(0, gq_ref)),
        (mm_pieces(z1, 0, OFF_BD, LANES) + mm_pieces(ck, 8, OFF_GQKV + HEADW, HEADW),
         conv_pieces(1, gk_ref) + [functools.partial(ep_bd, 1)]),
        (mm_pieces(cv, 8, OFF_GQKV + 2 * HEADW, HEADW), conv_pieces(2, gv_ref)),
        (mm_pieces(z0, 0, OFF_BQ, HEADW), qk_pieces(0, qg_ref, bq_ref, q_scale)),
        (mm_pieces(z1, 0, OFF_BK, HEADW), qk_pieces(1, kg_ref, bk_ref, None)),
        (mm_pieces(z0, 0, OFF_AGATE, HEADW), gate_pieces(0, ag_ref)),
        (mm_pieces(z1, 0, OFF_BGATE, HEADW), gate_pieces(1, bgate_ref)),
        (mm_pieces(z0, 0, OFF_BV, HEADW), [cast_bv]),
    ]
    emit(stages[0][0], table_pieces())
    for i, (_, epilogue) in enumerate(stages):
        emit(stages[i + 1][0] if i + 1 < len(stages) else [], epilogue)


def _proj(x, pos_row, shift, scale, norm_g, w_all, conv_w, alog_row, dtb_row, qg_row, kg_row,
          lmat, gmat, freq_rows, sgn_rows, idx_row):
    b, s, d = x.shape
    tm = PROJ_TM
    row = lambda bi, si: (bi, si, 0)
    const2 = lambda bi, si: (0, 0)
    per_b = lambda bi, si: (bi, 0, 0)
    hw = jax.ShapeDtypeStruct((b, s, HEADW), BF16)
    out_shape = [hw] * 8 + [jax.ShapeDtypeStruct((b, s, LANES), F32),
                            jax.ShapeDtypeStruct((b, 8, s), F32)]
    hw_spec = pl.BlockSpec((None, tm, HEADW), row)
    out_specs = [hw_spec] * 8 + [pl.BlockSpec((None, tm, LANES), row),
                                 pl.BlockSpec((None, 8, tm), lambda bi, si: (bi, 0, si))]
    in_specs = [
        pl.BlockSpec((None, tm, d), row),
        pl.BlockSpec((None, 1, tm), lambda bi, si: (bi, 0, si)),
        pl.BlockSpec((None, 1, d), per_b),
        pl.BlockSpec((None, 1, d), per_b),
        pl.BlockSpec((1, d), const2),
        pl.BlockSpec((d, W_COLS), const2),
        pl.BlockSpec((CONV_K, GDN_QKV), const2),
        pl.BlockSpec((1, LANES), const2),
        pl.BlockSpec((1, LANES), const2),
        pl.BlockSpec((1, HEADW), const2),
        pl.BlockSpec((1, HEADW), const2),
        pl.BlockSpec((tm, tm), const2),
        pl.BlockSpec((GMAT_W, GMAT_W), const2),
        pl.BlockSpec((2 * (ROT_DIM // 2), LANES), const2),
        pl.BlockSpec((2, LANES), const2),
        pl.BlockSpec((1, LANES), const2),
    ]
    return pl.pallas_call(
        _proj_kernel,
        grid=(b, s // tm),
        in_specs=in_specs,
        out_specs=out_specs,
        out_shape=out_shape,
        scratch_shapes=[pltpu.VMEM((tm + 8, HEADW), F32)] * 3 + [pltpu.VMEM((tm, HEADW), F32)] * 2
        + [pltpu.VMEM((tm, d), BF16), pltpu.VMEM((tm, HEADW), F32), pltpu.VMEM((3, tm, LANES), F32)],
        compiler_params=pltpu.CompilerParams(
            dimension_semantics=("arbitrary", "arbitrary"), vmem_limit_bytes=VMEM_LIMIT),
        name="proj",
    )(x, pos_row, shift, scale, norm_g, w_all, conv_w, alog_row, dtb_row, qg_row, kg_row,
      lmat, gmat, freq_rows, sgn_rows, idx_row)


def _split(a):
    hi = a.astype(BF16)
    return hi, (a - hi.astype(F32)).astype(BF16)


def _gdn_kernel(q_ref, k_ref, v_ref, gate_ref, bg_ref, bgt_ref, ng_ref, o_ref,
                state, wq_s, u_s, kva_s, vn_s):
    nb, tc = q_ref.shape[0], q_ref.shape[1]
    npair = tc // PAIR
    nc = tc // CHUNK

    @pl.when(pl.program_id(0) == 0)
    def _():
        state[...] = jnp.zeros(state.shape, F32)
        vn_s[...] = jnp.zeros(vn_s.shape, BF16)

    ii = lax.broadcasted_iota(jnp.int32, (CHUNK, PAIR), 0)
    lane_w = lax.broadcasted_iota(jnp.int32, (CHUNK, PAIR), 1)
    jj = lane_w % CHUNK
    left_w = lane_w < CHUNK
    left_p = lax.broadcasted_iota(jnp.int32, (PAIR, PAIR), 1) < CHUNK
    top_col = lax.broadcasted_iota(jnp.int32, (PAIR, 1), 0) < CHUNK
    eye = (ii == jj).astype(F32)
    lower = ii >= jj
    strict = ii > jj

    def widen(m):
        return jnp.where(left_w, m[0:CHUNK], m[CHUNK:PAIR])

    def bdiag(wd):
        z = jnp.zeros_like(wd)
        return jnp.concatenate([jnp.where(left_w, wd, z), jnp.where(left_w, z, wd)], axis=0)

    def mm(parts, rhs):
        n = len(parts)
        b_hi, b_lo = bdiag(rhs[0]), bdiag(rhs[1])
        his = [hi_ for hi_, _ in parts]
        r_hi = _dot(jnp.concatenate(his + [lo_ for _, lo_ in parts], axis=0), b_hi)
        r_lo = _dot(jnp.concatenate(his, axis=0) if n > 1 else his[0], b_lo)
        blk = lambda r, i: r[i * CHUNK:(i + 1) * CHUNK]
        return [blk(r_hi, i) + blk(r_hi, n + i) + blk(r_lo, i) for i in range(n)]

    items = [(bb, hh, p) for bb in range(nb) for hh in range(GDN_HEADS) for p in range(npair)]
    pre = []
    for bb, hh, p in items:
        cs = slice(hh * LANES, (hh + 1) * LANES)
        rs = slice(p * PAIR, (p + 1) * PAIR)
        q2, k2 = q_ref[bb, rs, cs], k_ref[bb, rs, cs]
        gcol2 = bg_ref[bb, rs, GDN_HEADS + hh:GDN_HEADS + hh + 1]
        bcol_w = widen(jnp.broadcast_to(bg_ref[bb, rs, hh:hh + 1], (PAIR, PAIR)))
        gcol_w = widen(jnp.broadcast_to(gcol2, (PAIR, PAIR)))
        grow = bgt_ref[bb, GDN_HEADS + hh:GDN_HEADS + hh + 1, rs]
        dec = jnp.where(lower, jnp.exp(jnp.minimum(gcol_w - grow, 0.0)), 0.0)
        gram = _dot_nt(jnp.concatenate([k2, q2], axis=0), k2)
        xm = jnp.where(strict, -(bcol_w * widen(gram[0:PAIR]) * dec), 0.0)
        a_w = jnp.where(lower, widen(gram[PAIR:2 * PAIR]) * dec, 0.0)
        pre.append((xm, a_w))

    def decayed_operands(item, a_w):
        bb, hh, p = item
        cs = slice(hh * LANES, (hh + 1) * LANES)
        rs = slice(p * PAIR, (p + 1) * PAIR)
        q2, k2 = q_ref[bb, rs, cs], k_ref[bb, rs, cs]
        gcol2 = bg_ref[bb, rs, GDN_HEADS + hh:GDN_HEADS + hh + 1]
        grow = bgt_ref[bb, GDN_HEADS + hh:GDN_HEADS + hh + 1, rs]
        g_end = jnp.where(top_col, grow[:, CHUNK - 1:CHUNK], grow[:, PAIR - 1:PAIR])
        e_g = jnp.exp(gcol2)
        k2f = k2.astype(F32)
        qd = (q2.astype(F32) * e_g).astype(BF16)
        kdt = (k2f * jnp.exp(g_end - gcol2)).T
        a_bd = bdiag(a_w)
        kva_s[bb, hh, 2 * p] = jnp.concatenate(
            [jnp.where(left_p, kdt, 0.0), a_bd[0:CHUNK]], axis=0).astype(BF16)
        kva_s[bb, hh, 2 * p + 1] = jnp.concatenate(
            [jnp.where(left_p, 0.0, kdt), a_bd[CHUNK:PAIR]], axis=0).astype(BF16)
        return qd, (k2f * e_g).astype(BF16)

    levels = 4
    per_level = -(-len(items) // levels)
    side = []
    ts = [eye + xm for xm, _ in pre]
    xs = [_split(xm) for xm, _ in pre]
    ps = [_split(mm([xp], xp)[0]) for xp in xs]
    for lvl in range(levels):
        both = [mm([_split(t), pp], pp) for t, pp in zip(ts, ps)]
        for idx in range(lvl * per_level, min((lvl + 1) * per_level, len(items))):
            side.append(decayed_operands(items[idx], pre[idx][1]))
        ts = [t + tp for t, (tp, _) in zip(ts, both)]
        ps = [_split(sq) for _, sq in both]
    ts = [t + mm([_split(t)], pp)[0] for t, pp in zip(ts, ps)]
    for (bb, hh, p), t, (qd, kg) in zip(items, ts, side):
        cs = slice(hh * LANES, (hh + 1) * LANES)
        rs = slice(p * PAIR, (p + 1) * PAIR)
        tb = bdiag(t * bgt_ref[bb, hh:hh + 1, rs]).astype(BF16)
        uw = _dot(tb, jnp.concatenate([v_ref[bb, rs, cs], kg], axis=1))
        u_s[bb, hh, rs, :] = uw[:, 0:GDN_DV]
        w2 = uw[:, GDN_DV:GDN_DV + GDN_DK].astype(BF16)
        for e in range(2):
            es = slice(e * CHUNK, (e + 1) * CHUNK)
            wq_s[bb, hh, 2 * p + e] = jnp.concatenate([w2[es], qd[es]], axis=0)

    chains = [(bb, hh) for bb in range(nb) for hh in range(GDN_HEADS)]
    for c in range(nc):
        e = c % 2
        rs = slice(c * CHUNK, (c + 1) * CHUNK)
        es = slice(e * CHUNK, (e + 1) * CHUNK)
        sts = [state[ch] for ch in chains]
        rr = [_dot(wq_s[bb, hh, c], st.astype(BF16)) for (bb, hh), st in zip(chains, sts)]
        for (bb, hh), r in zip(chains, rr):
            vn_s[bb, hh, es, :] = (u_s[bb, hh, rs, :] - r[0:CHUNK]).astype(BF16)
        for (bb, hh), st, r in zip(chains, sts, rr):
            cs = slice(hh * LANES, (hh + 1) * LANES)
            g_end = bgt_ref[bb, GDN_HEADS + hh:GDN_HEADS + hh + 1, (c + 1) * CHUNK - 1:(c + 1) * CHUNK]
            kva = _dot(kva_s[bb, hh, c], vn_s[bb, hh])
            state[bb, hh] = st * jnp.exp(g_end) + kva[0:GDN_DK]
            o = r[CHUNK:PAIR] + kva[GDN_DK:GDN_DK + CHUNK]
            on = o * lax.rsqrt(jnp.mean(o * o, axis=-1, keepdims=True) + NORM_EPS) * ng_ref[...]
            o_ref[bb, rs, cs] = (on * gate_ref[bb, rs, cs].astype(F32)).astype(BF16)


def _gdn(gq, gk, gv, agate, bg, bgt, ng_row):
    b, s, _ = gq.shape
    tc = GDN_TC
    row = lambda si: (0, si, 0)
    hw_spec = pl.BlockSpec((b, tc, HEADW), row)
    per_chain = (b, GDN_HEADS)
    return pl.pallas_call(
        _gdn_kernel,
        grid=(s // tc,),
        in_specs=[hw_spec, hw_spec, hw_spec, hw_spec,
                  pl.BlockSpec((b, tc, LANES), row),
                  pl.BlockSpec((b, 8, tc), lambda si: (0, 0, si)),
                  pl.BlockSpec((1, GDN_DV), lambda si: (0, 0))],
        out_specs=hw_spec,
        out_shape=jax.ShapeDtypeStruct((b, s, HEADW), BF16),
        scratch_shapes=[pltpu.VMEM(per_chain + (GDN_DK, GDN_DV), F32),
                        pltpu.VMEM(per_chain + (tc // CHUNK, PAIR, GDN_DK), BF16),
                        pltpu.VMEM(per_chain + (tc, GDN_DV), F32),
                        pltpu.VMEM(per_chain + (tc // CHUNK, GDN_DK + CHUNK, PAIR), BF16),
                        pltpu.VMEM(per_chain + (PAIR, GDN_DV), BF16)],
        compiler_params=pltpu.CompilerParams(
            dimension_semantics=("arbitrary",), vmem_limit_bytes=VMEM_LIMIT),
        name="gdn",
    )(gq, gk, gv, agate, bg, bgt, ng_row)


def _attn_kernel(q_ref, k_ref, v_ref, gate_ref, lamp_ref, sg_ref, gqk_ref, o_ref,
                 vext, acc, pbuf, kaug_s, *, lambda_init, tq):
    s_len = q_ref.shape[0]
    nq = s_len // tq
    tk = tq
    nch = tq // CHUNK
    w = 2 * DIFF_D

    vext[:, 0:w] = v_ref[...]
    vext[:, w:2 * w] = jnp.ones((s_len, w), BF16)

    @pl.when(jnp.logical_and(pl.program_id(0) == 0, pl.program_id(1) == 0))
    def _():
        acc[...] = jnp.zeros(acc.shape, F32)

    lane = lax.broadcasted_iota(jnp.int32, (tk, w), 1)
    rchunk = lax.broadcasted_iota(jnp.int32, (tk, w), 0) // CHUNK
    lo = lane < DIFF_D
    hi = jnp.logical_not(lo)
    half = (lo, hi)

    gqk = gqk_ref[...]
    bound = (jnp.max(jnp.abs(gqk[0:1, :])) * jnp.max(jnp.abs(gqk[1:2, :]))
             * (DIFF_D * DIFF_D ** -0.5 * math.log2(math.e) * 1.01))

    lp = lamp_ref[...]
    lam = (jnp.exp(jnp.sum(lp[0:1, :] * lp[1:2, :], axis=-1, keepdims=True))
           - jnp.exp(jnp.sum(lp[2:3, :] * lp[3:4, :], axis=-1, keepdims=True)) + lambda_init)

    def rows_of(blk):
        return pl.ds(pl.multiple_of(blk * tq, tq), tq)

    def write_rows(qi, a0, a1):
        o = a0[:, 0:w] / a0[:, w:2 * w] - lam * (a1[:, 0:w] / a1[:, w:2 * w])
        on = o * lax.rsqrt(jnp.mean(o * o, axis=-1, keepdims=True) + NORM_EPS) * sg_ref[...]
        on = on * (1.0 - lambda_init)
        o_ref[rows_of(qi), :] = (on * gate_ref[rows_of(qi), :].astype(F32)).astype(BF16)
        acc[qi] = jnp.zeros(acc.shape[1:], F32)


    @pl.when(bound <= MAX_STATIC_SHIFT)
    def _():
        q_aug = []
        for sub, base in enumerate((DIFF_D, 0)):
            mlane = lane - (base + 1)
            in_mask = (mlane >= 0) & (mlane < nch)
            q_aug.append(jnp.where((lane == base) | (in_mask & (rchunk == mlane)), 1.0, 0.0).astype(BF16))
            plain = jnp.where(lane == base, -bound, 0.0)
            kaug_s[0, sub] = plain.astype(BF16)
            kaug_s[1, sub] = jnp.where(in_mask & (rchunk > mlane), -MASK_BIG, plain).astype(BF16)

        def scores(qi, j, pslot):
            q = q_ref[rows_of(qi), :]
            kb = k_ref[rows_of(j), :]
            is_diag = jnp.asarray(j == qi, jnp.int32)
            for sub in range(2):
                qs = jnp.where(half[sub], q, q_aug[sub])
                ks = jnp.where(half[sub], kb, kaug_s[is_diag, sub])
                pbuf[pslot, sub] = jnp.exp2(_dot_nt(qs, ks)).astype(BF16)

        def accumulate(qi, j, pslot):
            ve = vext[rows_of(j), :]
            for sub in range(2):
                acc[qi, sub] += _dot(pbuf[pslot, sub], ve)

        def advance(st):
            qi, j, pqi, pj, n = st
            accumulate(pqi, pj, n & 1)
            scores(qi, j, (n + 1) & 1)
            wrap = j >= qi
            return (jnp.where(wrap, qi + 1, qi), jnp.where(wrap, 0, j + 1), qi, j, n + 1)

        todo = nq * (nq + 1) // 2 - 1
        unroll = max([u for u in range(ATT_UNROLL // 2, 2 * ATT_UNROLL) if todo % u == 0] or [ATT_UNROLL],
                     key=lambda u: -abs(u - ATT_UNROLL))

        def trip(t, st):
            for _ in range(unroll):
                st = advance(st)
            return st

        scores(0, 0, 0)
        one, zero = jnp.int32(1), jnp.int32(0)
        trips, rest = divmod(todo, unroll)
        st = lax.fori_loop(0, trips, trip, (one, zero, zero, zero, zero))
        for _ in range(rest):
            st = advance(st)
        accumulate(st[2], st[3], st[4] & 1)

        group = 4 if nq % 4 == 0 else 1

        def write_blocks(g, carry):
            for u in range(group):
                qi = g * group + u
                write_rows(qi, acc[qi, 0], acc[qi, 1])
            return carry

        lax.fori_loop(0, nq // group, write_blocks, 0)

    @pl.when(bound > MAX_STATIC_SHIFT)
    def _():
        zero = jnp.zeros((tk, w), BF16)
        rr = lax.broadcasted_iota(jnp.int32, (tq, tk), 0) // CHUNK
        cc = lax.broadcasted_iota(jnp.int32, (tq, tk), 1) // CHUNK
        causal = cc <= rr

        def outer(qi, carry):
            q = q_ref[rows_of(qi), :]

            def step(j, ms, diag):
                kb = k_ref[rows_of(j), :]
                ve = vext[rows_of(j), :]
                new = []
                for sub in range(2):
                    s = _dot_nt(q, jnp.where(half[sub], kb, zero))
                    if diag:
                        s = jnp.where(causal, s, NEG_BIG)
                    m_new = jnp.maximum(ms[sub], jnp.max(s, axis=-1, keepdims=True))
                    p = jnp.exp2(s - m_new)
                    acc[qi, sub] = jnp.exp2(ms[sub] - m_new) * acc[qi, sub] + _dot(p.astype(BF16), ve)
                    new.append(m_new)
                return tuple(new)

            init = (jnp.full((tq, 1), NEG_BIG, F32), jnp.full((tq, 1), NEG_BIG, F32))
            ms = lax.fori_loop(0, qi, lambda j, c: step(j, c, False), init)
            step(qi, ms, True)
            write_rows(qi, acc[qi, 0], acc[qi, 1])
            return carry

        lax.fori_loop(0, nq, outer, 0)


def _attn(bq, bk, bv, bgate, lam_params, sg_row, gqk_rows, lambda_init):
    b, s, _ = bq.shape
    tq = ATT_TQ
    w = 2 * DIFF_D
    hspec = pl.BlockSpec((None, s, w), lambda bi, hi: (bi, 0, hi))
    const = lambda bi, hi: (0, 0)
    return pl.pallas_call(
        functools.partial(_attn_kernel, lambda_init=lambda_init, tq=tq),
        grid=(b, DIFF_HEADS),
        in_specs=[hspec, hspec, hspec, hspec,
                  pl.BlockSpec((8, LANES), const),
                  pl.BlockSpec((1, w), const),
                  pl.BlockSpec((2, LANES), const)],
        out_specs=hspec,
        out_shape=jax.ShapeDtypeStruct((b, s, HEADW), BF16),
        scratch_shapes=[pltpu.VMEM((s, 2 * w), BF16),
                        pltpu.VMEM((s // tq, 2, tq, 2 * w), F32),
                        pltpu.VMEM((2, 2, tq, tq), BF16),
                        pltpu.VMEM((2, 2, tq, w), BF16)],
        compiler_params=pltpu.CompilerParams(
            dimension_semantics=("arbitrary", "arbitrary"), vmem_limit_bytes=VMEM_LIMIT),
        name="attn",
    )(bq, bk, bv, bgate, lam_params, sg_row, gqk_rows)


def _out_kernel(x_ref, ma_ref, mb_ref, wa_ref, wb_ref, gate_ref, o_ref):
    y = _dot(ma_ref[...], wa_ref[...]) + _dot(mb_ref[...], wb_ref[...])
    o_ref[...] = x_ref[...] + gate_ref[...] * y


def _out(x, mixed_a, mixed_b, w_a, w_b, gate):
    b, s, d = x.shape
    tm = OUT_TM
    row = lambda bi, si: (bi, si, 0)
    const2 = lambda bi, si: (0, 0)
    return pl.pallas_call(
        _out_kernel,
        grid=(b, s // tm),
        in_specs=[pl.BlockSpec((None, tm, d), row),
                  pl.BlockSpec((None, tm, HEADW), row),
                  pl.BlockSpec((None, tm, HEADW), row),
                  pl.BlockSpec((HEADW, d), const2),
                  pl.BlockSpec((HEADW, d), const2),
                  pl.BlockSpec((None, 1, d), lambda bi, si: (bi, 0, 0))],
        out_specs=pl.BlockSpec((None, tm, d), row),
        out_shape=jax.ShapeDtypeStruct((b, s, d), F32),
        compiler_params=pltpu.CompilerParams(
            dimension_semantics=("arbitrary", "arbitrary"), vmem_limit_bytes=VMEM_LIMIT),
        name="out",
    )(x, mixed_a, mixed_b, w_a, w_b, gate)


def _constants():
    sub = np.arange(LANES) % DIFF_D
    half = ROT_DIM // 2
    inv_freq = ROPE_THETA ** (-np.arange(0, ROT_DIM, 2, dtype=np.float32) / ROT_DIM)
    freq_rows = np.zeros((2 * half, LANES), np.float32)
    freq_rows[:half] = inv_freq[:, None]
    idx_row = np.where(sub < ROT_DIM, sub % half, half).astype(np.int32)[None, :]
    sgn_rows = np.stack([np.where(sub < half, -1.0, 0.0),
                         np.where((sub >= half) & (sub < ROT_DIM), 1.0, 0.0)]).astype(np.float32)
    r = np.arange(PROJ_TM)
    lmat = (r[:, None] // CHUNK == r[None, :] // CHUNK) & (r[None, :] <= r[:, None])
    g = np.arange(GMAT_W)
    gmat = np.where(g[:, None] // DIFF_D == g[None, :] // DIFF_D, 1.0 / DIFF_D, 0.0)
    return (jnp.asarray(freq_rows), jnp.asarray(sgn_rows), jnp.asarray(idx_row),
            jnp.asarray(lmat, BF16), jnp.asarray(gmat, BF16))


def _pad_lanes(v, offset):
    return jnp.zeros((1, LANES), F32).at[0, offset:offset + v.shape[0]].set(v.astype(F32))


def _layer(x, c_pad, pos_row, lambda_init, norm_g, w_ada_layers, layer, b_ada, w_in, conv_w, a_log, dt_bias, gdn_norm_g,
           q_norm_g, k_norm_g, lam_q1, lam_k1, lam_q2, lam_k2, subln_g, w_out):
    b, s, d = x.shape
    mod = _ada(c_pad, w_ada_layers, layer, b_ada[None, :])[:b]
    shift = mod[:, None, 0:d]
    scale = mod[:, None, d:2 * d]
    gate = mod[:, None, 2 * d:3 * d]

    nk = GDN_HEADS * GDN_DK
    o_beta = 2 * nk + GDN_HEADS * GDN_DV
    o_agate = o_beta + 2 * GDN_HEADS
    o_b = o_agate + GDN_HEADS * GDN_DV
    bd = jnp.pad(w_in[:, o_beta:o_agate], ((0, 0), (0, LANES - 2 * GDN_HEADS)))
    w_all = jnp.concatenate([
        w_in[:, 0:o_beta], w_in[:, o_agate:o_b],
        w_in[:, o_b:o_b + HEADW], w_in[:, o_b + HEADW:o_b + 2 * HEADW],
        w_in[:, o_b + 2 * HEADW:o_b + 3 * HEADW], w_in[:, o_b + 3 * HEADW:o_b + 4 * HEADW],
        bd], axis=1).astype(BF16)
    freq_rows, sgn_rows, idx_row, lmat, gmat = _constants()
    lam_params = jnp.zeros((8, LANES), F32)
    for r, v in enumerate((lam_q1, lam_k1, lam_q2, lam_k2)):
        lam_params = lam_params.at[r, 0:DIFF_D].set(v.astype(F32))

    gq, gk, gv, agate, bq, bk, bv, bgate, bg, bgt = _proj(
        x, pos_row, shift, scale, norm_g[None, :], w_all, conv_w,
        _pad_lanes(a_log, GDN_HEADS), _pad_lanes(dt_bias, GDN_HEADS),
        jnp.tile(q_norm_g, 2 * DIFF_HEADS)[None, :], jnp.tile(k_norm_g, 2 * DIFF_HEADS)[None, :],
        lmat, gmat, freq_rows, sgn_rows, idx_row)
    mixed_a = _gdn(gq, gk, gv, agate, bg, bgt, gdn_norm_g[None, :])
    gqk_rows = jnp.concatenate([_pad_lanes(q_norm_g, 0), _pad_lanes(k_norm_g, 0)], axis=0)
    mixed_b = _attn(bq, bk, bv, bgate, lam_params, subln_g[None, :], gqk_rows, lambda_init)
    w_o = w_out.astype(BF16)
    return _out(x, mixed_a, mixed_b, w_o[:HEADW], w_o[HEADW:], gate)


def kernel(x, c, positions, norm_g, w_ada, b_ada, w_in, conv_w, a_log, dt_bias, gdn_norm_g, q_norm_g,
           k_norm_g, lambda_q1, lambda_k1, lambda_q2, lambda_k2, subln_g, w_out):
    b, s, d = x.shape
    for tile in (PROJ_TM, GDN_TC, ATT_TQ, OUT_TM):
        assert s % tile == 0, (s, tile)
    assert d == 2 * HEADW and w_in.shape[-1] == W_COLS - LANES + 2 * GDN_HEADS
    c_pad = jnp.pad(c, ((0, 8 - b % 8 if b % 8 else 0), (0, 0)))
    pos_row = positions[:, None, :]
    for l in range(norm_g.shape[0]):
        lambda_init = 0.8 - 0.6 * math.exp(-0.3 * l)
        x = _layer(x, c_pad, pos_row, lambda_init, norm_g[l], w_ada, l, b_ada[l], w_in[l], conv_w[l],
                   a_log[l], dt_bias[l], gdn_norm_g[l], q_norm_g[l], k_norm_g[l], lambda_q1[l],
                   lambda_k1[l], lambda_q2[l], lambda_k2[l], subln_g[l], w_out[l])
    return x
```

```python
import functools
import math

import jax
import jax.numpy as jnp
import numpy as np
from jax import lax
from jax.experimental import pallas as pl
from jax.experimental.pallas import tpu as pltpu

F32 = jnp.float32
BF16 = jnp.bfloat16
HIGHEST = lax.Precision.HIGHEST

CHUNK = 64
PAIR = 2 * CHUNK
GDN_HEADS = 4
GDN_DK = 128
GDN_DV = 128
CONV_K = 4
DIFF_HEADS = 4
DIFF_D = 64
ROT_DIM = 16
ROPE_THETA = 500000.0
NORM_EPS = 1e-6
LANES = 128
NEG_BIG = -1e30
MASK_BIG = 256.0

GDN_QKV = 3 * GDN_HEADS * GDN_DK
HEADW = 512
OFF_GQKV = 0
OFF_AGATE = GDN_QKV
OFF_BQ = OFF_AGATE + HEADW
OFF_BK = OFF_BQ + HEADW
OFF_BV = OFF_BK + HEADW
OFF_BGATE = OFF_BV + HEADW
OFF_BD = OFF_BGATE + HEADW
W_COLS = OFF_BD + LANES

PROJ_TM = 512
PROJ_RB = 128
GMAT_W = 256
GDN_TC = 256
ATT_TQ = 512
ATT_UNROLL = 16
MAX_STATIC_SHIFT = 60.0
OUT_TM = 1024
VMEM_LIMIT = 48 * 1024 * 1024


def _silu(x):
    h = 0.5 * x
    return h + h * jnp.tanh(h)


def _dot(a, b, **kw):
    return jnp.dot(a, b, preferred_element_type=F32, **kw)


def _dot_nt(a, b, **kw):
    return lax.dot_general(a, b, (((1,), (1,)), ((), ())), preferred_element_type=F32, **kw)


def _dot_tn(a, b, **kw):
    return lax.dot_general(a, b, (((0,), (0,)), ((), ())), preferred_element_type=F32, **kw)


def _ada_kernel(c_ref, w_ref, b_ref, o_ref):
    o_ref[...] = _dot(_silu(c_ref[...]), w_ref[...], precision=HIGHEST) + b_ref[...]


def _ada(c_pad, w_ada_layers, layer, b_ada):
    rows, d = c_pad.shape
    n = w_ada_layers.shape[2]
    tn = 512
    return pl.pallas_call(
        _ada_kernel,
        grid=(n // tn,),
        in_specs=[pl.BlockSpec((rows, d), lambda j: (0, 0)),
                  pl.BlockSpec((None, d, tn), lambda j: (layer, 0, j)),
                  pl.BlockSpec((1, tn), lambda j: (0, j))],
        out_specs=pl.BlockSpec((rows, tn), lambda j: (0, j)),
        out_shape=jax.ShapeDtypeStruct((rows, n), F32),
        name="ada",
    )(c_pad, w_ada_layers, b_ada)


def _proj_kernel(x_ref, pos_ref, shift_ref, scale_ref, ng_ref, w_ref, convw_ref, alog_ref, dtb_ref,
                 qg_ref, kg_ref, lmat_ref, gmat_ref, freq_ref, sgn_ref, idx_ref,
                 gq_ref, gk_ref, gv_ref, ag_ref, bq_ref, bk_ref, bv_ref, bgate_ref, bg_ref, bgt_ref,
                 cq, ck, cv, z0, z1, h_s, gm_s, tab):
    tm = x_ref.shape[0]
    x = x_ref[...]
    ms = jnp.mean(x * x, axis=-1, keepdims=True)
    gain = ng_ref[...] * (1.0 + scale_ref[...])
    h_s[...] = (x * lax.rsqrt(ms + NORM_EPS) * gain + shift_ref[...]).astype(BF16)

    cbufs = (cq, ck, cv)
    zbuf = (z0, z1)

    @pl.when(pl.program_id(1) == 0)
    def _():
        for cb in cbufs:
            cb[0:8, :] = jnp.zeros((8, HEADW), F32)

    def mm_pieces(dst, row0, off, width):
        def piece(c0):
            c1 = min(c0 + GMAT_W, width)
            dst[row0:row0 + tm, c0:c1] = _dot(h_s[...], w_ref[:, off + c0:off + c1])
        return [functools.partial(piece, c0) for c0 in range(0, width, GMAT_W)]

    def conv_pieces(g, out_ref):
        cb = cbufs[g]

        def piece(hh):
            hs = slice(hh * LANES, (hh + 1) * LANES)
            cs = slice(g * HEADW + hh * LANES, g * HEADW + (hh + 1) * LANES)
            for r0 in range(0, tm, PROJ_RB):
                acc = convw_ref[3:4, cs] * cb[8 + r0:8 + r0 + PROJ_RB, hs]
                for j in range(CONV_K - 1):
                    acc = acc + convw_ref[j:j + 1, cs] * cb[5 + j + r0:5 + j + r0 + PROJ_RB, hs]
                a = _silu(acc)
                if g < 2:
                    a = a * lax.rsqrt(jnp.sum(a * a, axis=-1, keepdims=True) + NORM_EPS)
                    if g == 0:
                        a = a * (GDN_DK ** -0.5)
                out_ref[r0:r0 + PROJ_RB, hs] = a.astype(BF16)
            cb[0:8, hs] = cb[tm:tm + 8, hs]
        return [functools.partial(piece, hh) for hh in range(GDN_HEADS)]

    def gate_pieces(slot, out_ref):
        def piece(r0):
            rs = slice(r0, r0 + PROJ_RB)
            out_ref[rs, :] = _silu(zbuf[slot][rs, :]).astype(BF16)
        return [functools.partial(piece, r0) for r0 in range(0, tm, PROJ_RB)]

    def qk_pieces(slot, g_ref, out_ref, post):
        def piece(c0):
            z = zbuf[slot][:, c0:c0 + GMAT_W]
            gm_s[:, c0:c0 + GMAT_W] = _dot((z * z).astype(BF16), gmat_ref[...])
            for r0 in range(0, tm, PROJ_RB):
                rs = slice(r0, r0 + PROJ_RB)
                for h0 in range(c0, c0 + GMAT_W, LANES):
                    hs = slice(h0, h0 + LANES)
                    ys = zbuf[slot][rs, hs] * lax.rsqrt(gm_s[rs, hs] + NORM_EPS) * g_ref[:, hs]
                    r = (ys * tab[0, rs, :] + pltpu.roll(ys, LANES - ROT_DIM // 2, axis=1) * tab[1, rs, :]
                         + pltpu.roll(ys, ROT_DIM // 2, axis=1) * tab[2, rs, :])
                    if post is not None:
                        r = r * post
                    out_ref[rs, hs] = r.astype(BF16)
        return [functools.partial(piece, c0) for c0 in range(0, HEADW, GMAT_W)]

    def table_pieces():
        def piece(r0):
            rs = slice(r0, r0 + LANES)
            ang = freq_ref[...] * pos_ref[:, rs].astype(F32)
            pad = jnp.zeros((LANES - ang.shape[0], LANES), F32)
            idx = jnp.broadcast_to(idx_ref[...], (LANES, LANES))
            cos_t = jnp.take_along_axis(jnp.concatenate([jnp.cos(ang), pad], axis=0).T, idx, axis=1)
            sin_t = jnp.take_along_axis(jnp.concatenate([jnp.sin(ang), pad], axis=0).T, idx, axis=1)
            tab[0, rs, :] = cos_t
            tab[1, rs, :] = sin_t * sgn_ref[0:1, :]
            tab[2, rs, :] = sin_t * sgn_ref[1:2, :]
        return [functools.partial(piece, r0) for r0 in range(0, tm, LANES)]

    def ep_bd(slot):
        z = zbuf[slot][:, 0:LANES]
        lane = lax.broadcasted_iota(jnp.int32, (1, LANES), 1)
        beta = 1.0 / (1.0 + jnp.exp(-z))
        zz = z + dtb_ref[...]
        softplus = jnp.maximum(zz, 0.0) + jnp.log(1.0 + jnp.exp(-jnp.abs(zz)))
        is_g = (lane >= GDN_HEADS) & (lane < 2 * GDN_HEADS)
        g = jnp.where(is_g, -jnp.exp(alog_ref[...]) * softplus, 0.0)
        g_hi = g.astype(BF16)
        g_lo = (g - g_hi.astype(F32)).astype(BF16)
        gc = _dot(lmat_ref[...], g_hi) + _dot(lmat_ref[...], g_lo)
        bg = jnp.where(lane < GDN_HEADS, beta, gc)
        bg_ref[...] = bg
        bgt_ref[...] = bg.T[0:8, :]

    def cast_bv():
        bv_ref[...] = zbuf[0][...].astype(BF16)

    def emit(*lists):
        for k in range(max(len(pl_) for pl_ in lists)):
            for pl_ in lists:
                if k < len(pl_):
                    pl_[k]()

    q_scale = DIFF_D ** -0.5 * math.log2(math.e)
    stages = [
        (mm_pieces(cq, 8, OFF_GQKV, HEADW), conv_pieces(0, gq_ref)),
        (mm_pieces(z1, 0, OFF_BD, LANES) + mm_pieces(ck, 8, OFF_GQKV + HEADW, HEADW),
         conv_pieces(1, gk_ref) + [functools.partial(ep_bd, 1)]),
        (mm_pieces(cv, 8, OFF_GQKV + 2 * HEADW, HEADW), conv_pieces(2, gv_ref)),
        (mm_pieces(z0, 0, OFF_BQ, HEADW), qk_pieces(0, qg_ref, bq_ref, q_scale)),
        (mm_pieces(z1, 0, OFF_BK, HEADW), qk_pieces(1, kg_ref, bk_ref, None)),
        (mm_pieces(z0, 0, OFF_AGATE, HEADW), gate_pieces(0, ag_ref)),
        (mm_pieces(z1, 0, OFF_BGATE, HEADW), gate_pieces(1, bgate_ref)),
        (mm_pieces(z0, 0, OFF_BV, HEADW), [cast_bv]),
    ]
    emit(stages[0][0], table_pieces())
    for i, (_, epilogue) in enumerate(stages):
        emit(stages[i + 1][0] if i + 1 < len(stages) else [], epilogue)


def _proj(x, pos_row, shift, scale, norm_g, w_all, conv_w, alog_row, dtb_row, qg_row, kg_row,
          lmat, gmat, freq_rows, sgn_rows, idx_row):
    b, s, d = x.shape
    tm = PROJ_TM
    row = lambda bi, si: (bi, si, 0)
    const2 = lambda bi, si: (0, 0)
    per_b = lambda bi, si: (bi, 0, 0)
    hw = jax.ShapeDtypeStruct((b, s, HEADW), BF16)
    out_shape = [hw] * 8 + [jax.ShapeDtypeStruct((b, s, LANES), F32),
                            jax.ShapeDtypeStruct((b, 8, s), F32)]
    hw_spec = pl.BlockSpec((None, tm, HEADW), row)
    out_specs = [hw_spec] * 8 + [pl.BlockSpec((None, tm, LANES), row),
                                 pl.BlockSpec((None, 8, tm), lambda bi, si: (bi, 0, si))]
    in_specs = [
        pl.BlockSpec((None, tm, d), row),
        pl.BlockSpec((None, 1, tm), lambda bi, si: (bi, 0, si)),
        pl.BlockSpec((None, 1, d), per_b),
        pl.BlockSpec((None, 1, d), per_b),
        pl.BlockSpec((1, d), const2),
        pl.BlockSpec((d, W_COLS), const2),
        pl.BlockSpec((CONV_K, GDN_QKV), const2),
        pl.BlockSpec((1, LANES), const2),
        pl.BlockSpec((1, LANES), const2),
        pl.BlockSpec((1, HEADW), const2),
        pl.BlockSpec((1, HEADW), const2),
        pl.BlockSpec((tm, tm), const2),
        pl.BlockSpec((GMAT_W, GMAT_W), const2),
        pl.BlockSpec((2 * (ROT_DIM // 2), LANES), const2),
        pl.BlockSpec((2, LANES), const2),
        pl.BlockSpec((1, LANES), const2),
    ]
    return pl.pallas_call(
        _proj_kernel,
        grid=(b, s // tm),
        in_specs=in_specs,
        out_specs=out_specs,
        out_shape=out_shape,
        scratch_shapes=[pltpu.VMEM((tm + 8, HEADW), F32)] * 3 + [pltpu.VMEM((tm, HEADW), F32)] * 2
        + [pltpu.VMEM((tm, d), BF16), pltpu.VMEM((tm, HEADW), F32), pltpu.VMEM((3, tm, LANES), F32)],
        compiler_params=pltpu.CompilerParams(
            dimension_semantics=("arbitrary", "arbitrary"), vmem_limit_bytes=VMEM_LIMIT),
        name="proj",
    )(x, pos_row, shift, scale, norm_g, w_all, conv_w, alog_row, dtb_row, qg_row, kg_row,
      lmat, gmat, freq_rows, sgn_rows, idx_row)


def _split(a):
    hi = a.astype(BF16)
    return hi, (a - hi.astype(F32)).astype(BF16)


def _gdn_kernel(q_ref, k_ref, v_ref, gate_ref, bg_ref, bgt_ref, ng_ref, o_ref,
                state, wq_s, u_s, kva_s, vn_s):
    nb, tc = q_ref.shape[0], q_ref.shape[1]
    npair = tc // PAIR
    nc = tc // CHUNK

    @pl.when(pl.program_id(0) == 0)
    def _():
        state[...] = jnp.zeros(state.shape, F32)
        vn_s[...] = jnp.zeros(vn_s.shape, BF16)

    ii = lax.broadcasted_iota(jnp.int32, (CHUNK, PAIR), 0)
    lane_w = lax.broadcasted_iota(jnp.int32, (CHUNK, PAIR), 1)
    jj = lane_w % CHUNK
    left_w = lane_w < CHUNK
    left_p = lax.broadcasted_iota(jnp.int32, (PAIR, PAIR), 1) < CHUNK
    top_col = lax.broadcasted_iota(jnp.int32, (PAIR, 1), 0) < CHUNK
    eye = (ii == jj).astype(F32)
    lower = ii >= jj
    strict = ii > jj

    def widen(m):
        return jnp.where(left_w, m[0:CHUNK], m[CHUNK:PAIR])

    def bdiag(wd):
        z = jnp.zeros_like(wd)
        return jnp.concatenate([jnp.where(left_w, wd, z), jnp.where(left_w, z, wd)], axis=0)

    def mm(parts, rhs):
        n = len(parts)
        b_hi, b_lo = bdiag(rhs[0]), bdiag(rhs[1])
        his = [hi_ for hi_, _ in parts]
        r_hi = _dot(jnp.concatenate(his + [lo_ for _, lo_ in parts], axis=0), b_hi)
        r_lo = _dot(jnp.concatenate(his, axis=0) if n > 1 else his[0], b_lo)
        blk = lambda r, i: r[i * CHUNK:(i + 1) * CHUNK]
        return [blk(r_hi, i) + blk(r_hi, n + i) + blk(r_lo, i) for i in range(n)]

    items = [(bb, hh, p) for bb in range(nb) for hh in range(GDN_HEADS) for p in range(npair)]
    pre = []
    for bb, hh, p in items:
        cs = slice(hh * LANES, (hh + 1) * LANES)
        rs = slice(p * PAIR, (p + 1) * PAIR)
        q2, k2 = q_ref[bb, rs, cs], k_ref[bb, rs, cs]
        gcol2 = bg_ref[bb, rs, GDN_HEADS + hh:GDN_HEADS + hh + 1]
        bcol_w = widen(jnp.broadcast_to(bg_ref[bb, rs, hh:hh + 1], (PAIR, PAIR)))
        gcol_w = widen(jnp.broadcast_to(gcol2, (PAIR, PAIR)))
        grow = bgt_ref[bb, GDN_HEADS + hh:GDN_HEADS + hh + 1, rs]
        dec = jnp.where(lower, jnp.exp(jnp.minimum(gcol_w - grow, 0.0)), 0.0)
        gram = _dot_nt(jnp.concatenate([k2, q2], axis=0), k2)
        xm = jnp.where(strict, -(bcol_w * widen(gram[0:PAIR]) * dec), 0.0)
        a_w = jnp.where(lower, widen(gram[PAIR:2 * PAIR]) * dec, 0.0)
        pre.append((xm, a_w))

    def decayed_operands(item, a_w):
        bb, hh, p = item
        cs = slice(hh * LANES, (hh + 1) * LANES)
        rs = slice(p * PAIR, (p + 1) * PAIR)
        q2, k2 = q_ref[bb, rs, cs], k_ref[bb, rs, cs]
        gcol2 = bg_ref[bb, rs, GDN_HEADS + hh:GDN_HEADS + hh + 1]
        grow = bgt_ref[bb, GDN_HEADS + hh:GDN_HEADS + hh + 1, rs]
        g_end = jnp.where(top_col, grow[:, CHUNK - 1:CHUNK], grow[:, PAIR - 1:PAIR])
        e_g = jnp.exp(gcol2)
        k2f = k2.astype(F32)
        qd = (q2.astype(F32) * e_g).astype(BF16)
        kdt = (k2f * jnp.exp(g_end - gcol2)).T
        a_bd = bdiag(a_w)
        kva_s[bb, hh, 2 * p] = jnp.concatenate(
            [jnp.where(left_p, kdt, 0.0), a_bd[0:CHUNK]], axis=0).astype(BF16)
        kva_s[bb, hh, 2 * p + 1] = jnp.concatenate(
            [jnp.where(left_p, 0.0, kdt), a_bd[CHUNK:PAIR]], axis=0).astype(BF16)
        return qd, (k2f * e_g).astype(BF16)

    levels = 4
    per_level = -(-len(items) // levels)
    side = []
    ts = [eye + xm for xm, _ in pre]
    xs = [_split(xm) for xm, _ in pre]
    ps = [_split(mm([xp], xp)[0]) for xp in xs]
    for lvl in range(levels):
        both = [mm([_split(t), pp], pp) for t, pp in zip(ts, ps)]
        for idx in range(lvl * per_level, min((lvl + 1) * per_level, len(items))):
            side.append(decayed_operands(items[idx], pre[idx][1]))
        ts = [t + tp for t, (tp, _) in zip(ts, both)]
        ps = [_split(sq) for _, sq in both]
    ts = [t + mm([_split(t)], pp)[0] for t, pp in zip(ts, ps)]
    for (bb, hh, p), t, (qd, kg) in zip(items, ts, side):
        cs = slice(hh * LANES, (hh + 1) * LANES)
        rs = slice(p * PAIR, (p + 1) * PAIR)
        tb = bdiag(t * bgt_ref[bb, hh:hh + 1, rs]).astype(BF16)
        uw = _dot(tb, jnp.concatenate([v_ref[bb, rs, cs], kg], axis=1))
        u_s[bb, hh, rs, :] = uw[:, 0:GDN_DV]
        w2 = uw[:, GDN_DV:GDN_DV + GDN_DK].astype(BF16)
        for e in range(2):
            es = slice(e * CHUNK, (e + 1) * CHUNK)
            wq_s[bb, hh, 2 * p + e] = jnp.concatenate([w2[es], qd[es]], axis=0)

    chains = [(bb, hh) for bb in range(nb) for hh in range(GDN_HEADS)]
    for c in range(nc):
        e = c % 2
        rs = slice(c * CHUNK, (c + 1) * CHUNK)
        es = slice(e * CHUNK, (e + 1) * CHUNK)
        sts = [state[ch] for ch in chains]
        rr = [_dot(wq_s[bb, hh, c], st.astype(BF16)) for (bb, hh), st in zip(chains, sts)]
        for (bb, hh), r in zip(chains, rr):
            vn_s[bb, hh, es, :] = (u_s[bb, hh, rs, :] - r[0:CHUNK]).astype(BF16)
        for (bb, hh), st, r in zip(chains, sts, rr):
            cs = slice(hh * LANES, (hh + 1) * LANES)
            g_end = bgt_ref[bb, GDN_HEADS + hh:GDN_HEADS + hh + 1, (c + 1) * CHUNK - 1:(c + 1) * CHUNK]
            kva = _dot(kva_s[bb, hh, c], vn_s[bb, hh])
            state[bb, hh] = st * jnp.exp(g_end) + kva[0:GDN_DK]
            o = r[CHUNK:PAIR] + kva[GDN_DK:GDN_DK + CHUNK]
            on = o * lax.rsqrt(jnp.mean(o * o, axis=-1, keepdims=True) + NORM_EPS) * ng_ref[...]
            o_ref[bb, rs, cs] = (on * gate_ref[bb, rs, cs].astype(F32)).astype(BF16)


def _gdn(gq, gk, gv, agate, bg, bgt, ng_row):
    b, s, _ = gq.shape
    tc = GDN_TC
    row = lambda si: (0, si, 0)
    hw_spec = pl.BlockSpec((b, tc, HEADW), row)
    per_chain = (b, GDN_HEADS)
    return pl.pallas_call(
        _gdn_kernel,
        grid=(s // tc,),
        in_specs=[hw_spec, hw_spec, hw_spec, hw_spec,
                  pl.BlockSpec((b, tc, LANES), row),
                  pl.BlockSpec((b, 8, tc), lambda si: (0, 0, si)),
                  pl.BlockSpec((1, GDN_DV), lambda si: (0, 0))],
        out_specs=hw_spec,
        out_shape=jax.ShapeDtypeStruct((b, s, HEADW), BF16),
        scratch_shapes=[pltpu.VMEM(per_chain + (GDN_DK, GDN_DV), F32),
                        pltpu.VMEM(per_chain + (tc // CHUNK, PAIR, GDN_DK), BF16),
                        pltpu.VMEM(per_chain + (tc, GDN_DV), F32),
                        pltpu.VMEM(per_chain + (tc // CHUNK, GDN_DK + CHUNK, PAIR), BF16),
                        pltpu.VMEM(per_chain + (PAIR, GDN_DV), BF16)],
        compiler_params=pltpu.CompilerParams(
            dimension_semantics=("arbitrary",), vmem_limit_bytes=VMEM_LIMIT),
        name="gdn",
    )(gq, gk, gv, agate, bg, bgt, ng_row)


def _attn_kernel(q_ref, k_ref, v_ref, gate_ref, lamp_ref, sg_ref, gqk_ref, o_ref,
                 vext, acc, pbuf, kaug_s, *, lambda_init, tq):
    s_len = q_ref.shape[0]
    nq = s_len // tq
    tk = tq
    nch = tq // CHUNK
    w = 2 * DIFF_D

    vext[:, 0:w] = v_ref[...]
    vext[:, w:2 * w] = jnp.ones((s_len, w), BF16)

    @pl.when(jnp.logical_and(pl.program_id(0) == 0, pl.program_id(1) == 0))
    def _():
        acc[...] = jnp.zeros(acc.shape, F32)

    lane = lax.broadcasted_iota(jnp.int32, (tk, w), 1)
    rchunk = lax.broadcasted_iota(jnp.int32, (tk, w), 0) // CHUNK
    lo = lane < DIFF_D
    hi = jnp.logical_not(lo)
    half = (lo, hi)

    gqk = gqk_ref[...]
    bound = (jnp.max(jnp.abs(gqk[0:1, :])) * jnp.max(jnp.abs(gqk[1:2, :]))
             * (DIFF_D * DIFF_D ** -0.5 * math.log2(math.e) * 1.01))

    lp = lamp_ref[...]
    lam = (jnp.exp(jnp.sum(lp[0:1, :] * lp[1:2, :], axis=-1, keepdims=True))
           - jnp.exp(jnp.sum(lp[2:3, :] * lp[3:4, :], axis=-1, keepdims=True)) + lambda_init)

    def rows_of(blk):
        return pl.ds(pl.multiple_of(blk * tq, tq), tq)

    def write_rows(qi, a0, a1):
        o = a0[:, 0:w] / a0[:, w:2 * w] - lam * (a1[:, 0:w] / a1[:, w:2 * w])
        on = o * lax.rsqrt(jnp.mean(o * o, axis=-1, keepdims=True) + NORM_EPS) * sg_ref[...]
        on = on * (1.0 - lambda_init)
        o_ref[rows_of(qi), :] = (on * gate_ref[rows_of(qi), :].astype(F32)).astype(BF16)
        acc[qi] = jnp.zeros(acc.shape[1:], F32)


    @pl.when(bound <= MAX_STATIC_SHIFT)
    def _():
        q_aug = []
        for sub, base in enumerate((DIFF_D, 0)):
            mlane = lane - (base + 1)
            in_mask = (mlane >= 0) & (mlane < nch)
            q_aug.append(jnp.where((lane == base) | (in_mask & (rchunk == mlane)), 1.0, 0.0).astype(BF16))
            plain = jnp.where(lane == base, -bound, 0.0)
            kaug_s[0, sub] = plain.astype(BF16)
            kaug_s[1, sub] = jnp.where(in_mask & (rchunk > mlane), -MASK_BIG, plain).astype(BF16)

        def scores(qi, j, pslot):
            q = q_ref[rows_of(qi), :]
            kb = k_ref[rows_of(j), :]
            is_diag = jnp.asarray(j == qi, jnp.int32)
            for sub in range(2):
                qs = jnp.where(half[sub], q, q_aug[sub])
                ks = jnp.where(half[sub], kb, kaug_s[is_diag, sub])
                pbuf[pslot, sub] = jnp.exp2(_dot_nt(qs, ks)).astype(BF16)

        def accumulate(qi, j, pslot):
            ve = vext[rows_of(j), :]
            for sub in range(2):
                acc[qi, sub] += _dot(pbuf[pslot, sub], ve)

        def advance(st):
            qi, j, pqi, pj, n = st
            accumulate(pqi, pj, n & 1)
            scores(qi, j, (n + 1) & 1)
            wrap = j >= qi
            return (jnp.where(wrap, qi + 1, qi), jnp.where(wrap, 0, j + 1), qi, j, n + 1)

        todo = nq * (nq + 1) // 2 - 1
        unroll = max([u for u in range(ATT_UNROLL // 2, 2 * ATT_UNROLL) if todo % u == 0] or [ATT_UNROLL],
                     key=lambda u: -abs(u - ATT_UNROLL))

        def trip(t, st):
            for _ in range(unroll):
                st = advance(st)
            return st

        scores(0, 0, 0)
        one, zero = jnp.int32(1), jnp.int32(0)
        trips, rest = divmod(todo, unroll)
        st = lax.fori_loop(0, trips, trip, (one, zero, zero, zero, zero))
        for _ in range(rest):
            st = advance(st)
        accumulate(st[2], st[3], st[4] & 1)

        group = 4 if nq % 4 == 0 else 1

        def write_blocks(g, carry):
            for u in range(group):
                qi = g * group + u
                write_rows(qi, acc[qi, 0], acc[qi, 1])
            return carry

        lax.fori_loop(0, nq // group, write_blocks, 0)

    @pl.when(bound > MAX_STATIC_SHIFT)
    def _():
        zero = jnp.zeros((tk, w), BF16)
        rr = lax.broadcasted_iota(jnp.int32, (tq, tk), 0) // CHUNK
        cc = lax.broadcasted_iota(jnp.int32, (tq, tk), 1) // CHUNK
        causal = cc <= rr

        def outer(qi, carry):
            q = q_ref[rows_of(qi), :]

            def step(j, ms, diag):
                kb = k_ref[rows_of(j), :]
                ve = vext[rows_of(j), :]
                new = []
                for sub in range(2):
                    s = _dot_nt(q, jnp.where(half[sub], kb, zero))
                    if diag:
                        s = jnp.where(causal, s, NEG_BIG)
                    m_new = jnp.maximum(ms[sub], jnp.max(s, axis=-1, keepdims=True))
                    p = jnp.exp2(s - m_new)
                    acc[qi, sub] = jnp.exp2(ms[sub] - m_new) * acc[qi, sub] + _dot(p.astype(BF16), ve)
                    new.append(m_new)
                return tuple(new)

            init = (jnp.full((tq, 1), NEG_BIG, F32), jnp.full((tq, 1), NEG_BIG, F32))
            ms = lax.fori_loop(0, qi, lambda j, c: step(j, c, False), init)
            step(qi, ms, True)
            write_rows(qi, acc[qi, 0], acc[qi, 1])
            return carry

        lax.fori_loop(0, nq, outer, 0)


def _attn(bq, bk, bv, bgate, lam_params, sg_row, gqk_rows, lambda_init):
    b, s, _ = bq.shape
    tq = ATT_TQ
    w = 2 * DIFF_D
    hspec = pl.BlockSpec((None, s, w), lambda bi, hi: (bi, 0, hi))
    const = lambda bi, hi: (0, 0)
    return pl.pallas_call(
        functools.partial(_attn_kernel, lambda_init=lambda_init, tq=tq),
        grid=(b, DIFF_HEADS),
        in_specs=[hspec, hspec, hspec, hspec,
                  pl.BlockSpec((8, LANES), const),
                  pl.BlockSpec((1, w), const),
                  pl.BlockSpec((2, LANES), const)],
        out_specs=hspec,
        out_shape=jax.ShapeDtypeStruct((b, s, HEADW), BF16),
        scratch_shapes=[pltpu.VMEM((s, 2 * w), BF16),
                        pltpu.VMEM((s // tq, 2, tq, 2 * w), F32),
                        pltpu.VMEM((2, 2, tq, tq), BF16),
                        pltpu.VMEM((2, 2, tq, w), BF16)],
        compiler_params=pltpu.CompilerParams(
            dimension_semantics=("arbitrary", "arbitrary"), vmem_limit_bytes=VMEM_LIMIT),
        name="attn",
    )(bq, bk, bv, bgate, lam_params, sg_row, gqk_rows)


def _out_kernel(x_ref, ma_ref, mb_ref, wa_ref, wb_ref, gate_ref, o_ref):
    y = _dot(ma_ref[...], wa_ref[...]) + _dot(mb_ref[...], wb_ref[...])
    o_ref[...] = x_ref[...] + gate_ref[...] * y


def _out(x, mixed_a, mixed_b, w_a, w_b, gate):
    b, s, d = x.shape
    tm = OUT_TM
    row = lambda bi, si: (bi, si, 0)
    const2 = lambda bi, si: (0, 0)
    return pl.pallas_call(
        _out_kernel,
        grid=(b, s // tm),
        in_specs=[pl.BlockSpec((None, tm, d), row),
                  pl.BlockSpec((None, tm, HEADW), row),
                  pl.BlockSpec((None, tm, HEADW), row),
                  pl.BlockSpec((HEADW, d), const2),
                  pl.BlockSpec((HEADW, d), const2),
                  pl.BlockSpec((None, 1, d), lambda bi, si: (bi, 0, 0))],
        out_specs=pl.BlockSpec((None, tm, d), row),
        out_shape=jax.ShapeDtypeStruct((b, s, d), F32),
        compiler_params=pltpu.CompilerParams(
            dimension_semantics=("arbitrary", "arbitrary"), vmem_limit_bytes=VMEM_LIMIT),
        name="out",
    )(x, mixed_a, mixed_b, w_a, w_b, gate)


def _constants():
    sub = np.arange(LANES) % DIFF_D
    half = ROT_DIM // 2
    inv_freq = ROPE_THETA ** (-np.arange(0, ROT_DIM, 2, dtype=np.float32) / ROT_DIM)
    freq_rows = np.zeros((2 * half, LANES), np.float32)
    freq_rows[:half] = inv_freq[:, None]
    idx_row = np.where(sub < ROT_DIM, sub % half, half).astype(np.int32)[None, :]
    sgn_rows = np.stack([np.where(sub < half, -1.0, 0.0),
                         np.where((sub >= half) & (sub < ROT_DIM), 1.0, 0.0)]).astype(np.float32)
    r = np.arange(PROJ_TM)
    lmat = (r[:, None] // CHUNK == r[None, :] // CHUNK) & (r[None, :] <= r[:, None])
    g = np.arange(GMAT_W)
    gmat = np.where(g[:, None] // DIFF_D == g[None, :] // DIFF_D, 1.0 / DIFF_D, 0.0)
    return (jnp.asarray(freq_rows), jnp.asarray(sgn_rows), jnp.asarray(idx_row),
            jnp.asarray(lmat, BF16), jnp.asarray(gmat, BF16))


def _pad_lanes(v, offset):
    return jnp.zeros((1, LANES), F32).at[0, offset:offset + v.shape[0]].set(v.astype(F32))


def _layer(x, c_pad, pos_row, lambda_init, norm_g, w_ada_layers, layer, b_ada, w_in, conv_w, a_log, dt_bias, gdn_norm_g,
           q_norm_g, k_norm_g, lam_q1, lam_k1, lam_q2, lam_k2, subln_g, w_out):
    b, s, d = x.shape
    mod = _ada(c_pad, w_ada_layers, layer, b_ada[None, :])[:b]
    shift = mod[:, None, 0:d]
    scale = mod[:, None, d:2 * d]
    gate = mod[:, None, 2 * d:3 * d]

    nk = GDN_HEADS * GDN_DK
    o_beta = 2 * nk + GDN_HEADS * GDN_DV
    o_agate = o_beta + 2 * GDN_HEADS
    assert (OFF_AGATE, OFF_BD) == (o_beta, o_beta + w_in.shape[1] - o_agate)
    bd = jnp.pad(w_in[:, o_beta:o_agate], ((0, 0), (0, LANES - 2 * GDN_HEADS)))
    w_all = jnp.concatenate([w_in[:, 0:o_beta], w_in[:, o_agate:], bd], axis=1).astype(BF16)
    freq_rows, sgn_rows, idx_row, lmat, gmat = _constants()
    lam_params = jnp.zeros((8, LANES), F32)
    for r, v in enumerate((lam_q1, lam_k1, lam_q2, lam_k2)):
        lam_params = lam_params.at[r, 0:DIFF_D].set(v.astype(F32))

    gq, gk, gv, agate, bq, bk, bv, bgate, bg, bgt = _proj(
        x, pos_row, shift, scale, norm_g[None, :], w_all, conv_w,
        _pad_lanes(a_log, GDN_HEADS), _pad_lanes(dt_bias, GDN_HEADS),
        jnp.tile(q_norm_g, 2 * DIFF_HEADS)[None, :], jnp.tile(k_norm_g, 2 * DIFF_HEADS)[None, :],
        lmat, gmat, freq_rows, sgn_rows, idx_row)
    mixed_a = _gdn(gq, gk, gv, agate, bg, bgt, gdn_norm_g[None, :])
    gqk_rows = jnp.concatenate([_pad_lanes(q_norm_g, 0), _pad_lanes(k_norm_g, 0)], axis=0)
    mixed_b = _attn(bq, bk, bv, bgate, lam_params, subln_g[None, :], gqk_rows, lambda_init)
    w_o = w_out.astype(BF16)
    return _out(x, mixed_a, mixed_b, w_o[:HEADW], w_o[HEADW:], gate)


def kernel(x, c, positions, norm_g, w_ada, b_ada, w_in, conv_w, a_log, dt_bias, gdn_norm_g, q_norm_g,
           k_norm_g, lambda_q1, lambda_k1, lambda_q2, lambda_k2, subln_g, w_out):
    b, s, d = x.shape
    for tile in (PROJ_TM, GDN_TC, ATT_TQ, OUT_TM):
        assert s % tile == 0, (s, tile)
    assert d == 2 * HEADW and w_in.shape[-1] == W_COLS - LANES + 2 * GDN_HEADS
    c_pad = jnp.pad(c, ((0, 8 - b % 8 if b % 8 else 0), (0, 0)))
    pos_row = positions[:, None, :]
    for l in range(norm_g.shape[0]):
        lambda_init = 0.8 - 0.6 * math.exp(-0.3 * l)
        x = _layer(x, c_pad, pos_row, lambda_init, norm_g[l], w_ada, l, b_ada[l], w_in[l], conv_w[l],
                   a_log[l], dt_bias[l], gdn_norm_g[l], q_norm_g[l], k_norm_g[l], lambda_q1[l],
                   lambda_k1[l], lambda_q2[l], lambda_k2[l], subln_g[l], w_out[l])
    return x
```

```python
import functools
import math

import jax
import jax.numpy as jnp
import numpy as np
from jax import lax
from jax.experimental import pallas as pl
from jax.experimental.pallas import tpu as pltpu

F32 = jnp.float32
BF16 = jnp.bfloat16
HIGHEST = lax.Precision.HIGHEST

CHUNK = 64
PAIR = 2 * CHUNK
GDN_HEADS = 4
GDN_DK = 128
GDN_DV = 128
CONV_K = 4
DIFF_HEADS = 4
DIFF_D = 64
ROT_DIM = 16
ROPE_THETA = 500000.0
NORM_EPS = 1e-6
LANES = 128
NEG_BIG = -1e30
MASK_BIG = 256.0

GDN_QKV = 3 * GDN_HEADS * GDN_DK
HEADW = 512
OFF_GQKV = 0
OFF_AGATE = GDN_QKV
OFF_BQ = OFF_AGATE + HEADW
OFF_BK = OFF_BQ + HEADW
OFF_BV = OFF_BK + HEADW
OFF_BGATE = OFF_BV + HEADW
OFF_BD = OFF_BGATE + HEADW
W_COLS = OFF_BD + LANES

PROJ_TM = 512
PROJ_RB = 128
GMAT_W = 256
GDN_TC = 256
ATT_TQ = 512
WRITES_PER_TRIP = 2
ATT_UNROLL = 16
MAX_STATIC_SHIFT = 60.0
OUT_TM = 1024
VMEM_LIMIT = 48 * 1024 * 1024


def _silu(x):
    h = 0.5 * x
    return h + h * jnp.tanh(h)


def _dot(a, b, **kw):
    return jnp.dot(a, b, preferred_element_type=F32, **kw)


def _dot_nt(a, b, **kw):
    return lax.dot_general(a, b, (((1,), (1,)), ((), ())), preferred_element_type=F32, **kw)


def _dot_tn(a, b, **kw):
    return lax.dot_general(a, b, (((0,), (0,)), ((), ())), preferred_element_type=F32, **kw)


def _ada_kernel(c_ref, w_ref, b_ref, o_ref):
    o_ref[...] = _dot(_silu(c_ref[...]), w_ref[...], precision=HIGHEST) + b_ref[...]


def _ada(c_pad, w_ada_layers, layer, b_ada):
    rows, d = c_pad.shape
    n = w_ada_layers.shape[2]
    tn = 512
    return pl.pallas_call(
        _ada_kernel,
        grid=(n // tn,),
        in_specs=[pl.BlockSpec((rows, d), lambda j: (0, 0)),
                  pl.BlockSpec((None, d, tn), lambda j: (layer, 0, j)),
                  pl.BlockSpec((1, tn), lambda j: (0, j))],
        out_specs=pl.BlockSpec((rows, tn), lambda j: (0, j)),
        out_shape=jax.ShapeDtypeStruct((rows, n), F32),
        name="ada",
    )(c_pad, w_ada_layers, b_ada)


def _proj_kernel(x_ref, pos_ref, shift_ref, scale_ref, ng_ref, w_ref, convw_ref, alog_ref, dtb_ref,
                 qg_ref, kg_ref, lmat_ref, gmat_ref, freq_ref, sgn_ref, idx_ref,
                 gq_ref, gk_ref, gv_ref, ag_ref, bq_ref, bk_ref, bv_ref, bgate_ref, bg_ref, bgt_ref,
                 cq, ck, cv, z0, z1, h_s, gm_s, tab):
    tm = x_ref.shape[0]
    x = x_ref[...]
    ms = jnp.mean(x * x, axis=-1, keepdims=True)
    gain = ng_ref[...] * (1.0 + scale_ref[...])
    h_s[...] = (x * lax.rsqrt(ms + NORM_EPS) * gain + shift_ref[...]).astype(BF16)

    cbufs = (cq, ck, cv)
    zbuf = (z0, z1)

    @pl.when(pl.program_id(1) == 0)
    def _():
        for cb in cbufs:
            cb[0:8, :] = jnp.zeros((8, HEADW), F32)

    def mm_pieces(dst, row0, off, width):
        def piece(c0):
            c1 = min(c0 + GMAT_W, width)
            dst[row0:row0 + tm, c0:c1] = _dot(h_s[...], w_ref[:, off + c0:off + c1])
        return [functools.partial(piece, c0) for c0 in range(0, width, GMAT_W)]

    def conv_pieces(g, out_ref):
        cb = cbufs[g]

        def piece(hh):
            hs = slice(hh * LANES, (hh + 1) * LANES)
            cs = slice(g * HEADW + hh * LANES, g * HEADW + (hh + 1) * LANES)
            for r0 in range(0, tm, PROJ_RB):
                acc = convw_ref[3:4, cs] * cb[8 + r0:8 + r0 + PROJ_RB, hs]
                for j in range(CONV_K - 1):
                    acc = acc + convw_ref[j:j + 1, cs] * cb[5 + j + r0:5 + j + r0 + PROJ_RB, hs]
                a = _silu(acc)
                if g < 2:
                    a = a * lax.rsqrt(jnp.sum(a * a, axis=-1, keepdims=True) + NORM_EPS)
                    if g == 0:
                        a = a * (GDN_DK ** -0.5)
                out_ref[r0:r0 + PROJ_RB, hs] = a.astype(BF16)
            cb[0:8, hs] = cb[tm:tm + 8, hs]
        return [functools.partial(piece, hh) for hh in range(GDN_HEADS)]

    def gate_pieces(slot, out_ref):
        def piece(r0):
            rs = slice(r0, r0 + PROJ_RB)
            out_ref[rs, :] = _silu(zbuf[slot][rs, :]).astype(BF16)
        return [functools.partial(piece, r0) for r0 in range(0, tm, PROJ_RB)]

    def qk_pieces(slot, g_ref, out_ref, post):
        def piece(c0):
            z = zbuf[slot][:, c0:c0 + GMAT_W]
            gm_s[:, c0:c0 + GMAT_W] = _dot((z * z).astype(BF16), gmat_ref[...])
            for r0 in range(0, tm, PROJ_RB):
                rs = slice(r0, r0 + PROJ_RB)
                for h0 in range(c0, c0 + GMAT_W, LANES):
                    hs = slice(h0, h0 + LANES)
                    ys = zbuf[slot][rs, hs] * lax.rsqrt(gm_s[rs, hs] + NORM_EPS) * g_ref[:, hs]
                    r = (ys * tab[0, rs, :] + pltpu.roll(ys, LANES - ROT_DIM // 2, axis=1) * tab[1, rs, :]
                         + pltpu.roll(ys, ROT_DIM // 2, axis=1) * tab[2, rs, :])
                    if post is not None:
                        r = r * post
                    out_ref[rs, hs] = r.astype(BF16)
        return [functools.partial(piece, c0) for c0 in range(0, HEADW, GMAT_W)]

    def table_pieces():
        def piece(r0):
            rs = slice(r0, r0 + LANES)
            ang = freq_ref[...] * pos_ref[:, rs].astype(F32)
            pad = jnp.zeros((LANES - ang.shape[0], LANES), F32)
            idx = jnp.broadcast_to(idx_ref[...], (LANES, LANES))
            cos_t = jnp.take_along_axis(jnp.concatenate([jnp.cos(ang), pad], axis=0).T, idx, axis=1)
            sin_t = jnp.take_along_axis(jnp.concatenate([jnp.sin(ang), pad], axis=0).T, idx, axis=1)
            tab[0, rs, :] = cos_t
            tab[1, rs, :] = sin_t * sgn_ref[0:1, :]
            tab[2, rs, :] = sin_t * sgn_ref[1:2, :]
        return [functools.partial(piece, r0) for r0 in range(0, tm, LANES)]

    def ep_bd(slot):
        z = zbuf[slot][:, 0:LANES]
        lane = lax.broadcasted_iota(jnp.int32, (1, LANES), 1)
        beta = 1.0 / (1.0 + jnp.exp(-z))
        zz = z + dtb_ref[...]
        softplus = jnp.maximum(zz, 0.0) + jnp.log(1.0 + jnp.exp(-jnp.abs(zz)))
        is_g = (lane >= GDN_HEADS) & (lane < 2 * GDN_HEADS)
        g = jnp.where(is_g, -jnp.exp(alog_ref[...]) * softplus, 0.0)
        g_hi = g.astype(BF16)
        g_lo = (g - g_hi.astype(F32)).astype(BF16)
        gc = _dot(lmat_ref[...], g_hi) + _dot(lmat_ref[...], g_lo)
        bg = jnp.where(lane < GDN_HEADS, beta, gc)
        bg_ref[...] = bg
        bgt_ref[...] = bg.T[0:8, :]

    def cast_bv():
        bv_ref[...] = zbuf[0][...].astype(BF16)

    def emit(*lists):
        for k in range(max(len(pl_) for pl_ in lists)):
            for pl_ in lists:
                if k < len(pl_):
                    pl_[k]()

    q_scale = DIFF_D ** -0.5 * math.log2(math.e)
    stages = [
        (mm_pieces(cq, 8, OFF_GQKV, HEADW), conv_pieces(0, gq_ref)),
        (mm_pieces(z1, 0, OFF_BD, LANES) + mm_pieces(ck, 8, OFF_GQKV + HEADW, HEADW),
         conv_pieces(1, gk_ref) + [functools.partial(ep_bd, 1)]),
        (mm_pieces(cv, 8, OFF_GQKV + 2 * HEADW, HEADW), conv_pieces(2, gv_ref)),
        (mm_pieces(z0, 0, OFF_BQ, HEADW), qk_pieces(0, qg_ref, bq_ref, q_scale)),
        (mm_pieces(z1, 0, OFF_BK, HEADW), qk_pieces(1, kg_ref, bk_ref, None)),
        (mm_pieces(z0, 0, OFF_AGATE, HEADW), gate_pieces(0, ag_ref)),
        (mm_pieces(z1, 0, OFF_BGATE, HEADW), gate_pieces(1, bgate_ref)),
        (mm_pieces(z0, 0, OFF_BV, HEADW), [cast_bv]),
    ]
    emit(stages[0][0], table_pieces())
    for i, (_, epilogue) in enumerate(stages):
        emit(stages[i + 1][0] if i + 1 < len(stages) else [], epilogue)


def _proj(x, pos_row, shift, scale, norm_g, w_all, conv_w, alog_row, dtb_row, qg_row, kg_row,
          lmat, gmat, freq_rows, sgn_rows, idx_row):
    b, s, d = x.shape
    tm = PROJ_TM
    row = lambda bi, si: (bi, si, 0)
    const2 = lambda bi, si: (0, 0)
    per_b = lambda bi, si: (bi, 0, 0)
    hw = jax.ShapeDtypeStruct((b, s, HEADW), BF16)
    out_shape = [hw] * 8 + [jax.ShapeDtypeStruct((b, s, LANES), F32),
                            jax.ShapeDtypeStruct((b, 8, s), F32)]
    hw_spec = pl.BlockSpec((None, tm, HEADW), row)
    out_specs = [hw_spec] * 8 + [pl.BlockSpec((None, tm, LANES), row),
                                 pl.BlockSpec((None, 8, tm), lambda bi, si: (bi, 0, si))]
    in_specs = [
        pl.BlockSpec((None, tm, d), row),
        pl.BlockSpec((None, 1, tm), lambda bi, si: (bi, 0, si)),
        pl.BlockSpec((None, 1, d), per_b),
        pl.BlockSpec((None, 1, d), per_b),
        pl.BlockSpec((1, d), const2),
        pl.BlockSpec((d, W_COLS), const2),
        pl.BlockSpec((CONV_K, GDN_QKV), const2),
        pl.BlockSpec((1, LANES), const2),
        pl.BlockSpec((1, LANES), const2),
        pl.BlockSpec((1, HEADW), const2),
        pl.BlockSpec((1, HEADW), const2),
        pl.BlockSpec((tm, tm), const2),
        pl.BlockSpec((GMAT_W, GMAT_W), const2),
        pl.BlockSpec((2 * (ROT_DIM // 2), LANES), const2),
        pl.BlockSpec((2, LANES), const2),
        pl.BlockSpec((1, LANES), const2),
    ]
    return pl.pallas_call(
        _proj_kernel,
        grid=(b, s // tm),
        in_specs=in_specs,
        out_specs=out_specs,
        out_shape=out_shape,
        scratch_shapes=[pltpu.VMEM((tm + 8, HEADW), F32)] * 3 + [pltpu.VMEM((tm, HEADW), F32)] * 2
        + [pltpu.VMEM((tm, d), BF16), pltpu.VMEM((tm, HEADW), F32), pltpu.VMEM((3, tm, LANES), F32)],
        compiler_params=pltpu.CompilerParams(
            dimension_semantics=("arbitrary", "arbitrary"), vmem_limit_bytes=VMEM_LIMIT),
        name="proj",
    )(x, pos_row, shift, scale, norm_g, w_all, conv_w, alog_row, dtb_row, qg_row, kg_row,
      lmat, gmat, freq_rows, sgn_rows, idx_row)


def _split(a):
    hi = a.astype(BF16)
    return hi, (a - hi.astype(F32)).astype(BF16)


def _gdn_kernel(q_ref, k_ref, v_ref, gate_ref, bg_ref, bgt_ref, ng_ref, o_ref,
                state, wq_s, u_s, kva_s, vn_s):
    nb, tc = q_ref.shape[0], q_ref.shape[1]
    npair = tc // PAIR
    nc = tc // CHUNK

    @pl.when(pl.program_id(0) == 0)
    def _():
        state[...] = jnp.zeros(state.shape, F32)
        vn_s[...] = jnp.zeros(vn_s.shape, BF16)

    ii = lax.broadcasted_iota(jnp.int32, (CHUNK, PAIR), 0)
    lane_w = lax.broadcasted_iota(jnp.int32, (CHUNK, PAIR), 1)
    jj = lane_w % CHUNK
    left_w = lane_w < CHUNK
    left_p = lax.broadcasted_iota(jnp.int32, (PAIR, PAIR), 1) < CHUNK
    top_col = lax.broadcasted_iota(jnp.int32, (PAIR, 1), 0) < CHUNK
    eye = (ii == jj).astype(F32)
    lower = ii >= jj
    strict = ii > jj

    def widen(m):
        return jnp.where(left_w, m[0:CHUNK], m[CHUNK:PAIR])

    def bdiag(wd):
        z = jnp.zeros_like(wd)
        return jnp.concatenate([jnp.where(left_w, wd, z), jnp.where(left_w, z, wd)], axis=0)

    def mm(parts, rhs):
        n = len(parts)
        b_hi, b_lo = bdiag(rhs[0]), bdiag(rhs[1])
        his = [hi_ for hi_, _ in parts]
        r_hi = _dot(jnp.concatenate(his + [lo_ for _, lo_ in parts], axis=0), b_hi)
        r_lo = _dot(jnp.concatenate(his, axis=0) if n > 1 else his[0], b_lo)
        blk = lambda r, i: r[i * CHUNK:(i + 1) * CHUNK]
        return [blk(r_hi, i) + blk(r_hi, n + i) + blk(r_lo, i) for i in range(n)]

    items = [(bb, hh, p) for bb in range(nb) for hh in range(GDN_HEADS) for p in range(npair)]
    pre = []
    for bb, hh, p in items:
        cs = slice(hh * LANES, (hh + 1) * LANES)
        rs = slice(p * PAIR, (p + 1) * PAIR)
        q2, k2 = q_ref[bb, rs, cs], k_ref[bb, rs, cs]
        gcol2 = bg_ref[bb, rs, GDN_HEADS + hh:GDN_HEADS + hh + 1]
        bcol_w = widen(jnp.broadcast_to(bg_ref[bb, rs, hh:hh + 1], (PAIR, PAIR)))
        gcol_w = widen(jnp.broadcast_to(gcol2, (PAIR, PAIR)))
        grow = bgt_ref[bb, GDN_HEADS + hh:GDN_HEADS + hh + 1, rs]
        dec = jnp.where(lower, jnp.exp(jnp.minimum(gcol_w - grow, 0.0)), 0.0)
        gram = _dot_nt(jnp.concatenate([k2, q2], axis=0), k2)
        xm = jnp.where(strict, -(bcol_w * widen(gram[0:PAIR]) * dec), 0.0)
        a_w = jnp.where(lower, widen(gram[PAIR:2 * PAIR]) * dec, 0.0)
        pre.append((xm, a_w))

    def decayed_operands(item, a_w):
        bb, hh, p = item
        cs = slice(hh * LANES, (hh + 1) * LANES)
        rs = slice(p * PAIR, (p + 1) * PAIR)
        q2, k2 = q_ref[bb, rs, cs], k_ref[bb, rs, cs]
        gcol2 = bg_ref[bb, rs, GDN_HEADS + hh:GDN_HEADS + hh + 1]
        grow = bgt_ref[bb, GDN_HEADS + hh:GDN_HEADS + hh + 1, rs]
        g_end = jnp.where(top_col, grow[:, CHUNK - 1:CHUNK], grow[:, PAIR - 1:PAIR])
        e_g = jnp.exp(gcol2)
        k2f = k2.astype(F32)
        qd = (q2.astype(F32) * e_g).astype(BF16)
        kdt = (k2f * jnp.exp(g_end - gcol2)).T
        a_bd = bdiag(a_w)
        kva_s[bb, hh, 2 * p] = jnp.concatenate(
            [jnp.where(left_p, kdt, 0.0), a_bd[0:CHUNK]], axis=0).astype(BF16)
        kva_s[bb, hh, 2 * p + 1] = jnp.concatenate(
            [jnp.where(left_p, 0.0, kdt), a_bd[CHUNK:PAIR]], axis=0).astype(BF16)
        return qd, (k2f * e_g).astype(BF16)

    levels = 4
    per_level = -(-len(items) // levels)
    side = []
    ts = [eye + xm for xm, _ in pre]
    xs = [_split(xm) for xm, _ in pre]
    ps = [_split(mm([xp], xp)[0]) for xp in xs]
    for lvl in range(levels):
        both = [mm([_split(t), pp], pp) for t, pp in zip(ts, ps)]
        for idx in range(lvl * per_level, min((lvl + 1) * per_level, len(items))):
            side.append(decayed_operands(items[idx], pre[idx][1]))
        ts = [t + tp for t, (tp, _) in zip(ts, both)]
        ps = [_split(sq) for _, sq in both]
    ts = [t + mm([_split(t)], pp)[0] for t, pp in zip(ts, ps)]
    for (bb, hh, p), t, (qd, kg) in zip(items, ts, side):
        cs = slice(hh * LANES, (hh + 1) * LANES)
        rs = slice(p * PAIR, (p + 1) * PAIR)
        tb = bdiag(t * bgt_ref[bb, hh:hh + 1, rs]).astype(BF16)
        uw = _dot(tb, jnp.concatenate([v_ref[bb, rs, cs], kg], axis=1))
        u_s[bb, hh, rs, :] = uw[:, 0:GDN_DV]
        w2 = uw[:, GDN_DV:GDN_DV + GDN_DK].astype(BF16)
        for e in range(2):
            es = slice(e * CHUNK, (e + 1) * CHUNK)
            wq_s[bb, hh, 2 * p + e] = jnp.concatenate([w2[es], qd[es]], axis=0)

    chains = [(bb, hh) for bb in range(nb) for hh in range(GDN_HEADS)]
    for c in range(nc):
        e = c % 2
        rs = slice(c * CHUNK, (c + 1) * CHUNK)
        es = slice(e * CHUNK, (e + 1) * CHUNK)
        sts = [state[ch] for ch in chains]
        rr = [_dot(wq_s[bb, hh, c], st.astype(BF16)) for (bb, hh), st in zip(chains, sts)]
        for (bb, hh), r in zip(chains, rr):
            vn_s[bb, hh, es, :] = (u_s[bb, hh, rs, :] - r[0:CHUNK]).astype(BF16)
        for (bb, hh), st, r in zip(chains, sts, rr):
            cs = slice(hh * LANES, (hh + 1) * LANES)
            g_end = bgt_ref[bb, GDN_HEADS + hh:GDN_HEADS + hh + 1, (c + 1) * CHUNK - 1:(c + 1) * CHUNK]
            kva = _dot(kva_s[bb, hh, c], vn_s[bb, hh])
            state[bb, hh] = st * jnp.exp(g_end) + kva[0:GDN_DK]
            o = r[CHUNK:PAIR] + kva[GDN_DK:GDN_DK + CHUNK]
            on = o * lax.rsqrt(jnp.mean(o * o, axis=-1, keepdims=True) + NORM_EPS) * ng_ref[...]
            o_ref[bb, rs, cs] = (on * gate_ref[bb, rs, cs].astype(F32)).astype(BF16)


def _gdn(gq, gk, gv, agate, bg, bgt, ng_row):
    b, s, _ = gq.shape
    tc = GDN_TC
    row = lambda si: (0, si, 0)
    hw_spec = pl.BlockSpec((b, tc, HEADW), row)
    per_chain = (b, GDN_HEADS)
    return pl.pallas_call(
        _gdn_kernel,
        grid=(s // tc,),
        in_specs=[hw_spec, hw_spec, hw_spec, hw_spec,
                  pl.BlockSpec((b, tc, LANES), row),
                  pl.BlockSpec((b, 8, tc), lambda si: (0, 0, si)),
                  pl.BlockSpec((1, GDN_DV), lambda si: (0, 0))],
        out_specs=hw_spec,
        out_shape=jax.ShapeDtypeStruct((b, s, HEADW), BF16),
        scratch_shapes=[pltpu.VMEM(per_chain + (GDN_DK, GDN_DV), F32),
                        pltpu.VMEM(per_chain + (tc // CHUNK, PAIR, GDN_DK), BF16),
                        pltpu.VMEM(per_chain + (tc, GDN_DV), F32),
                        pltpu.VMEM(per_chain + (tc // CHUNK, GDN_DK + CHUNK, PAIR), BF16),
                        pltpu.VMEM(per_chain + (PAIR, GDN_DV), BF16)],
        compiler_params=pltpu.CompilerParams(
            dimension_semantics=("arbitrary",), vmem_limit_bytes=VMEM_LIMIT),
        name="gdn",
    )(gq, gk, gv, agate, bg, bgt, ng_row)


def _attn_kernel(q_ref, k_ref, v_ref, gate_ref, lamp_ref, sg_ref, gqk_ref, o_ref,
                 vext, acc, pbuf, kaug_s, *, lambda_init, tq):
    s_len = q_ref.shape[0]
    nq = s_len // tq
    tk = tq
    nch = tq // CHUNK
    w = 2 * DIFF_D

    vext[:, 0:w] = v_ref[...]
    vext[:, w:2 * w] = jnp.ones((s_len, w), BF16)

    @pl.when(jnp.logical_and(pl.program_id(0) == 0, pl.program_id(1) == 0))
    def _():
        acc[...] = jnp.zeros(acc.shape, F32)

    lane = lax.broadcasted_iota(jnp.int32, (tk, w), 1)
    rchunk = lax.broadcasted_iota(jnp.int32, (tk, w), 0) // CHUNK
    lo = lane < DIFF_D
    hi = jnp.logical_not(lo)
    half = (lo, hi)

    gqk = gqk_ref[...]
    bound = (jnp.max(jnp.abs(gqk[0:1, :])) * jnp.max(jnp.abs(gqk[1:2, :]))
             * (DIFF_D * DIFF_D ** -0.5 * math.log2(math.e) * 1.01))

    lp = lamp_ref[...]
    lam = (jnp.exp(jnp.sum(lp[0:1, :] * lp[1:2, :], axis=-1, keepdims=True))
           - jnp.exp(jnp.sum(lp[2:3, :] * lp[3:4, :], axis=-1, keepdims=True)) + lambda_init)

    def rows_of(blk):
        return pl.ds(pl.multiple_of(blk * tq, tq), tq)

    def write_rows(qi, a0, a1):
        o = a0[:, 0:w] / a0[:, w:2 * w] - lam * (a1[:, 0:w] / a1[:, w:2 * w])
        on = o * lax.rsqrt(jnp.mean(o * o, axis=-1, keepdims=True) + NORM_EPS) * sg_ref[...]
        on = on * (1.0 - lambda_init)
        o_ref[rows_of(qi), :] = (on * gate_ref[rows_of(qi), :].astype(F32)).astype(BF16)


    @pl.when(bound <= MAX_STATIC_SHIFT)
    def _():
        q_aug = []
        for sub, base in enumerate((DIFF_D, 0)):
            mlane = lane - (base + 1)
            in_mask = (mlane >= 0) & (mlane < nch)
            q_aug.append(jnp.where((lane == base) | (in_mask & (rchunk == mlane)), 1.0, 0.0).astype(BF16))
            plain = jnp.where(lane == base, -bound, 0.0)
            kaug_s[0, sub] = plain.astype(BF16)
            kaug_s[1, sub] = jnp.where(in_mask & (rchunk > mlane), -MASK_BIG, plain).astype(BF16)

        def scores(qi, j, pslot):
            q = q_ref[rows_of(qi), :]
            kb = k_ref[rows_of(j), :]
            is_diag = jnp.asarray(j == qi, jnp.int32)
            for sub in range(2):
                qs = jnp.where(half[sub], q, q_aug[sub])
                ks = jnp.where(half[sub], kb, kaug_s[is_diag, sub])
                pbuf[pslot, sub] = jnp.exp2(_dot_nt(qs, ks)).astype(BF16)

        def accumulate(qi, j, pslot):
            ve = vext[rows_of(j), :]
            keep = jnp.where(j == 0, 0.0, 1.0)
            for sub in range(2):
                acc[qi, sub] = acc[qi, sub] * keep + _dot(pbuf[pslot, sub], ve)

        def advance(st):
            qi, j, pqi, pj, n = st
            accumulate(pqi, pj, n & 1)
            scores(qi, j, (n + 1) & 1)
            wrap = j >= qi
            return (jnp.where(wrap, qi + 1, qi), jnp.where(wrap, 0, j + 1), qi, j, n + 1)

        todo = nq * (nq + 1) // 2 - 1
        unroll = max([u for u in range(ATT_UNROLL // 2, 2 * ATT_UNROLL) if todo % u == 0] or [ATT_UNROLL],
                     key=lambda u: -abs(u - ATT_UNROLL))

        def trip(t, carry):
            st, done = carry
            complete = st[2]
            for u in range(WRITES_PER_TRIP):
                qw = jnp.clip(done + u, 0, jnp.maximum(complete - 1, 0))
                write_rows(qw, acc[qw, 0], acc[qw, 1])
            done = jnp.minimum(done + WRITES_PER_TRIP, complete)
            for _ in range(unroll):
                st = advance(st)
            return st, done

        scores(0, 0, 0)
        one, zero = jnp.int32(1), jnp.int32(0)
        trips, rest = divmod(todo, unroll)
        st, done = lax.fori_loop(0, trips, trip, ((one, zero, zero, zero, zero), zero))
        for _ in range(rest):
            st = advance(st)
        accumulate(st[2], st[3], st[4] & 1)

        def write_block(qi, carry):
            write_rows(qi, acc[qi, 0], acc[qi, 1])
            return carry

        lax.fori_loop(done, nq, write_block, 0)

    @pl.when(bound > MAX_STATIC_SHIFT)
    def _():
        zero = jnp.zeros((tk, w), BF16)
        rr = lax.broadcasted_iota(jnp.int32, (tq, tk), 0) // CHUNK
        cc = lax.broadcasted_iota(jnp.int32, (tq, tk), 1) // CHUNK
        causal = cc <= rr

        def outer(qi, carry):
            q = q_ref[rows_of(qi), :]

            def step(j, ms, diag):
                kb = k_ref[rows_of(j), :]
                ve = vext[rows_of(j), :]
                new = []
                for sub in range(2):
                    s = _dot_nt(q, jnp.where(half[sub], kb, zero))
                    if diag:
                        s = jnp.where(causal, s, NEG_BIG)
                    m_new = jnp.maximum(ms[sub], jnp.max(s, axis=-1, keepdims=True))
                    p = jnp.exp2(s - m_new)
                    acc[qi, sub] = jnp.exp2(ms[sub] - m_new) * acc[qi, sub] + _dot(p.astype(BF16), ve)
                    new.append(m_new)
                return tuple(new)

            init = (jnp.full((tq, 1), NEG_BIG, F32), jnp.full((tq, 1), NEG_BIG, F32))
            ms = lax.fori_loop(0, qi, lambda j, c: step(j, c, False), init)
            step(qi, ms, True)
            write_rows(qi, acc[qi, 0], acc[qi, 1])
            return carry

        lax.fori_loop(0, nq, outer, 0)


def _attn(bq, bk, bv, bgate, lam_params, sg_row, gqk_rows, lambda_init):
    b, s, _ = bq.shape
    tq = ATT_TQ
    w = 2 * DIFF_D
    hspec = pl.BlockSpec((None, s, w), lambda bi, hi: (bi, 0, hi))
    const = lambda bi, hi: (0, 0)
    return pl.pallas_call(
        functools.partial(_attn_kernel, lambda_init=lambda_init, tq=tq),
        grid=(b, DIFF_HEADS),
        in_specs=[hspec, hspec, hspec, hspec,
                  pl.BlockSpec((8, LANES), const),
                  pl.BlockSpec((1, w), const),
                  pl.BlockSpec((2, LANES), const)],
        out_specs=hspec,
        out_shape=jax.ShapeDtypeStruct((b, s, HEADW), BF16),
        scratch_shapes=[pltpu.VMEM((s, 2 * w), BF16),
                        pltpu.VMEM((s // tq, 2, tq, 2 * w), F32),
                        pltpu.VMEM((2, 2, tq, tq), BF16),
                        pltpu.VMEM((2, 2, tq, w), BF16)],
        compiler_params=pltpu.CompilerParams(
            dimension_semantics=("arbitrary", "arbitrary"), vmem_limit_bytes=VMEM_LIMIT),
        name="attn",
    )(bq, bk, bv, bgate, lam_params, sg_row, gqk_rows)


def _out_kernel(x_ref, ma_ref, mb_ref, wa_ref, wb_ref, gate_ref, o_ref):
    y = _dot(ma_ref[...], wa_ref[...]) + _dot(mb_ref[...], wb_ref[...])
    o_ref[...] = x_ref[...] + gate_ref[...] * y


def _out(x, mixed_a, mixed_b, w_a, w_b, gate):
    b, s, d = x.shape
    tm = OUT_TM
    row = lambda bi, si: (bi, si, 0)
    const2 = lambda bi, si: (0, 0)
    return pl.pallas_call(
        _out_kernel,
        grid=(b, s // tm),
        in_specs=[pl.BlockSpec((None, tm, d), row),
                  pl.BlockSpec((None, tm, HEADW), row),
                  pl.BlockSpec((None, tm, HEADW), row),
                  pl.BlockSpec((HEADW, d), const2),
                  pl.BlockSpec((HEADW, d), const2),
                  pl.BlockSpec((None, 1, d), lambda bi, si: (bi, 0, 0))],
        out_specs=pl.BlockSpec((None, tm, d), row),
        out_shape=jax.ShapeDtypeStruct((b, s, d), F32),
        compiler_params=pltpu.CompilerParams(
            dimension_semantics=("arbitrary", "arbitrary"), vmem_limit_bytes=VMEM_LIMIT),
        name="out",
    )(x, mixed_a, mixed_b, w_a, w_b, gate)


def _constants():
    sub = np.arange(LANES) % DIFF_D
    half = ROT_DIM // 2
    inv_freq = ROPE_THETA ** (-np.arange(0, ROT_DIM, 2, dtype=np.float32) / ROT_DIM)
    freq_rows = np.zeros((2 * half, LANES), np.float32)
    freq_rows[:half] = inv_freq[:, None]
    idx_row = np.where(sub < ROT_DIM, sub % half, half).astype(np.int32)[None, :]
    sgn_rows = np.stack([np.where(sub < half, -1.0, 0.0),
                         np.where((sub >= half) & (sub < ROT_DIM), 1.0, 0.0)]).astype(np.float32)
    r = np.arange(PROJ_TM)
    lmat = (r[:, None] // CHUNK == r[None, :] // CHUNK) & (r[None, :] <= r[:, None])
    g = np.arange(GMAT_W)
    gmat = np.where(g[:, None] // DIFF_D == g[None, :] // DIFF_D, 1.0 / DIFF_D, 0.0)
    return (jnp.asarray(freq_rows), jnp.asarray(sgn_rows), jnp.asarray(idx_row),
            jnp.asarray(lmat, BF16), jnp.asarray(gmat, BF16))


def _pad_lanes(v, offset):
    return jnp.zeros((1, LANES), F32).at[0, offset:offset + v.shape[0]].set(v.astype(F32))


def _layer(x, c_pad, pos_row, lambda_init, norm_g, w_ada_layers, layer, b_ada, w_in, conv_w, a_log, dt_bias, gdn_norm_g,
           q_norm_g, k_norm_g, lam_q1, lam_k1, lam_q2, lam_k2, subln_g, w_out):
    b, s, d = x.shape
    mod = _ada(c_pad, w_ada_layers, layer, b_ada[None, :])[:b]
    shift = mod[:, None, 0:d]
    scale = mod[:, None, d:2 * d]
    gate = mod[:, None, 2 * d:3 * d]

    nk = GDN_HEADS * GDN_DK
    o_beta = 2 * nk + GDN_HEADS * GDN_DV
    o_agate = o_beta + 2 * GDN_HEADS
    assert (OFF_AGATE, OFF_BD) == (o_beta, o_beta + w_in.shape[1] - o_agate)
    bd = jnp.pad(w_in[:, o_beta:o_agate], ((0, 0), (0, LANES - 2 * GDN_HEADS)))
    w_all = jnp.concatenate([w_in[:, 0:o_beta], w_in[:, o_agate:], bd], axis=1).astype(BF16)
    freq_rows, sgn_rows, idx_row, lmat, gmat = _constants()
    lam_params = jnp.zeros((8, LANES), F32)
    for r, v in enumerate((lam_q1, lam_k1, lam_q2, lam_k2)):
        lam_params = lam_params.at[r, 0:DIFF_D].set(v.astype(F32))

    gq, gk, gv, agate, bq, bk, bv, bgate, bg, bgt = _proj(
        x, pos_row, shift, scale, norm_g[None, :], w_all, conv_w,
        _pad_lanes(a_log, GDN_HEADS), _pad_lanes(dt_bias, GDN_HEADS),
        jnp.tile(q_norm_g, 2 * DIFF_HEADS)[None, :], jnp.tile(k_norm_g, 2 * DIFF_HEADS)[None, :],
        lmat, gmat, freq_rows, sgn_rows, idx_row)
    mixed_a = _gdn(gq, gk, gv, agate, bg, bgt, gdn_norm_g[None, :])
    gqk_rows = jnp.concatenate([_pad_lanes(q_norm_g, 0), _pad_lanes(k_norm_g, 0)], axis=0)
    mixed_b = _attn(bq, bk, bv, bgate, lam_params, subln_g[None, :], gqk_rows, lambda_init)
    w_o = w_out.astype(BF16)
    return _out(x, mixed_a, mixed_b, w_o[:HEADW], w_o[HEADW:], gate)


def kernel(x, c, positions, norm_g, w_ada, b_ada, w_in, conv_w, a_log, dt_bias, gdn_norm_g, q_norm_g,
           k_norm_g, lambda_q1, lambda_k1, lambda_q2, lambda_k2, subln_g, w_out):
    b, s, d = x.shape
    for tile in (PROJ_TM, GDN_TC, ATT_TQ, OUT_TM):
        assert s % tile == 0, (s, tile)
    assert d == 2 * HEADW and w_in.shape[-1] == W_COLS - LANES + 2 * GDN_HEADS
    c_pad = jnp.pad(c, ((0, 8 - b % 8 if b % 8 else 0), (0, 0)))
    pos_row = positions[:, None, :]
    for l in range(norm_g.shape[0]):
        lambda_init = 0.8 - 0.6 * math.exp(-0.3 * l)
        x = _layer(x, c_pad, pos_row, lambda_init, norm_g[l], w_ada, l, b_ada[l], w_in[l], conv_w[l],
                   a_log[l], dt_bias[l], gdn_norm_g[l], q_norm_g[l], k_norm_g[l], lambda_q1[l],
                   lambda_k1[l], lambda_q2[l], lambda_k2[l], subln_g[l], w_out[l])
    return x
```

```python
import functools
import math

import jax
import jax.numpy as jnp
import numpy as np
from jax import lax
from jax.experimental import pallas as pl
from jax.experimental.pallas import tpu as pltpu

F32 = jnp.float32
BF16 = jnp.bfloat16
HIGHEST = lax.Precision.HIGHEST

CHUNK = 64
PAIR = 2 * CHUNK
GDN_HEADS = 4
GDN_DK = 128
GDN_DV = 128
CONV_K = 4
DIFF_HEADS = 4
DIFF_D = 64
ROT_DIM = 16
ROPE_THETA = 500000.0
NORM_EPS = 1e-6
LANES = 128
NEG_BIG = -1e30
MASK_BIG = 256.0

GDN_QKV = 3 * GDN_HEADS * GDN_DK
HEADW = 512
OFF_GQKV = 0
OFF_AGATE = GDN_QKV
OFF_BQ = OFF_AGATE + HEADW
OFF_BK = OFF_BQ + HEADW
OFF_BV = OFF_BK + HEADW
OFF_BGATE = OFF_BV + HEADW
OFF_BD = OFF_BGATE + HEADW
W_COLS = OFF_BD + LANES

PROJ_TM = 512
PROJ_RB = 128
GMAT_W = 256
GDN_TC = 256
ATT_TQ = 512
WRITES_PER_TRIP = 2
ATT_UNROLL = 16
MAX_STATIC_SHIFT = 60.0
OUT_TM = 1024
VMEM_LIMIT = 48 * 1024 * 1024


def _silu(x):
    h = 0.5 * x
    return h + h * jnp.tanh(h)


def _dot(a, b, **kw):
    return jnp.dot(a, b, preferred_element_type=F32, **kw)


def _dot_nt(a, b, **kw):
    return lax.dot_general(a, b, (((1,), (1,)), ((), ())), preferred_element_type=F32, **kw)


def _ada_kernel(c_ref, w_ref, b_ref, o_ref):
    o_ref[...] = _dot(_silu(c_ref[...]), w_ref[...], precision=HIGHEST) + b_ref[...]


def _ada(c_pad, w_ada_layers, layer, b_ada):
    rows, d = c_pad.shape
    n = w_ada_layers.shape[2]
    tn = 512
    return pl.pallas_call(
        _ada_kernel,
        grid=(n // tn,),
        in_specs=[pl.BlockSpec((rows, d), lambda j: (0, 0)),
                  pl.BlockSpec((None, d, tn), lambda j: (layer, 0, j)),
                  pl.BlockSpec((1, tn), lambda j: (0, j))],
        out_specs=pl.BlockSpec((rows, tn), lambda j: (0, j)),
        out_shape=jax.ShapeDtypeStruct((rows, n), F32),
        name="ada",
    )(c_pad, w_ada_layers, b_ada)


def _proj_kernel(x_ref, pos_ref, shift_ref, scale_ref, ng_ref, w_ref, convw_ref, alog_ref, dtb_ref,
                 qg_ref, kg_ref, lmat_ref, gmat_ref, freq_ref, sgn_ref, idx_ref,
                 gq_ref, gk_ref, gv_ref, ag_ref, bq_ref, bk_ref, bv_ref, bgate_ref, bg_ref, bgt_ref,
                 cq, ck, cv, z0, z1, h_s, gm_s, tab):
    tm = x_ref.shape[0]
    x = x_ref[...]
    ms = jnp.mean(x * x, axis=-1, keepdims=True)
    gain = ng_ref[...] * (1.0 + scale_ref[...])
    h_s[...] = (x * lax.rsqrt(ms + NORM_EPS) * gain + shift_ref[...]).astype(BF16)

    cbufs = (cq, ck, cv)
    zbuf = (z0, z1)

    @pl.when(pl.program_id(1) == 0)
    def _():
        for cb in cbufs:
            cb[0:8, :] = jnp.zeros((8, HEADW), F32)

    def mm_pieces(dst, row0, off, width):
        def piece(c0):
            c1 = min(c0 + GMAT_W, width)
            dst[row0:row0 + tm, c0:c1] = _dot(h_s[...], w_ref[:, off + c0:off + c1])
        return [functools.partial(piece, c0) for c0 in range(0, width, GMAT_W)]

    def conv_pieces(g, out_ref):
        cb = cbufs[g]

        def piece(hh):
            hs = slice(hh * LANES, (hh + 1) * LANES)
            cs = slice(g * HEADW + hh * LANES, g * HEADW + (hh + 1) * LANES)
            for r0 in range(0, tm, PROJ_RB):
                acc = convw_ref[3:4, cs] * cb[8 + r0:8 + r0 + PROJ_RB, hs]
                for j in range(CONV_K - 1):
                    acc = acc + convw_ref[j:j + 1, cs] * cb[5 + j + r0:5 + j + r0 + PROJ_RB, hs]
                a = _silu(acc)
                if g < 2:
                    a = a * lax.rsqrt(jnp.sum(a * a, axis=-1, keepdims=True) + NORM_EPS)
                    if g == 0:
                        a = a * (GDN_DK ** -0.5)
                out_ref[r0:r0 + PROJ_RB, hs] = a.astype(BF16)
            cb[0:8, hs] = cb[tm:tm + 8, hs]
        return [functools.partial(piece, hh) for hh in range(GDN_HEADS)]

    def gate_pieces(slot, out_ref):
        def piece(r0):
            rs = slice(r0, r0 + PROJ_RB)
            out_ref[rs, :] = _silu(zbuf[slot][rs, :]).astype(BF16)
        return [functools.partial(piece, r0) for r0 in range(0, tm, PROJ_RB)]

    def qk_pieces(slot, g_ref, out_ref, post):
        def piece(c0):
            z = zbuf[slot][:, c0:c0 + GMAT_W]
            gm_s[:, c0:c0 + GMAT_W] = _dot((z * z).astype(BF16), gmat_ref[...])
            for r0 in range(0, tm, PROJ_RB):
                rs = slice(r0, r0 + PROJ_RB)
                for h0 in range(c0, c0 + GMAT_W, LANES):
                    hs = slice(h0, h0 + LANES)
                    ys = zbuf[slot][rs, hs] * lax.rsqrt(gm_s[rs, hs] + NORM_EPS) * g_ref[:, hs]
                    r = (ys * tab[0, rs, :] + pltpu.roll(ys, LANES - ROT_DIM // 2, axis=1) * tab[1, rs, :]
                         + pltpu.roll(ys, ROT_DIM // 2, axis=1) * tab[2, rs, :])
                    if post is not None:
                        r = r * post
                    out_ref[rs, hs] = r.astype(BF16)
        return [functools.partial(piece, c0) for c0 in range(0, HEADW, GMAT_W)]

    def table_pieces():
        def piece(r0):
            rs = slice(r0, r0 + LANES)
            ang = freq_ref[...] * pos_ref[:, rs].astype(F32)
            pad = jnp.zeros((LANES - ang.shape[0], LANES), F32)
            idx = jnp.broadcast_to(idx_ref[...], (LANES, LANES))
            cos_t = jnp.take_along_axis(jnp.concatenate([jnp.cos(ang), pad], axis=0).T, idx, axis=1)
            sin_t = jnp.take_along_axis(jnp.concatenate([jnp.sin(ang), pad], axis=0).T, idx, axis=1)
            tab[0, rs, :] = cos_t
            tab[1, rs, :] = sin_t * sgn_ref[0:1, :]
            tab[2, rs, :] = sin_t * sgn_ref[1:2, :]
        return [functools.partial(piece, r0) for r0 in range(0, tm, LANES)]

    def ep_bd(slot):
        z = zbuf[slot][:, 0:LANES]
        lane = lax.broadcasted_iota(jnp.int32, (1, LANES), 1)
        beta = 1.0 / (1.0 + jnp.exp(-z))
        zz = z + dtb_ref[...]
        softplus = jnp.maximum(zz, 0.0) + jnp.log(1.0 + jnp.exp(-jnp.abs(zz)))
        is_g = (lane >= GDN_HEADS) & (lane < 2 * GDN_HEADS)
        g = jnp.where(is_g, -jnp.exp(alog_ref[...]) * softplus, 0.0)
        g_hi = g.astype(BF16)
        g_lo = (g - g_hi.astype(F32)).astype(BF16)
        gc = _dot(lmat_ref[...], g_hi) + _dot(lmat_ref[...], g_lo)
        bg = jnp.where(lane < GDN_HEADS, beta, gc)
        bg_ref[...] = bg
        bgt_ref[...] = bg.T[0:8, :]

    def cast_bv():
        bv_ref[...] = zbuf[0][...].astype(BF16)

    def emit(*lists):
        for k in range(max(len(pl_) for pl_ in lists)):
            for pl_ in lists:
                if k < len(pl_):
                    pl_[k]()

    q_scale = DIFF_D ** -0.5 * math.log2(math.e)
    stages = [
        (mm_pieces(cq, 8, OFF_GQKV, HEADW), conv_pieces(0, gq_ref)),
        (mm_pieces(z1, 0, OFF_BD, LANES) + mm_pieces(ck, 8, OFF_GQKV + HEADW, HEADW),
         conv_pieces(1, gk_ref) + [functools.partial(ep_bd, 1)]),
        (mm_pieces(cv, 8, OFF_GQKV + 2 * HEADW, HEADW), conv_pieces(2, gv_ref)),
        (mm_pieces(z0, 0, OFF_BQ, HEADW), qk_pieces(0, qg_ref, bq_ref, q_scale)),
        (mm_pieces(z1, 0, OFF_BK, HEADW), qk_pieces(1, kg_ref, bk_ref, None)),
        (mm_pieces(z0, 0, OFF_AGATE, HEADW), gate_pieces(0, ag_ref)),
        (mm_pieces(z1, 0, OFF_BGATE, HEADW), gate_pieces(1, bgate_ref)),
        (mm_pieces(z0, 0, OFF_BV, HEADW), [cast_bv]),
    ]
    emit(stages[0][0], table_pieces())
    for i, (_, epilogue) in enumerate(stages):
        emit(stages[i + 1][0] if i + 1 < len(stages) else [], epilogue)


def _proj(x, pos_row, shift, scale, norm_g, w_all, conv_w, alog_row, dtb_row, qg_row, kg_row,
          lmat, gmat, freq_rows, sgn_rows, idx_row):
    b, s, d = x.shape
    tm = PROJ_TM
    row = lambda bi, si: (bi, si, 0)
    const2 = lambda bi, si: (0, 0)
    per_b = lambda bi, si: (bi, 0, 0)
    hw = jax.ShapeDtypeStruct((b, s, HEADW), BF16)
    out_shape = [hw] * 8 + [jax.ShapeDtypeStruct((b, s, LANES), F32),
                            jax.ShapeDtypeStruct((b, 8, s), F32)]
    hw_spec = pl.BlockSpec((None, tm, HEADW), row)
    out_specs = [hw_spec] * 8 + [pl.BlockSpec((None, tm, LANES), row),
                                 pl.BlockSpec((None, 8, tm), lambda bi, si: (bi, 0, si))]
    in_specs = [
        pl.BlockSpec((None, tm, d), row),
        pl.BlockSpec((None, 1, tm), lambda bi, si: (bi, 0, si)),
        pl.BlockSpec((None, 1, d), per_b),
        pl.BlockSpec((None, 1, d), per_b),
        pl.BlockSpec((1, d), const2),
        pl.BlockSpec((d, W_COLS), const2),
        pl.BlockSpec((CONV_K, GDN_QKV), const2),
        pl.BlockSpec((1, LANES), const2),
        pl.BlockSpec((1, LANES), const2),
        pl.BlockSpec((1, HEADW), const2),
        pl.BlockSpec((1, HEADW), const2),
        pl.BlockSpec((tm, tm), const2),
        pl.BlockSpec((GMAT_W, GMAT_W), const2),
        pl.BlockSpec((2 * (ROT_DIM // 2), LANES), const2),
        pl.BlockSpec((2, LANES), const2),
        pl.BlockSpec((1, LANES), const2),
    ]
    return pl.pallas_call(
        _proj_kernel,
        grid=(b, s // tm),
        in_specs=in_specs,
        out_specs=out_specs,
        out_shape=out_shape,
        scratch_shapes=[pltpu.VMEM((tm + 8, HEADW), F32)] * 3 + [pltpu.VMEM((tm, HEADW), F32)] * 2
        + [pltpu.VMEM((tm, d), BF16), pltpu.VMEM((tm, HEADW), F32), pltpu.VMEM((3, tm, LANES), F32)],
        compiler_params=pltpu.CompilerParams(
            dimension_semantics=("arbitrary", "arbitrary"), vmem_limit_bytes=VMEM_LIMIT),
        name="proj",
    )(x, pos_row, shift, scale, norm_g, w_all, conv_w, alog_row, dtb_row, qg_row, kg_row,
      lmat, gmat, freq_rows, sgn_rows, idx_row)


def _split(a):
    hi = a.astype(BF16)
    return hi, (a - hi.astype(F32)).astype(BF16)


def _gdn_kernel(q_ref, k_ref, v_ref, gate_ref, bg_ref, bgt_ref, ng_ref, o_ref,
                state, wq_s, u_s, kva_s, vn_s):
    nb, tc = q_ref.shape[0], q_ref.shape[1]
    npair = tc // PAIR
    nc = tc // CHUNK

    @pl.when(pl.program_id(0) == 0)
    def _():
        state[...] = jnp.zeros(state.shape, F32)
        vn_s[...] = jnp.zeros(vn_s.shape, BF16)

    ii = lax.broadcasted_iota(jnp.int32, (CHUNK, PAIR), 0)
    lane_w = lax.broadcasted_iota(jnp.int32, (CHUNK, PAIR), 1)
    jj = lane_w % CHUNK
    left_w = lane_w < CHUNK
    left_p = lax.broadcasted_iota(jnp.int32, (PAIR, PAIR), 1) < CHUNK
    top_col = lax.broadcasted_iota(jnp.int32, (PAIR, 1), 0) < CHUNK
    eye = (ii == jj).astype(F32)
    lower = ii >= jj
    strict = ii > jj

    def widen(m):
        return jnp.where(left_w, m[0:CHUNK], m[CHUNK:PAIR])

    def bdiag(wd):
        z = jnp.zeros_like(wd)
        return jnp.concatenate([jnp.where(left_w, wd, z), jnp.where(left_w, z, wd)], axis=0)

    def mm(parts, rhs):
        n = len(parts)
        b_hi, b_lo = bdiag(rhs[0]), bdiag(rhs[1])
        his = [hi_ for hi_, _ in parts]
        r_hi = _dot(jnp.concatenate(his + [lo_ for _, lo_ in parts], axis=0), b_hi)
        r_lo = _dot(jnp.concatenate(his, axis=0) if n > 1 else his[0], b_lo)
        blk = lambda r, i: r[i * CHUNK:(i + 1) * CHUNK]
        return [blk(r_hi, i) + blk(r_hi, n + i) + blk(r_lo, i) for i in range(n)]

    items = [(bb, hh, p) for bb in range(nb) for hh in range(GDN_HEADS) for p in range(npair)]
    pre = []
    for bb, hh, p in items:
        cs = slice(hh * LANES, (hh + 1) * LANES)
        rs = slice(p * PAIR, (p + 1) * PAIR)
        q2, k2 = q_ref[bb, rs, cs], k_ref[bb, rs, cs]
        gcol2 = bg_ref[bb, rs, GDN_HEADS + hh:GDN_HEADS + hh + 1]
        bcol_w = widen(jnp.broadcast_to(bg_ref[bb, rs, hh:hh + 1], (PAIR, PAIR)))
        gcol_w = widen(jnp.broadcast_to(gcol2, (PAIR, PAIR)))
        grow = bgt_ref[bb, GDN_HEADS + hh:GDN_HEADS + hh + 1, rs]
        dec = jnp.where(lower, jnp.exp(jnp.minimum(gcol_w - grow, 0.0)), 0.0)
        gram = _dot_nt(jnp.concatenate([k2, q2], axis=0), k2)
        xm = jnp.where(strict, -(bcol_w * widen(gram[0:PAIR]) * dec), 0.0)
        a_w = jnp.where(lower, widen(gram[PAIR:2 * PAIR]) * dec, 0.0)
        pre.append((xm, a_w))

    def decayed_operands(item, a_w):
        bb, hh, p = item
        cs = slice(hh * LANES, (hh + 1) * LANES)
        rs = slice(p * PAIR, (p + 1) * PAIR)
        q2, k2 = q_ref[bb, rs, cs], k_ref[bb, rs, cs]
        gcol2 = bg_ref[bb, rs, GDN_HEADS + hh:GDN_HEADS + hh + 1]
        grow = bgt_ref[bb, GDN_HEADS + hh:GDN_HEADS + hh + 1, rs]
        g_end = jnp.where(top_col, grow[:, CHUNK - 1:CHUNK], grow[:, PAIR - 1:PAIR])
        e_g = jnp.exp(gcol2)
        k2f = k2.astype(F32)
        qd = (q2.astype(F32) * e_g).astype(BF16)
        kdt = (k2f * jnp.exp(g_end - gcol2)).T
        a_bd = bdiag(a_w)
        kva_s[bb, hh, 2 * p] = jnp.concatenate(
            [jnp.where(left_p, kdt, 0.0), a_bd[0:CHUNK]], axis=0).astype(BF16)
        kva_s[bb, hh, 2 * p + 1] = jnp.concatenate(
            [jnp.where(left_p, 0.0, kdt), a_bd[CHUNK:PAIR]], axis=0).astype(BF16)
        return qd, (k2f * e_g).astype(BF16)

    levels = 4
    per_level = -(-len(items) // levels)
    side = []
    ts = [eye + xm for xm, _ in pre]
    xs = [_split(xm) for xm, _ in pre]
    ps = [_split(mm([xp], xp)[0]) for xp in xs]
    for lvl in range(levels):
        both = [mm([_split(t), pp], pp) for t, pp in zip(ts, ps)]
        for idx in range(lvl * per_level, min((lvl + 1) * per_level, len(items))):
            side.append(decayed_operands(items[idx], pre[idx][1]))
        ts = [t + tp for t, (tp, _) in zip(ts, both)]
        ps = [_split(sq) for _, sq in both]
    ts = [t + mm([_split(t)], pp)[0] for t, pp in zip(ts, ps)]
    for (bb, hh, p), t, (qd, kg) in zip(items, ts, side):
        cs = slice(hh * LANES, (hh + 1) * LANES)
        rs = slice(p * PAIR, (p + 1) * PAIR)
        tb = bdiag(t * bgt_ref[bb, hh:hh + 1, rs]).astype(BF16)
        uw = _dot(tb, jnp.concatenate([v_ref[bb, rs, cs], kg], axis=1))
        u_s[bb, hh, rs, :] = uw[:, 0:GDN_DV]
        w2 = uw[:, GDN_DV:GDN_DV + GDN_DK].astype(BF16)
        for e in range(2):
            es = slice(e * CHUNK, (e + 1) * CHUNK)
            wq_s[bb, hh, 2 * p + e] = jnp.concatenate([w2[es], qd[es]], axis=0)

    chains = [(bb, hh) for bb in range(nb) for hh in range(GDN_HEADS)]
    for c in range(nc):
        e = c % 2
        rs = slice(c * CHUNK, (c + 1) * CHUNK)
        es = slice(e * CHUNK, (e + 1) * CHUNK)
        sts = [state[ch] for ch in chains]
        rr = [_dot(wq_s[bb, hh, c], st.astype(BF16)) for (bb, hh), st in zip(chains, sts)]
        for (bb, hh), r in zip(chains, rr):
            vn_s[bb, hh, es, :] = (u_s[bb, hh, rs, :] - r[0:CHUNK]).astype(BF16)
        for (bb, hh), st, r in zip(chains, sts, rr):
            cs = slice(hh * LANES, (hh + 1) * LANES)
            g_end = bgt_ref[bb, GDN_HEADS + hh:GDN_HEADS + hh + 1, (c + 1) * CHUNK - 1:(c + 1) * CHUNK]
            kva = _dot(kva_s[bb, hh, c], vn_s[bb, hh])
            state[bb, hh] = st * jnp.exp(g_end) + kva[0:GDN_DK]
            o = r[CHUNK:PAIR] + kva[GDN_DK:GDN_DK + CHUNK]
            on = o * lax.rsqrt(jnp.mean(o * o, axis=-1, keepdims=True) + NORM_EPS) * ng_ref[...]
            o_ref[bb, rs, cs] = (on * gate_ref[bb, rs, cs].astype(F32)).astype(BF16)


def _gdn(gq, gk, gv, agate, bg, bgt, ng_row):
    b, s, _ = gq.shape
    tc = GDN_TC
    row = lambda si: (0, si, 0)
    hw_spec = pl.BlockSpec((b, tc, HEADW), row)
    per_chain = (b, GDN_HEADS)
    return pl.pallas_call(
        _gdn_kernel,
        grid=(s // tc,),
        in_specs=[hw_spec, hw_spec, hw_spec, hw_spec,
                  pl.BlockSpec((b, tc, LANES), row),
                  pl.BlockSpec((b, 8, tc), lambda si: (0, 0, si)),
                  pl.BlockSpec((1, GDN_DV), lambda si: (0, 0))],
        out_specs=hw_spec,
        out_shape=jax.ShapeDtypeStruct((b, s, HEADW), BF16),
        scratch_shapes=[pltpu.VMEM(per_chain + (GDN_DK, GDN_DV), F32),
                        pltpu.VMEM(per_chain + (tc // CHUNK, PAIR, GDN_DK), BF16),
                        pltpu.VMEM(per_chain + (tc, GDN_DV), F32),
                        pltpu.VMEM(per_chain + (tc // CHUNK, GDN_DK + CHUNK, PAIR), BF16),
                        pltpu.VMEM(per_chain + (PAIR, GDN_DV), BF16)],
        compiler_params=pltpu.CompilerParams(
            dimension_semantics=("arbitrary",), vmem_limit_bytes=VMEM_LIMIT),
        name="gdn",
    )(gq, gk, gv, agate, bg, bgt, ng_row)


def _attn_kernel(q_ref, k_ref, v_ref, gate_ref, lamp_ref, sg_ref, gqk_ref, o_ref,
                 vext, acc, pbuf, kaug_s, *, lambda_init, tq):
    s_len = q_ref.shape[0]
    nq = s_len // tq
    tk = tq
    nch = tq // CHUNK
    w = 2 * DIFF_D

    vext[:, 0:w] = v_ref[...]
    vext[:, w:2 * w] = jnp.ones((s_len, w), BF16)

    @pl.when(jnp.logical_and(pl.program_id(0) == 0, pl.program_id(1) == 0))
    def _():
        acc[...] = jnp.zeros(acc.shape, F32)

    lane = lax.broadcasted_iota(jnp.int32, (tk, w), 1)
    rchunk = lax.broadcasted_iota(jnp.int32, (tk, w), 0) // CHUNK
    lo = lane < DIFF_D
    hi = jnp.logical_not(lo)
    half = (lo, hi)

    gqk = gqk_ref[...]
    bound = (jnp.max(jnp.abs(gqk[0:1, :])) * jnp.max(jnp.abs(gqk[1:2, :]))
             * (DIFF_D * DIFF_D ** -0.5 * math.log2(math.e) * 1.01))

    lp = lamp_ref[...]
    lam = (jnp.exp(jnp.sum(lp[0:1, :] * lp[1:2, :], axis=-1, keepdims=True))
           - jnp.exp(jnp.sum(lp[2:3, :] * lp[3:4, :], axis=-1, keepdims=True)) + lambda_init)

    def rows_of(blk):
        return pl.ds(pl.multiple_of(blk * tq, tq), tq)

    def write_rows(qi, a0, a1):
        o = a0[:, 0:w] / a0[:, w:2 * w] - lam * (a1[:, 0:w] / a1[:, w:2 * w])
        on = o * lax.rsqrt(jnp.mean(o * o, axis=-1, keepdims=True) + NORM_EPS) * sg_ref[...]
        on = on * (1.0 - lambda_init)
        o_ref[rows_of(qi), :] = (on * gate_ref[rows_of(qi), :].astype(F32)).astype(BF16)


    @pl.when(bound <= MAX_STATIC_SHIFT)
    def _():
        q_aug = []
        for sub, base in enumerate((DIFF_D, 0)):
            mlane = lane - (base + 1)
            in_mask = (mlane >= 0) & (mlane < nch)
            q_aug.append(jnp.where((lane == base) | (in_mask & (rchunk == mlane)), 1.0, 0.0).astype(BF16))
            plain = jnp.where(lane == base, -bound, 0.0)
            kaug_s[0, sub] = plain.astype(BF16)
            kaug_s[1, sub] = jnp.where(in_mask & (rchunk > mlane), -MASK_BIG, plain).astype(BF16)

        def scores(qi, j, pslot):
            q = q_ref[rows_of(qi), :]
            kb = k_ref[rows_of(j), :]
            is_diag = jnp.asarray(j == qi, jnp.int32)
            for sub in range(2):
                qs = jnp.where(half[sub], q, q_aug[sub])
                ks = jnp.where(half[sub], kb, kaug_s[is_diag, sub])
                pbuf[pslot, sub] = jnp.exp2(_dot_nt(qs, ks)).astype(BF16)

        def accumulate(qi, j, pslot):
            ve = vext[rows_of(j), :]
            keep = jnp.where(j == 0, 0.0, 1.0)
            for sub in range(2):
                acc[qi, sub] = acc[qi, sub] * keep + _dot(pbuf[pslot, sub], ve)

        def advance(st):
            qi, j, pqi, pj, n = st
            accumulate(pqi, pj, n & 1)
            scores(qi, j, (n + 1) & 1)
            wrap = j >= qi
            return (jnp.where(wrap, qi + 1, qi), jnp.where(wrap, 0, j + 1), qi, j, n + 1)

        todo = nq * (nq + 1) // 2 - 1
        unroll = max([u for u in range(ATT_UNROLL // 2, 2 * ATT_UNROLL) if todo % u == 0] or [ATT_UNROLL],
                     key=lambda u: -abs(u - ATT_UNROLL))

        def trip(t, carry):
            st, done = carry
            complete = st[2]
            for u in range(WRITES_PER_TRIP):
                qw = jnp.clip(done + u, 0, jnp.maximum(complete - 1, 0))
                write_rows(qw, acc[qw, 0], acc[qw, 1])
            done = jnp.minimum(done + WRITES_PER_TRIP, complete)
            for _ in range(unroll):
                st = advance(st)
            return st, done

        scores(0, 0, 0)
        one, zero = jnp.int32(1), jnp.int32(0)
        trips, rest = divmod(todo, unroll)
        st, done = lax.fori_loop(0, trips, trip, ((one, zero, zero, zero, zero), zero))
        for _ in range(rest):
            st = advance(st)
        accumulate(st[2], st[3], st[4] & 1)

        def write_block(qi, carry):
            write_rows(qi, acc[qi, 0], acc[qi, 1])
            return carry

        lax.fori_loop(done, nq, write_block, 0)

    @pl.when(bound > MAX_STATIC_SHIFT)
    def _():
        zero = jnp.zeros((tk, w), BF16)
        rr = lax.broadcasted_iota(jnp.int32, (tq, tk), 0) // CHUNK
        cc = lax.broadcasted_iota(jnp.int32, (tq, tk), 1) // CHUNK
        causal = cc <= rr

        def outer(qi, carry):
            q = q_ref[rows_of(qi), :]

            def step(j, ms, diag):
                kb = k_ref[rows_of(j), :]
                ve = vext[rows_of(j), :]
                new = []
                for sub in range(2):
                    s = _dot_nt(q, jnp.where(half[sub], kb, zero))
                    if diag:
                        s = jnp.where(causal, s, NEG_BIG)
                    m_new = jnp.maximum(ms[sub], jnp.max(s, axis=-1, keepdims=True))
                    p = jnp.exp2(s - m_new)
                    acc[qi, sub] = jnp.exp2(ms[sub] - m_new) * acc[qi, sub] + _dot(p.astype(BF16), ve)
                    new.append(m_new)
                return tuple(new)

            init = (jnp.full((tq, 1), NEG_BIG, F32), jnp.full((tq, 1), NEG_BIG, F32))
            ms = lax.fori_loop(0, qi, lambda j, c: step(j, c, False), init)
            step(qi, ms, True)
            write_rows(qi, acc[qi, 0], acc[qi, 1])
            return carry

        lax.fori_loop(0, nq, outer, 0)


def _attn(bq, bk, bv, bgate, lam_params, sg_row, gqk_rows, lambda_init):
    b, s, _ = bq.shape
    tq = ATT_TQ
    w = 2 * DIFF_D
    hspec = pl.BlockSpec((None, s, w), lambda bi, hi: (bi, 0, hi))
    const = lambda bi, hi: (0, 0)
    return pl.pallas_call(
        functools.partial(_attn_kernel, lambda_init=lambda_init, tq=tq),
        grid=(b, DIFF_HEADS),
        in_specs=[hspec, hspec, hspec, hspec,
                  pl.BlockSpec((8, LANES), const),
                  pl.BlockSpec((1, w), const),
                  pl.BlockSpec((2, LANES), const)],
        out_specs=hspec,
        out_shape=jax.ShapeDtypeStruct((b, s, HEADW), BF16),
        scratch_shapes=[pltpu.VMEM((s, 2 * w), BF16),
                        pltpu.VMEM((s // tq, 2, tq, 2 * w), F32),
                        pltpu.VMEM((2, 2, tq, tq), BF16),
                        pltpu.VMEM((2, 2, tq, w), BF16)],
        compiler_params=pltpu.CompilerParams(
            dimension_semantics=("arbitrary", "arbitrary"), vmem_limit_bytes=VMEM_LIMIT),
        name="attn",
    )(bq, bk, bv, bgate, lam_params, sg_row, gqk_rows)


def _out_kernel(x_ref, ma_ref, mb_ref, wa_ref, wb_ref, gate_ref, o_ref):
    y = _dot(ma_ref[...], wa_ref[...]) + _dot(mb_ref[...], wb_ref[...])
    o_ref[...] = x_ref[...] + gate_ref[...] * y


def _out(x, mixed_a, mixed_b, w_a, w_b, gate):
    b, s, d = x.shape
    tm = OUT_TM
    row = lambda bi, si: (bi, si, 0)
    const2 = lambda bi, si: (0, 0)
    return pl.pallas_call(
        _out_kernel,
        grid=(b, s // tm),
        in_specs=[pl.BlockSpec((None, tm, d), row),
                  pl.BlockSpec((None, tm, HEADW), row),
                  pl.BlockSpec((None, tm, HEADW), row),
                  pl.BlockSpec((HEADW, d), const2),
                  pl.BlockSpec((HEADW, d), const2),
                  pl.BlockSpec((None, 1, d), lambda bi, si: (bi, 0, 0))],
        out_specs=pl.BlockSpec((None, tm, d), row),
        out_shape=jax.ShapeDtypeStruct((b, s, d), F32),
        compiler_params=pltpu.CompilerParams(
            dimension_semantics=("arbitrary", "arbitrary"), vmem_limit_bytes=VMEM_LIMIT),
        name="out",
    )(x, mixed_a, mixed_b, w_a, w_b, gate)


def _constants():
    sub = np.arange(LANES) % DIFF_D
    half = ROT_DIM // 2
    inv_freq = ROPE_THETA ** (-np.arange(0, ROT_DIM, 2, dtype=np.float32) / ROT_DIM)
    freq_rows = np.zeros((2 * half, LANES), np.float32)
    freq_rows[:half] = inv_freq[:, None]
    idx_row = np.where(sub < ROT_DIM, sub % half, half).astype(np.int32)[None, :]
    sgn_rows = np.stack([np.where(sub < half, -1.0, 0.0),
                         np.where((sub >= half) & (sub < ROT_DIM), 1.0, 0.0)]).astype(np.float32)
    r = np.arange(PROJ_TM)
    lmat = (r[:, None] // CHUNK == r[None, :] // CHUNK) & (r[None, :] <= r[:, None])
    g = np.arange(GMAT_W)
    gmat = np.where(g[:, None] // DIFF_D == g[None, :] // DIFF_D, 1.0 / DIFF_D, 0.0)
    return (jnp.asarray(freq_rows), jnp.asarray(sgn_rows), jnp.asarray(idx_row),
            jnp.asarray(lmat, BF16), jnp.asarray(gmat, BF16))


def _pad_lanes(v, offset):
    return jnp.zeros((1, LANES), F32).at[0, offset:offset + v.shape[0]].set(v.astype(F32))


def _layer(x, c_pad, pos_row, lambda_init, norm_g, w_ada_layers, layer, b_ada, w_in, conv_w, a_log, dt_bias, gdn_norm_g,
           q_norm_g, k_norm_g, lam_q1, lam_k1, lam_q2, lam_k2, subln_g, w_out):
    b, s, d = x.shape
    mod = _ada(c_pad, w_ada_layers, layer, b_ada[None, :])[:b]
    shift = mod[:, None, 0:d]
    scale = mod[:, None, d:2 * d]
    gate = mod[:, None, 2 * d:3 * d]

    nk = GDN_HEADS * GDN_DK
    o_beta = 2 * nk + GDN_HEADS * GDN_DV
    o_agate = o_beta + 2 * GDN_HEADS
    assert (OFF_AGATE, OFF_BD) == (o_beta, o_beta + w_in.shape[1] - o_agate)
    bd = jnp.pad(w_in[:, o_beta:o_agate], ((0, 0), (0, LANES - 2 * GDN_HEADS)))
    w_all = jnp.concatenate([w_in[:, 0:o_beta], w_in[:, o_agate:], bd], axis=1).astype(BF16)
    freq_rows, sgn_rows, idx_row, lmat, gmat = _constants()
    lam_params = jnp.zeros((8, LANES), F32)
    for r, v in enumerate((lam_q1, lam_k1, lam_q2, lam_k2)):
        lam_params = lam_params.at[r, 0:DIFF_D].set(v.astype(F32))

    gq, gk, gv, agate, bq, bk, bv, bgate, bg, bgt = _proj(
        x, pos_row, shift, scale, norm_g[None, :], w_all, conv_w,
        _pad_lanes(a_log, GDN_HEADS), _pad_lanes(dt_bias, GDN_HEADS),
        jnp.tile(q_norm_g, 2 * DIFF_HEADS)[None, :], jnp.tile(k_norm_g, 2 * DIFF_HEADS)[None, :],
        lmat, gmat, freq_rows, sgn_rows, idx_row)
    mixed_a = _gdn(gq, gk, gv, agate, bg, bgt, gdn_norm_g[None, :])
    gqk_rows = jnp.concatenate([_pad_lanes(q_norm_g, 0), _pad_lanes(k_norm_g, 0)], axis=0)
    mixed_b = _attn(bq, bk, bv, bgate, lam_params, subln_g[None, :], gqk_rows, lambda_init)
    w_o = w_out.astype(BF16)
    return _out(x, mixed_a, mixed_b, w_o[:HEADW], w_o[HEADW:], gate)


def kernel(x, c, positions, norm_g, w_ada, b_ada, w_in, conv_w, a_log, dt_bias, gdn_norm_g, q_norm_g,
           k_norm_g, lambda_q1, lambda_k1, lambda_q2, lambda_k2, subln_g, w_out):
    b, s, d = x.shape
    for tile in (PROJ_TM, GDN_TC, ATT_TQ, OUT_TM):
        assert s % tile == 0, (s, tile)
    assert d == 2 * HEADW and w_in.shape[-1] == W_COLS - LANES + 2 * GDN_HEADS
    c_pad = jnp.pad(c, ((0, 8 - b % 8 if b % 8 else 0), (0, 0)))
    pos_row = positions[:, None, :]
    for l in range(norm_g.shape[0]):
        lambda_init = 0.8 - 0.6 * math.exp(-0.3 * l)
        x = _layer(x, c_pad, pos_row, lambda_init, norm_g[l], w_ada, l, b_ada[l], w_in[l], conv_w[l],
                   a_log[l], dt_bias[l], gdn_norm_g[l], q_norm_g[l], k_norm_g[l], lambda_q1[l],
                   lambda_k1[l], lambda_q2[l], lambda_k2[l], subln_g[l], w_out[l])
    return x
```

```python
import functools
import math

import jax
import jax.numpy as jnp
import numpy as np
from jax import lax
from jax.experimental import pallas as pl
from jax.experimental.pallas import tpu as pltpu

F32 = jnp.float32
BF16 = jnp.bfloat16
HIGHEST = lax.Precision.HIGHEST

CHUNK = 64
PAIR = 2 * CHUNK
GDN_HEADS = 4
GDN_DK = 128
GDN_DV = 128
CONV_K = 4
DIFF_HEADS = 4
DIFF_D = 64
ROT_DIM = 16
ROPE_THETA = 500000.0
NORM_EPS = 1e-6
LANES = 128
NEG_BIG = -1e30
MASK_BIG = 256.0

GDN_QKV = 3 * GDN_HEADS * GDN_DK
HEADW = 512
OFF_GQKV = 0
OFF_AGATE = GDN_QKV
OFF_BQ = OFF_AGATE + HEADW
OFF_BK = OFF_BQ + HEADW
OFF_BV = OFF_BK + HEADW
OFF_BGATE = OFF_BV + HEADW
OFF_BD = OFF_BGATE + HEADW
W_COLS = OFF_BD + LANES

PROJ_TM = 512
PROJ_RB = 128
GMAT_W = 256
GDN_TC = 256
ATT_TQ = 512
WRITES_PER_TRIP = 2
ATT_UNROLL = 16
MAX_STATIC_SHIFT = 60.0
OUT_TM = 1024
VMEM_LIMIT = 48 * 1024 * 1024


def _silu(x):
    h = 0.5 * x
    return h + h * jnp.tanh(h)


def _dot(a, b, **kw):
    return jnp.dot(a, b, preferred_element_type=F32, **kw)


def _dot_nt(a, b, **kw):
    return lax.dot_general(a, b, (((1,), (1,)), ((), ())), preferred_element_type=F32, **kw)


def _ada_kernel(c_ref, w_ref, b_ref, o_ref):
    o_ref[...] = _dot(_silu(c_ref[...]), w_ref[...], precision=HIGHEST) + b_ref[...]


def _ada(c_pad, w_ada_layers, layer, b_ada):
    rows, d = c_pad.shape
    n = w_ada_layers.shape[2]
    tn = 512
    return pl.pallas_call(
        _ada_kernel,
        grid=(n // tn,),
        in_specs=[pl.BlockSpec((rows, d), lambda j: (0, 0)),
                  pl.BlockSpec((None, d, tn), lambda j: (layer, 0, j)),
                  pl.BlockSpec((1, tn), lambda j: (0, j))],
        out_specs=pl.BlockSpec((rows, tn), lambda j: (0, j)),
        out_shape=jax.ShapeDtypeStruct((rows, n), F32),
        name="ada",
    )(c_pad, w_ada_layers, b_ada)


def _proj_kernel(x_ref, pos_ref, shift_ref, scale_ref, ng_ref, w_ref, convw_ref, alog_ref, dtb_ref,
                 qg_ref, kg_ref, lmat_ref, gmat_ref, freq_ref, sgn_ref, idx_ref,
                 gq_ref, gk_ref, gv_ref, ag_ref, bq_ref, bk_ref, bv_ref, bgate_ref, bg_ref, bgt_ref,
                 cq, ck, cv, z0, z1, h_s, gm_s, tab):
    tm = x_ref.shape[0]
    x = x_ref[...]
    ms = jnp.mean(x * x, axis=-1, keepdims=True)
    gain = ng_ref[...] * (1.0 + scale_ref[...])
    h_s[...] = (x * lax.rsqrt(ms + NORM_EPS) * gain + shift_ref[...]).astype(BF16)

    cbufs = (cq, ck, cv)
    zbuf = (z0, z1)

    @pl.when(pl.program_id(1) == 0)
    def _():
        for cb in cbufs:
            cb[0:8, :] = jnp.zeros((8, HEADW), F32)

    def mm_pieces(dst, row0, off, width):
        def piece(c0):
            c1 = min(c0 + GMAT_W, width)
            dst[row0:row0 + tm, c0:c1] = _dot(h_s[...], w_ref[:, off + c0:off + c1])
        return [functools.partial(piece, c0) for c0 in range(0, width, GMAT_W)]

    def conv_pieces(g, out_ref):
        cb = cbufs[g]

        def piece(hh):
            hs = slice(hh * LANES, (hh + 1) * LANES)
            cs = slice(g * HEADW + hh * LANES, g * HEADW + (hh + 1) * LANES)
            for r0 in range(0, tm, PROJ_RB):
                acc = convw_ref[3:4, cs] * cb[8 + r0:8 + r0 + PROJ_RB, hs]
                for j in range(CONV_K - 1):
                    acc = acc + convw_ref[j:j + 1, cs] * cb[5 + j + r0:5 + j + r0 + PROJ_RB, hs]
                a = _silu(acc)
                if g < 2:
                    a = a * lax.rsqrt(jnp.sum(a * a, axis=-1, keepdims=True) + NORM_EPS)
                    if g == 0:
                        a = a * (GDN_DK ** -0.5)
                out_ref[r0:r0 + PROJ_RB, hs] = a.astype(BF16)
            cb[0:8, hs] = cb[tm:tm + 8, hs]
        return [functools.partial(piece, hh) for hh in range(GDN_HEADS)]

    def gate_pieces(slot, out_ref):
        def piece(r0):
            rs = slice(r0, r0 + PROJ_RB)
            out_ref[rs, :] = _silu(zbuf[slot][rs, :]).astype(BF16)
        return [functools.partial(piece, r0) for r0 in range(0, tm, PROJ_RB)]

    def qk_pieces(slot, g_ref, out_ref, post):
        def piece(c0):
            z = zbuf[slot][:, c0:c0 + GMAT_W]
            gm_s[:, c0:c0 + GMAT_W] = _dot((z * z).astype(BF16), gmat_ref[...])
            for r0 in range(0, tm, PROJ_RB):
                rs = slice(r0, r0 + PROJ_RB)
                for h0 in range(c0, c0 + GMAT_W, LANES):
                    hs = slice(h0, h0 + LANES)
                    ys = zbuf[slot][rs, hs] * lax.rsqrt(gm_s[rs, hs] + NORM_EPS) * g_ref[:, hs]
                    r = (ys * tab[0, rs, :] + pltpu.roll(ys, LANES - ROT_DIM // 2, axis=1) * tab[1, rs, :]
                         + pltpu.roll(ys, ROT_DIM // 2, axis=1) * tab[2, rs, :])
                    if post is not None:
                        r = r * post
                    out_ref[rs, hs] = r.astype(BF16)
        return [functools.partial(piece, c0) for c0 in range(0, HEADW, GMAT_W)]

    def table_pieces():
        def piece(r0):
            rs = slice(r0, r0 + LANES)
            ang = freq_ref[...] * pos_ref[:, rs].astype(F32)
            pad = jnp.zeros((LANES - ang.shape[0], LANES), F32)
            idx = jnp.broadcast_to(idx_ref[...], (LANES, LANES))
            cos_t = jnp.take_along_axis(jnp.concatenate([jnp.cos(ang), pad], axis=0).T, idx, axis=1)
            sin_t = jnp.take_along_axis(jnp.concatenate([jnp.sin(ang), pad], axis=0).T, idx, axis=1)
            tab[0, rs, :] = cos_t
            tab[1, rs, :] = sin_t * sgn_ref[0:1, :]
            tab[2, rs, :] = sin_t * sgn_ref[1:2, :]
        return [functools.partial(piece, r0) for r0 in range(0, tm, LANES)]

    def ep_bd(slot):
        z = zbuf[slot][:, 0:LANES]
        lane = lax.broadcasted_iota(jnp.int32, (1, LANES), 1)
        beta = 1.0 / (1.0 + jnp.exp(-z))
        zz = z + dtb_ref[...]
        softplus = jnp.maximum(zz, 0.0) + jnp.log(1.0 + jnp.exp(-jnp.abs(zz)))
        is_g = (lane >= GDN_HEADS) & (lane < 2 * GDN_HEADS)
        g = jnp.where(is_g, -jnp.exp(alog_ref[...]) * softplus, 0.0)
        g_hi = g.astype(BF16)
        g_lo = (g - g_hi.astype(F32)).astype(BF16)
        gc = jnp.concatenate(
            [_dot(lmat_ref[...], g_hi[r0:r0 + GMAT_W]) + _dot(lmat_ref[...], g_lo[r0:r0 + GMAT_W])
             for r0 in range(0, tm, GMAT_W)], axis=0)
        bg = jnp.where(lane < GDN_HEADS, beta, gc)
        bg_ref[...] = bg
        bgt_ref[...] = bg.T[0:8, :]

    def cast_bv():
        bv_ref[...] = zbuf[0][...].astype(BF16)

    def emit(*lists):
        for k in range(max(len(pl_) for pl_ in lists)):
            for pl_ in lists:
                if k < len(pl_):
                    pl_[k]()

    q_scale = DIFF_D ** -0.5 * math.log2(math.e)
    stages = [
        (mm_pieces(cq, 8, OFF_GQKV, HEADW), conv_pieces(0, gq_ref)),
        (mm_pieces(z1, 0, OFF_BD, LANES) + mm_pieces(ck, 8, OFF_GQKV + HEADW, HEADW),
         conv_pieces(1, gk_ref) + [functools.partial(ep_bd, 1)]),
        (mm_pieces(cv, 8, OFF_GQKV + 2 * HEADW, HEADW), conv_pieces(2, gv_ref)),
        (mm_pieces(z0, 0, OFF_BQ, HEADW), qk_pieces(0, qg_ref, bq_ref, q_scale)),
        (mm_pieces(z1, 0, OFF_BK, HEADW), qk_pieces(1, kg_ref, bk_ref, None)),
        (mm_pieces(z0, 0, OFF_AGATE, HEADW), gate_pieces(0, ag_ref)),
        (mm_pieces(z1, 0, OFF_BGATE, HEADW), gate_pieces(1, bgate_ref)),
        (mm_pieces(z0, 0, OFF_BV, HEADW), [cast_bv]),
    ]
    emit(stages[0][0], table_pieces())
    for i, (_, epilogue) in enumerate(stages):
        emit(stages[i + 1][0] if i + 1 < len(stages) else [], epilogue)


def _proj(x, pos_row, shift, scale, norm_g, w_all, conv_w, alog_row, dtb_row, qg_row, kg_row,
          lmat, gmat, freq_rows, sgn_rows, idx_row):
    b, s, d = x.shape
    tm = PROJ_TM
    row = lambda bi, si: (bi, si, 0)
    const2 = lambda bi, si: (0, 0)
    per_b = lambda bi, si: (bi, 0, 0)
    hw = jax.ShapeDtypeStruct((b, s, HEADW), BF16)
    out_shape = [hw] * 8 + [jax.ShapeDtypeStruct((b, s, LANES), F32),
                            jax.ShapeDtypeStruct((b, 8, s), F32)]
    hw_spec = pl.BlockSpec((None, tm, HEADW), row)
    out_specs = [hw_spec] * 8 + [pl.BlockSpec((None, tm, LANES), row),
                                 pl.BlockSpec((None, 8, tm), lambda bi, si: (bi, 0, si))]
    in_specs = [
        pl.BlockSpec((None, tm, d), row),
        pl.BlockSpec((None, 1, tm), lambda bi, si: (bi, 0, si)),
        pl.BlockSpec((None, 1, d), per_b),
        pl.BlockSpec((None, 1, d), per_b),
        pl.BlockSpec((1, d), const2),
        pl.BlockSpec((d, W_COLS), const2),
        pl.BlockSpec((CONV_K, GDN_QKV), const2),
        pl.BlockSpec((1, LANES), const2),
        pl.BlockSpec((1, LANES), const2),
        pl.BlockSpec((1, HEADW), const2),
        pl.BlockSpec((1, HEADW), const2),
        pl.BlockSpec((GMAT_W, GMAT_W), const2),
        pl.BlockSpec((GMAT_W, GMAT_W), const2),
        pl.BlockSpec((2 * (ROT_DIM // 2), LANES), const2),
        pl.BlockSpec((2, LANES), const2),
        pl.BlockSpec((1, LANES), const2),
    ]
    return pl.pallas_call(
        _proj_kernel,
        grid=(b, s // tm),
        in_specs=in_specs,
        out_specs=out_specs,
        out_shape=out_shape,
        scratch_shapes=[pltpu.VMEM((tm + 8, HEADW), F32)] * 3 + [pltpu.VMEM((tm, HEADW), F32)] * 2
        + [pltpu.VMEM((tm, d), BF16), pltpu.VMEM((tm, HEADW), F32), pltpu.VMEM((3, tm, LANES), F32)],
        compiler_params=pltpu.CompilerParams(
            dimension_semantics=("arbitrary", "arbitrary"), vmem_limit_bytes=VMEM_LIMIT),
        name="proj",
    )(x, pos_row, shift, scale, norm_g, w_all, conv_w, alog_row, dtb_row, qg_row, kg_row,
      lmat, gmat, freq_rows, sgn_rows, idx_row)


def _split(a):
    hi = a.astype(BF16)
    return hi, (a - hi.astype(F32)).astype(BF16)


def _gdn_kernel(q_ref, k_ref, v_ref, gate_ref, bg_ref, bgt_ref, ng_ref, o_ref,
                state, wq_s, u_s, kva_s, vn_s):
    nb, tc = q_ref.shape[0], q_ref.shape[1]
    npair = tc // PAIR
    nc = tc // CHUNK

    @pl.when(pl.program_id(0) == 0)
    def _():
        state[...] = jnp.zeros(state.shape, F32)
        vn_s[...] = jnp.zeros(vn_s.shape, BF16)

    ii = lax.broadcasted_iota(jnp.int32, (CHUNK, PAIR), 0)
    lane_w = lax.broadcasted_iota(jnp.int32, (CHUNK, PAIR), 1)
    jj = lane_w % CHUNK
    left_w = lane_w < CHUNK
    left_p = lax.broadcasted_iota(jnp.int32, (PAIR, PAIR), 1) < CHUNK
    top_col = lax.broadcasted_iota(jnp.int32, (PAIR, 1), 0) < CHUNK
    eye = (ii == jj).astype(F32)
    lower = ii >= jj
    strict = ii > jj

    def widen(m):
        return jnp.where(left_w, m[0:CHUNK], m[CHUNK:PAIR])

    def bdiag(wd):
        z = jnp.zeros_like(wd)
        return jnp.concatenate([jnp.where(left_w, wd, z), jnp.where(left_w, z, wd)], axis=0)

    def mm(parts, rhs):
        n = len(parts)
        b_hi, b_lo = bdiag(rhs[0]), bdiag(rhs[1])
        his = [hi_ for hi_, _ in parts]
        r_hi = _dot(jnp.concatenate(his + [lo_ for _, lo_ in parts], axis=0), b_hi)
        r_lo = _dot(jnp.concatenate(his, axis=0) if n > 1 else his[0], b_lo)
        blk = lambda r, i: r[i * CHUNK:(i + 1) * CHUNK]
        return [blk(r_hi, i) + blk(r_hi, n + i) + blk(r_lo, i) for i in range(n)]

    items = [(bb, hh, p) for bb in range(nb) for hh in range(GDN_HEADS) for p in range(npair)]
    pre = []
    for bb, hh, p in items:
        cs = slice(hh * LANES, (hh + 1) * LANES)
        rs = slice(p * PAIR, (p + 1) * PAIR)
        q2, k2 = q_ref[bb, rs, cs], k_ref[bb, rs, cs]
        gcol2 = bg_ref[bb, rs, GDN_HEADS + hh:GDN_HEADS + hh + 1]
        bcol_w = widen(jnp.broadcast_to(bg_ref[bb, rs, hh:hh + 1], (PAIR, PAIR)))
        gcol_w = widen(jnp.broadcast_to(gcol2, (PAIR, PAIR)))
        grow = bgt_ref[bb, GDN_HEADS + hh:GDN_HEADS + hh + 1, rs]
        dec = jnp.where(lower, jnp.exp(jnp.minimum(gcol_w - grow, 0.0)), 0.0)
        gram = _dot_nt(jnp.concatenate([k2, q2], axis=0), k2)
        xm = jnp.where(strict, -(bcol_w * widen(gram[0:PAIR]) * dec), 0.0)
        a_w = jnp.where(lower, widen(gram[PAIR:2 * PAIR]) * dec, 0.0)
        pre.append((xm, a_w))

    def decayed_operands(item, a_w):
        bb, hh, p = item
        cs = slice(hh * LANES, (hh + 1) * LANES)
        rs = slice(p * PAIR, (p + 1) * PAIR)
        q2, k2 = q_ref[bb, rs, cs], k_ref[bb, rs, cs]
        gcol2 = bg_ref[bb, rs, GDN_HEADS + hh:GDN_HEADS + hh + 1]
        grow = bgt_ref[bb, GDN_HEADS + hh:GDN_HEADS + hh + 1, rs]
        g_end = jnp.where(top_col, grow[:, CHUNK - 1:CHUNK], grow[:, PAIR - 1:PAIR])
        e_g = jnp.exp(gcol2)
        k2f = k2.astype(F32)
        qd = (q2.astype(F32) * e_g).astype(BF16)
        kdt = (k2f * jnp.exp(g_end - gcol2)).T
        a_bd = bdiag(a_w)
        kva_s[bb, hh, 2 * p] = jnp.concatenate(
            [jnp.where(left_p, kdt, 0.0), a_bd[0:CHUNK]], axis=0).astype(BF16)
        kva_s[bb, hh, 2 * p + 1] = jnp.concatenate(
            [jnp.where(left_p, 0.0, kdt), a_bd[CHUNK:PAIR]], axis=0).astype(BF16)
        return qd, (k2f * e_g).astype(BF16)

    levels = 4
    per_level = -(-len(items) // levels)
    side = []
    ts = [eye + xm for xm, _ in pre]
    xs = [_split(xm) for xm, _ in pre]
    ps = [_split(mm([xp], xp)[0]) for xp in xs]
    for lvl in range(levels):
        both = [mm([_split(t), pp], pp) for t, pp in zip(ts, ps)]
        for idx in range(lvl * per_level, min((lvl + 1) * per_level, len(items))):
            side.append(decayed_operands(items[idx], pre[idx][1]))
        ts = [t + tp for t, (tp, _) in zip(ts, both)]
        ps = [_split(sq) for _, sq in both]
    ts = [t + mm([_split(t)], pp)[0] for t, pp in zip(ts, ps)]
    for (bb, hh, p), t, (qd, kg) in zip(items, ts, side):
        cs = slice(hh * LANES, (hh + 1) * LANES)
        rs = slice(p * PAIR, (p + 1) * PAIR)
        tb = bdiag(t * bgt_ref[bb, hh:hh + 1, rs]).astype(BF16)
        uw = _dot(tb, jnp.concatenate([v_ref[bb, rs, cs], kg], axis=1))
        u_s[bb, hh, rs, :] = uw[:, 0:GDN_DV]
        w2 = uw[:, GDN_DV:GDN_DV + GDN_DK].astype(BF16)
        for e in range(2):
            es = slice(e * CHUNK, (e + 1) * CHUNK)
            wq_s[bb, hh, 2 * p + e] = jnp.concatenate([w2[es], qd[es]], axis=0)

    chains = [(bb, hh) for bb in range(nb) for hh in range(GDN_HEADS)]
    for c in range(nc):
        e = c % 2
        rs = slice(c * CHUNK, (c + 1) * CHUNK)
        es = slice(e * CHUNK, (e + 1) * CHUNK)
        sts = [state[ch] for ch in chains]
        rr = [_dot(wq_s[bb, hh, c], st.astype(BF16)) for (bb, hh), st in zip(chains, sts)]
        for (bb, hh), r in zip(chains, rr):
            vn_s[bb, hh, es, :] = (u_s[bb, hh, rs, :] - r[0:CHUNK]).astype(BF16)
        for (bb, hh), st, r in zip(chains, sts, rr):
            cs = slice(hh * LANES, (hh + 1) * LANES)
            g_end = bgt_ref[bb, GDN_HEADS + hh:GDN_HEADS + hh + 1, (c + 1) * CHUNK - 1:(c + 1) * CHUNK]
            kva = _dot(kva_s[bb, hh, c], vn_s[bb, hh])
            state[bb, hh] = st * jnp.exp(g_end) + kva[0:GDN_DK]
            o = r[CHUNK:PAIR] + kva[GDN_DK:GDN_DK + CHUNK]
            on = o * lax.rsqrt(jnp.mean(o * o, axis=-1, keepdims=True) + NORM_EPS) * ng_ref[...]
            o_ref[bb, rs, cs] = (on * gate_ref[bb, rs, cs].astype(F32)).astype(BF16)


def _gdn(gq, gk, gv, agate, bg, bgt, ng_row):
    b, s, _ = gq.shape
    tc = GDN_TC
    row = lambda si: (0, si, 0)
    hw_spec = pl.BlockSpec((b, tc, HEADW), row)
    per_chain = (b, GDN_HEADS)
    return pl.pallas_call(
        _gdn_kernel,
        grid=(s // tc,),
        in_specs=[hw_spec, hw_spec, hw_spec, hw_spec,
                  pl.BlockSpec((b, tc, LANES), row),
                  pl.BlockSpec((b, 8, tc), lambda si: (0, 0, si)),
                  pl.BlockSpec((1, GDN_DV), lambda si: (0, 0))],
        out_specs=hw_spec,
        out_shape=jax.ShapeDtypeStruct((b, s, HEADW), BF16),
        scratch_shapes=[pltpu.VMEM(per_chain + (GDN_DK, GDN_DV), F32),
                        pltpu.VMEM(per_chain + (tc // CHUNK, PAIR, GDN_DK), BF16),
                        pltpu.VMEM(per_chain + (tc, GDN_DV), F32),
                        pltpu.VMEM(per_chain + (tc // CHUNK, GDN_DK + CHUNK, PAIR), BF16),
                        pltpu.VMEM(per_chain + (PAIR, GDN_DV), BF16)],
        compiler_params=pltpu.CompilerParams(
            dimension_semantics=("arbitrary",), vmem_limit_bytes=VMEM_LIMIT),
        name="gdn",
    )(gq, gk, gv, agate, bg, bgt, ng_row)


def _attn_kernel(q_ref, k_ref, v_ref, gate_ref, lamp_ref, sg_ref, gqk_ref, o_ref,
                 vext, acc, pbuf, kaug_s, *, lambda_init, tq):
    s_len = q_ref.shape[0]
    nq = s_len // tq
    tk = tq
    nch = tq // CHUNK
    w = 2 * DIFF_D

    vext[:, 0:w] = v_ref[...]
    vext[:, w:2 * w] = jnp.ones((s_len, w), BF16)

    @pl.when(jnp.logical_and(pl.program_id(0) == 0, pl.program_id(1) == 0))
    def _():
        acc[...] = jnp.zeros(acc.shape, F32)

    lane = lax.broadcasted_iota(jnp.int32, (tk, w), 1)
    rchunk = lax.broadcasted_iota(jnp.int32, (tk, w), 0) // CHUNK
    lo = lane < DIFF_D
    hi = jnp.logical_not(lo)
    half = (lo, hi)

    gqk = gqk_ref[...]
    bound = (jnp.max(jnp.abs(gqk[0:1, :])) * jnp.max(jnp.abs(gqk[1:2, :]))
             * (DIFF_D * DIFF_D ** -0.5 * math.log2(math.e) * 1.01))

    lp = lamp_ref[...]
    lam = (jnp.exp(jnp.sum(lp[0:1, :] * lp[1:2, :], axis=-1, keepdims=True))
           - jnp.exp(jnp.sum(lp[2:3, :] * lp[3:4, :], axis=-1, keepdims=True)) + lambda_init)

    def rows_of(blk):
        return pl.ds(pl.multiple_of(blk * tq, tq), tq)

    def write_rows(qi, a0, a1):
        o = a0[:, 0:w] / a0[:, w:2 * w] - lam * (a1[:, 0:w] / a1[:, w:2 * w])
        on = o * lax.rsqrt(jnp.mean(o * o, axis=-1, keepdims=True) + NORM_EPS) * sg_ref[...]
        on = on * (1.0 - lambda_init)
        o_ref[rows_of(qi), :] = (on * gate_ref[rows_of(qi), :].astype(F32)).astype(BF16)


    @pl.when(bound <= MAX_STATIC_SHIFT)
    def _():
        q_aug = []
        for sub, base in enumerate((DIFF_D, 0)):
            mlane = lane - (base + 1)
            in_mask = (mlane >= 0) & (mlane < nch)
            q_aug.append(jnp.where((lane == base) | (in_mask & (rchunk == mlane)), 1.0, 0.0).astype(BF16))
            plain = jnp.where(lane == base, -bound, 0.0)
            kaug_s[0, sub] = plain.astype(BF16)
            kaug_s[1, sub] = jnp.where(in_mask & (rchunk > mlane), -MASK_BIG, plain).astype(BF16)

        def scores(qi, j, pslot):
            q = q_ref[rows_of(qi), :]
            kb = k_ref[rows_of(j), :]
            is_diag = jnp.asarray(j == qi, jnp.int32)
            for sub in range(2):
                qs = jnp.where(half[sub], q, q_aug[sub])
                ks = jnp.where(half[sub], kb, kaug_s[is_diag, sub])
                pbuf[pslot, sub] = jnp.exp2(_dot_nt(qs, ks)).astype(BF16)

        def accumulate(qi, j, pslot):
            ve = vext[rows_of(j), :]
            keep = jnp.where(j == 0, 0.0, 1.0)
            for sub in range(2):
                acc[qi, sub] = acc[qi, sub] * keep + _dot(pbuf[pslot, sub], ve)

        def advance(st):
            qi, j, pqi, pj, n = st
            accumulate(pqi, pj, n & 1)
            scores(qi, j, (n + 1) & 1)
            wrap = j >= qi
            return (jnp.where(wrap, qi + 1, qi), jnp.where(wrap, 0, j + 1), qi, j, n + 1)

        todo = nq * (nq + 1) // 2 - 1
        unroll = max([u for u in range(ATT_UNROLL // 2, 2 * ATT_UNROLL) if todo % u == 0] or [ATT_UNROLL],
                     key=lambda u: -abs(u - ATT_UNROLL))

        def trip(t, carry):
            st, done = carry
            complete = st[2]
            for u in range(WRITES_PER_TRIP):
                qw = jnp.clip(done + u, 0, jnp.maximum(complete - 1, 0))
                write_rows(qw, acc[qw, 0], acc[qw, 1])
            done = jnp.minimum(done + WRITES_PER_TRIP, complete)
            for _ in range(unroll):
                st = advance(st)
            return st, done

        scores(0, 0, 0)
        one, zero = jnp.int32(1), jnp.int32(0)
        trips, rest = divmod(todo, unroll)
        st, done = lax.fori_loop(0, trips, trip, ((one, zero, zero, zero, zero), zero))
        for _ in range(rest):
            st = advance(st)
        accumulate(st[2], st[3], st[4] & 1)

        def write_block(qi, carry):
            write_rows(qi, acc[qi, 0], acc[qi, 1])
            return carry

        lax.fori_loop(done, nq, write_block, 0)

    @pl.when(bound > MAX_STATIC_SHIFT)
    def _():
        zero = jnp.zeros((tk, w), BF16)
        rr = lax.broadcasted_iota(jnp.int32, (tq, tk), 0) // CHUNK
        cc = lax.broadcasted_iota(jnp.int32, (tq, tk), 1) // CHUNK
        causal = cc <= rr

        def outer(qi, carry):
            q = q_ref[rows_of(qi), :]

            def step(j, ms, diag):
                kb = k_ref[rows_of(j), :]
                ve = vext[rows_of(j), :]
                new = []
                for sub in range(2):
                    s = _dot_nt(q, jnp.where(half[sub], kb, zero))
                    if diag:
                        s = jnp.where(causal, s, NEG_BIG)
                    m_new = jnp.maximum(ms[sub], jnp.max(s, axis=-1, keepdims=True))
                    p = jnp.exp2(s - m_new)
                    acc[qi, sub] = jnp.exp2(ms[sub] - m_new) * acc[qi, sub] + _dot(p.astype(BF16), ve)
                    new.append(m_new)
                return tuple(new)

            init = (jnp.full((tq, 1), NEG_BIG, F32), jnp.full((tq, 1), NEG_BIG, F32))
            ms = lax.fori_loop(0, qi, lambda j, c: step(j, c, False), init)
            step(qi, ms, True)
            write_rows(qi, acc[qi, 0], acc[qi, 1])
            return carry

        lax.fori_loop(0, nq, outer, 0)


def _attn(bq, bk, bv, bgate, lam_params, sg_row, gqk_rows, lambda_init):
    b, s, _ = bq.shape
    tq = ATT_TQ
    w = 2 * DIFF_D
    hspec = pl.BlockSpec((None, s, w), lambda bi, hi: (bi, 0, hi))
    const = lambda bi, hi: (0, 0)
    return pl.pallas_call(
        functools.partial(_attn_kernel, lambda_init=lambda_init, tq=tq),
        grid=(b, DIFF_HEADS),
        in_specs=[hspec, hspec, hspec, hspec,
                  pl.BlockSpec((8, LANES), const),
                  pl.BlockSpec((1, w), const),
                  pl.BlockSpec((2, LANES), const)],
        out_specs=hspec,
        out_shape=jax.ShapeDtypeStruct((b, s, HEADW), BF16),
        scratch_shapes=[pltpu.VMEM((s, 2 * w), BF16),
                        pltpu.VMEM((s // tq, 2, tq, 2 * w), F32),
                        pltpu.VMEM((2, 2, tq, tq), BF16),
                        pltpu.VMEM((2, 2, tq, w), BF16)],
        compiler_params=pltpu.CompilerParams(
            dimension_semantics=("arbitrary", "arbitrary"), vmem_limit_bytes=VMEM_LIMIT),
        name="attn",
    )(bq, bk, bv, bgate, lam_params, sg_row, gqk_rows)


def _out_kernel(x_ref, ma_ref, mb_ref, wa_ref, wb_ref, gate_ref, o_ref):
    y = _dot(ma_ref[...], wa_ref[...]) + _dot(mb_ref[...], wb_ref[...])
    o_ref[...] = x_ref[...] + gate_ref[...] * y


def _out(x, mixed_a, mixed_b, w_a, w_b, gate):
    b, s, d = x.shape
    tm = OUT_TM
    row = lambda bi, si: (bi, si, 0)
    const2 = lambda bi, si: (0, 0)
    return pl.pallas_call(
        _out_kernel,
        grid=(b, s // tm),
        in_specs=[pl.BlockSpec((None, tm, d), row),
                  pl.BlockSpec((None, tm, HEADW), row),
                  pl.BlockSpec((None, tm, HEADW), row),
                  pl.BlockSpec((HEADW, d), const2),
                  pl.BlockSpec((HEADW, d), const2),
                  pl.BlockSpec((None, 1, d), lambda bi, si: (bi, 0, 0))],
        out_specs=pl.BlockSpec((None, tm, d), row),
        out_shape=jax.ShapeDtypeStruct((b, s, d), F32),
        compiler_params=pltpu.CompilerParams(
            dimension_semantics=("arbitrary", "arbitrary"), vmem_limit_bytes=VMEM_LIMIT),
        name="out",
    )(x, mixed_a, mixed_b, w_a, w_b, gate)


def _constants():
    sub = np.arange(LANES) % DIFF_D
    half = ROT_DIM // 2
    inv_freq = ROPE_THETA ** (-np.arange(0, ROT_DIM, 2, dtype=np.float32) / ROT_DIM)
    freq_rows = np.zeros((2 * half, LANES), np.float32)
    freq_rows[:half] = inv_freq[:, None]
    idx_row = np.where(sub < ROT_DIM, sub % half, half).astype(np.int32)[None, :]
    sgn_rows = np.stack([np.where(sub < half, -1.0, 0.0),
                         np.where((sub >= half) & (sub < ROT_DIM), 1.0, 0.0)]).astype(np.float32)
    r = np.arange(GMAT_W)
    lmat = (r[:, None] // CHUNK == r[None, :] // CHUNK) & (r[None, :] <= r[:, None])
    g = np.arange(GMAT_W)
    gmat = np.where(g[:, None] // DIFF_D == g[None, :] // DIFF_D, 1.0 / DIFF_D, 0.0)
    return (jnp.asarray(freq_rows), jnp.asarray(sgn_rows), jnp.asarray(idx_row),
            jnp.asarray(lmat, BF16), jnp.asarray(gmat, BF16))


def _pad_lanes(v, offset):
    return jnp.zeros((1, LANES), F32).at[0, offset:offset + v.shape[0]].set(v.astype(F32))


def _layer(x, c_pad, pos_row, lambda_init, norm_g, w_ada_layers, layer, b_ada, w_in, conv_w, a_log, dt_bias, gdn_norm_g,
           q_norm_g, k_norm_g, lam_q1, lam_k1, lam_q2, lam_k2, subln_g, w_out):
    b, s, d = x.shape
    mod = _ada(c_pad, w_ada_layers, layer, b_ada[None, :])[:b]
    shift = mod[:, None, 0:d]
    scale = mod[:, None, d:2 * d]
    gate = mod[:, None, 2 * d:3 * d]

    nk = GDN_HEADS * GDN_DK
    o_beta = 2 * nk + GDN_HEADS * GDN_DV
    o_agate = o_beta + 2 * GDN_HEADS
    assert (OFF_AGATE, OFF_BD) == (o_beta, o_beta + w_in.shape[1] - o_agate)
    bd = jnp.pad(w_in[:, o_beta:o_agate], ((0, 0), (0, LANES - 2 * GDN_HEADS)))
    w_all = jnp.concatenate([w_in[:, 0:o_beta], w_in[:, o_agate:], bd], axis=1).astype(BF16)
    freq_rows, sgn_rows, idx_row, lmat, gmat = _constants()
    lam_params = jnp.zeros((8, LANES), F32)
    for r, v in enumerate((lam_q1, lam_k1, lam_q2, lam_k2)):
        lam_params = lam_params.at[r, 0:DIFF_D].set(v.astype(F32))

    gq, gk, gv, agate, bq, bk, bv, bgate, bg, bgt = _proj(
        x, pos_row, shift, scale, norm_g[None, :], w_all, conv_w,
        _pad_lanes(a_log, GDN_HEADS), _pad_lanes(dt_bias, GDN_HEADS),
        jnp.tile(q_norm_g, 2 * DIFF_HEADS)[None, :], jnp.tile(k_norm_g, 2 * DIFF_HEADS)[None, :],
        lmat, gmat, freq_rows, sgn_rows, idx_row)
    mixed_a = _gdn(gq, gk, gv, agate, bg, bgt, gdn_norm_g[None, :])
    gqk_rows = jnp.concatenate([_pad_lanes(q_norm_g, 0), _pad_lanes(k_norm_g, 0)], axis=0)
    mixed_b = _attn(bq, bk, bv, bgate, lam_params, subln_g[None, :], gqk_rows, lambda_init)
    w_o = w_out.astype(BF16)
    return _out(x, mixed_a, mixed_b, w_o[:HEADW], w_o[HEADW:], gate)


def kernel(x, c, positions, norm_g, w_ada, b_ada, w_in, conv_w, a_log, dt_bias, gdn_norm_g, q_norm_g,
           k_norm_g, lambda_q1, lambda_k1, lambda_q2, lambda_k2, subln_g, w_out):
    b, s, d = x.shape
    for tile in (PROJ_TM, GDN_TC, ATT_TQ, OUT_TM):
        assert s % tile == 0, (s, tile)
    assert d == 2 * HEADW and w_in.shape[-1] == W_COLS - LANES + 2 * GDN_HEADS
    c_pad = jnp.pad(c, ((0, 8 - b % 8 if b % 8 else 0), (0, 0)))
    pos_row = positions[:, None, :]
    for l in range(norm_g.shape[0]):
        lambda_init = 0.8 - 0.6 * math.exp(-0.3 * l)
        x = _layer(x, c_pad, pos_row, lambda_init, norm_g[l], w_ada, l, b_ada[l], w_in[l], conv_w[l],
                   a_log[l], dt_bias[l], gdn_norm_g[l], q_norm_g[l], k_norm_g[l], lambda_q1[l],
                   lambda_k1[l], lambda_q2[l], lambda_k2[l], subln_g[l], w_out[l])
    return x
```

```python
import functools
import math

import jax
import jax.numpy as jnp
import numpy as np
from jax import lax
from jax.experimental import pallas as pl
from jax.experimental.pallas import tpu as pltpu

F32 = jnp.float32
BF16 = jnp.bfloat16

CHUNK = 64
PAIR = 2 * CHUNK
GDN_HEADS = 4
GDN_DK = 128
GDN_DV = 128
CONV_K = 4
DIFF_HEADS = 4
DIFF_D = 64
ROT_DIM = 16
ROPE_THETA = 500000.0
NORM_EPS = 1e-6
LANES = 128
NEG_BIG = -1e30
MASK_BIG = 256.0

GDN_QKV = 3 * GDN_HEADS * GDN_DK
HEADW = 512
OFF_GQKV = 0
OFF_AGATE = GDN_QKV
OFF_BQ = OFF_AGATE + HEADW
OFF_BK = OFF_BQ + HEADW
OFF_BV = OFF_BK + HEADW
OFF_BGATE = OFF_BV + HEADW
OFF_BD = OFF_BGATE + HEADW
W_COLS = OFF_BD + LANES

PROJ_TM = 512
PROJ_RB = 128
GMAT_W = 256
GDN_TC = 256
ATT_TQ = 512
WRITES_PER_TRIP = 2
ATT_UNROLL = 16
MAX_STATIC_SHIFT = 60.0
OUT_TM = 1024
VMEM_LIMIT = 48 * 1024 * 1024


def _silu(x):
    h = 0.5 * x
    return h + h * jnp.tanh(h)


def _dot(a, b, **kw):
    return jnp.dot(a, b, preferred_element_type=F32, **kw)


def _dot_nt(a, b, **kw):
    return lax.dot_general(a, b, (((1,), (1,)), ((), ())), preferred_element_type=F32, **kw)


def _split(a):
    hi = a.astype(BF16)
    return hi, (a - hi.astype(F32)).astype(BF16)


def _ada_kernel(c_ref, w_ref, b_ref, o_ref):
    a_hi, a_lo = _split(_silu(c_ref[...]))
    w_hi, w_lo = _split(w_ref[...])
    o_ref[...] = _dot(a_hi, w_hi) + _dot(a_hi, w_lo) + _dot(a_lo, w_hi) + b_ref[...]


def _ada(c_pad, w_ada_layers, layer, b_ada):
    rows, d = c_pad.shape
    n = w_ada_layers.shape[2]
    tn = 512
    return pl.pallas_call(
        _ada_kernel,
        grid=(n // tn,),
        in_specs=[pl.BlockSpec((rows, d), lambda j: (0, 0)),
                  pl.BlockSpec((None, d, tn), lambda j: (layer, 0, j)),
                  pl.BlockSpec((1, tn), lambda j: (0, j))],
        out_specs=pl.BlockSpec((rows, tn), lambda j: (0, j)),
        out_shape=jax.ShapeDtypeStruct((rows, n), F32),
        name="ada",
    )(c_pad, w_ada_layers, b_ada)


def _proj_kernel(x_ref, pos_ref, shift_ref, scale_ref, ng_ref, w_ref, convw_ref, alog_ref, dtb_ref,
                 qg_ref, kg_ref, lmat_ref, gmat_ref, freq_ref, sgn_ref, idx_ref,
                 gq_ref, gk_ref, gv_ref, ag_ref, bq_ref, bk_ref, bv_ref, bgate_ref, bg_ref, bgt_ref,
                 cq, ck, cv, z0, z1, h_s, gm_s, tab):
    tm = x_ref.shape[0]
    x = x_ref[...]
    ms = jnp.mean(x * x, axis=-1, keepdims=True)
    gain = ng_ref[...] * (1.0 + scale_ref[...])
    h_s[...] = (x * lax.rsqrt(ms + NORM_EPS) * gain + shift_ref[...]).astype(BF16)

    cbufs = (cq, ck, cv)
    zbuf = (z0, z1)

    @pl.when(pl.program_id(1) == 0)
    def _():
        for cb in cbufs:
            cb[0:8, :] = jnp.zeros((8, HEADW), F32)

    def mm_pieces(dst, row0, off, width):
        def piece(c0):
            c1 = min(c0 + GMAT_W, width)
            dst[row0:row0 + tm, c0:c1] = _dot(h_s[...], w_ref[:, off + c0:off + c1])
        return [functools.partial(piece, c0) for c0 in range(0, width, GMAT_W)]

    def conv_pieces(g, out_ref):
        cb = cbufs[g]

        def piece(hh):
            hs = slice(hh * LANES, (hh + 1) * LANES)
            cs = slice(g * HEADW + hh * LANES, g * HEADW + (hh + 1) * LANES)
            for r0 in range(0, tm, PROJ_RB):
                acc = convw_ref[3:4, cs] * cb[8 + r0:8 + r0 + PROJ_RB, hs]
                for j in range(CONV_K - 1):
                    acc = acc + convw_ref[j:j + 1, cs] * cb[5 + j + r0:5 + j + r0 + PROJ_RB, hs]
                a = _silu(acc)
                if g < 2:
                    a = a * lax.rsqrt(jnp.sum(a * a, axis=-1, keepdims=True) + NORM_EPS)
                    if g == 0:
                        a = a * (GDN_DK ** -0.5)
                out_ref[r0:r0 + PROJ_RB, hs] = a.astype(BF16)
            cb[0:8, hs] = cb[tm:tm + 8, hs]
        return [functools.partial(piece, hh) for hh in range(GDN_HEADS)]

    def gate_pieces(slot, out_ref):
        def piece(r0):
            rs = slice(r0, r0 + PROJ_RB)
            out_ref[rs, :] = _silu(zbuf[slot][rs, :]).astype(BF16)
        return [functools.partial(piece, r0) for r0 in range(0, tm, PROJ_RB)]

    def qk_pieces(slot, g_ref, out_ref, post):
        def piece(c0):
            z = zbuf[slot][:, c0:c0 + GMAT_W]
            gm_s[:, c0:c0 + GMAT_W] = _dot((z * z).astype(BF16), gmat_ref[...])
            for r0 in range(0, tm, PROJ_RB):
                rs = slice(r0, r0 + PROJ_RB)
                for h0 in range(c0, c0 + GMAT_W, LANES):
                    hs = slice(h0, h0 + LANES)
                    ys = zbuf[slot][rs, hs] * lax.rsqrt(gm_s[rs, hs] + NORM_EPS) * g_ref[:, hs]
                    r = (ys * tab[0, rs, :] + pltpu.roll(ys, LANES - ROT_DIM // 2, axis=1) * tab[1, rs, :]
                         + pltpu.roll(ys, ROT_DIM // 2, axis=1) * tab[2, rs, :])
                    if post is not None:
                        r = r * post
                    out_ref[rs, hs] = r.astype(BF16)
        return [functools.partial(piece, c0) for c0 in range(0, HEADW, GMAT_W)]

    def table_pieces():
        def piece(r0):
            rs = slice(r0, r0 + LANES)
            ang = freq_ref[...] * pos_ref[:, rs].astype(F32)
            pad = jnp.zeros((LANES - ang.shape[0], LANES), F32)
            idx = jnp.broadcast_to(idx_ref[...], (LANES, LANES))
            cos_t = jnp.take_along_axis(jnp.concatenate([jnp.cos(ang), pad], axis=0).T, idx, axis=1)
            sin_t = jnp.take_along_axis(jnp.concatenate([jnp.sin(ang), pad], axis=0).T, idx, axis=1)
            tab[0, rs, :] = cos_t
            tab[1, rs, :] = sin_t * sgn_ref[0:1, :]
            tab[2, rs, :] = sin_t * sgn_ref[1:2, :]
        return [functools.partial(piece, r0) for r0 in range(0, tm, LANES)]

    def ep_bd(slot):
        z = zbuf[slot][:, 0:LANES]
        lane = lax.broadcasted_iota(jnp.int32, (1, LANES), 1)
        beta = 1.0 / (1.0 + jnp.exp(-z))
        zz = z + dtb_ref[...]
        softplus = jnp.maximum(zz, 0.0) + jnp.log(1.0 + jnp.exp(-jnp.abs(zz)))
        is_g = (lane >= GDN_HEADS) & (lane < 2 * GDN_HEADS)
        g = jnp.where(is_g, -jnp.exp(alog_ref[...]) * softplus, 0.0)
        g_hi = g.astype(BF16)
        g_lo = (g - g_hi.astype(F32)).astype(BF16)
        gc = jnp.concatenate(
            [_dot(lmat_ref[...], g_hi[r0:r0 + GMAT_W]) + _dot(lmat_ref[...], g_lo[r0:r0 + GMAT_W])
             for r0 in range(0, tm, GMAT_W)], axis=0)
        bg = jnp.where(lane < GDN_HEADS, beta, gc)
        bg_ref[...] = bg
        bgt_ref[...] = bg.T[0:8, :]

    def cast_bv():
        bv_ref[...] = zbuf[0][...].astype(BF16)

    def emit(*lists):
        for k in range(max(len(pl_) for pl_ in lists)):
            for pl_ in lists:
                if k < len(pl_):
                    pl_[k]()

    q_scale = DIFF_D ** -0.5 * math.log2(math.e)
    stages = [
        (mm_pieces(cq, 8, OFF_GQKV, HEADW), conv_pieces(0, gq_ref)),
        (mm_pieces(z1, 0, OFF_BD, LANES) + mm_pieces(ck, 8, OFF_GQKV + HEADW, HEADW),
         conv_pieces(1, gk_ref) + [functools.partial(ep_bd, 1)]),
        (mm_pieces(cv, 8, OFF_GQKV + 2 * HEADW, HEADW), conv_pieces(2, gv_ref)),
        (mm_pieces(z0, 0, OFF_BQ, HEADW), qk_pieces(0, qg_ref, bq_ref, q_scale)),
        (mm_pieces(z1, 0, OFF_BK, HEADW), qk_pieces(1, kg_ref, bk_ref, None)),
        (mm_pieces(z0, 0, OFF_AGATE, HEADW), gate_pieces(0, ag_ref)),
        (mm_pieces(z1, 0, OFF_BGATE, HEADW), gate_pieces(1, bgate_ref)),
        (mm_pieces(z0, 0, OFF_BV, HEADW), [cast_bv]),
    ]
    emit(stages[0][0], table_pieces())
    for i, (_, epilogue) in enumerate(stages):
        emit(stages[i + 1][0] if i + 1 < len(stages) else [], epilogue)


def _proj(x, pos_row, shift, scale, norm_g, w_all, conv_w, alog_row, dtb_row, qg_row, kg_row,
          lmat, gmat, freq_rows, sgn_rows, idx_row):
    b, s, d = x.shape
    tm = PROJ_TM
    row = lambda bi, si: (bi, si, 0)
    const2 = lambda bi, si: (0, 0)
    per_b = lambda bi, si: (bi, 0, 0)
    hw = jax.ShapeDtypeStruct((b, s, HEADW), BF16)
    out_shape = [hw] * 8 + [jax.ShapeDtypeStruct((b, s, LANES), F32),
                            jax.ShapeDtypeStruct((b, 8, s), F32)]
    hw_spec = pl.BlockSpec((None, tm, HEADW), row)
    out_specs = [hw_spec] * 8 + [pl.BlockSpec((None, tm, LANES), row),
                                 pl.BlockSpec((None, 8, tm), lambda bi, si: (bi, 0, si))]
    in_specs = [
        pl.BlockSpec((None, tm, d), row),
        pl.BlockSpec((None, 1, tm), lambda bi, si: (bi, 0, si)),
        pl.BlockSpec((None, 1, d), per_b),
        pl.BlockSpec((None, 1, d), per_b),
        pl.BlockSpec((1, d), const2),
        pl.BlockSpec((d, W_COLS), const2),
        pl.BlockSpec((CONV_K, GDN_QKV), const2),
        pl.BlockSpec((1, LANES), const2),
        pl.BlockSpec((1, LANES), const2),
        pl.BlockSpec((1, HEADW), const2),
        pl.BlockSpec((1, HEADW), const2),
        pl.BlockSpec((GMAT_W, GMAT_W), const2),
        pl.BlockSpec((GMAT_W, GMAT_W), const2),
        pl.BlockSpec((2 * (ROT_DIM // 2), LANES), const2),
        pl.BlockSpec((2, LANES), const2),
        pl.BlockSpec((1, LANES), const2),
    ]
    return pl.pallas_call(
        _proj_kernel,
        grid=(b, s // tm),
        in_specs=in_specs,
        out_specs=out_specs,
        out_shape=out_shape,
        scratch_shapes=[pltpu.VMEM((tm + 8, HEADW), F32)] * 3 + [pltpu.VMEM((tm, HEADW), F32)] * 2
        + [pltpu.VMEM((tm, d), BF16), pltpu.VMEM((tm, HEADW), F32), pltpu.VMEM((3, tm, LANES), F32)],
        compiler_params=pltpu.CompilerParams(
            dimension_semantics=("arbitrary", "arbitrary"), vmem_limit_bytes=VMEM_LIMIT),
        name="proj",
    )(x, pos_row, shift, scale, norm_g, w_all, conv_w, alog_row, dtb_row, qg_row, kg_row,
      lmat, gmat, freq_rows, sgn_rows, idx_row)


def _gdn_kernel(q_ref, k_ref, v_ref, gate_ref, bg_ref, bgt_ref, ng_ref, o_ref,
                state, wq_s, u_s, kva_s, vn_s):
    nb, tc = q_ref.shape[0], q_ref.shape[1]
    npair = tc // PAIR
    nc = tc // CHUNK

    @pl.when(pl.program_id(0) == 0)
    def _():
        state[...] = jnp.zeros(state.shape, F32)
        vn_s[...] = jnp.zeros(vn_s.shape, BF16)

    ii = lax.broadcasted_iota(jnp.int32, (CHUNK, PAIR), 0)
    lane_w = lax.broadcasted_iota(jnp.int32, (CHUNK, PAIR), 1)
    jj = lane_w % CHUNK
    left_w = lane_w < CHUNK
    left_p = lax.broadcasted_iota(jnp.int32, (PAIR, PAIR), 1) < CHUNK
    top_col = lax.broadcasted_iota(jnp.int32, (PAIR, 1), 0) < CHUNK
    eye = (ii == jj).astype(F32)
    lower = ii >= jj
    strict = ii > jj

    def widen(m):
        return jnp.where(left_w, m[0:CHUNK], m[CHUNK:PAIR])

    def bdiag(wd):
        z = jnp.zeros_like(wd)
        return jnp.concatenate([jnp.where(left_w, wd, z), jnp.where(left_w, z, wd)], axis=0)

    def mm(parts, rhs):
        n = len(parts)
        b_hi, b_lo = bdiag(rhs[0]), bdiag(rhs[1])
        his = [hi_ for hi_, _ in parts]
        r_hi = _dot(jnp.concatenate(his + [lo_ for _, lo_ in parts], axis=0), b_hi)
        r_lo = _dot(jnp.concatenate(his, axis=0) if n > 1 else his[0], b_lo)
        blk = lambda r, i: r[i * CHUNK:(i + 1) * CHUNK]
        return [blk(r_hi, i) + blk(r_hi, n + i) + blk(r_lo, i) for i in range(n)]

    items = [(bb, hh, p) for bb in range(nb) for hh in range(GDN_HEADS) for p in range(npair)]
    pre = []
    for bb, hh, p in items:
        cs = slice(hh * LANES, (hh + 1) * LANES)
        rs = slice(p * PAIR, (p + 1) * PAIR)
        q2, k2 = q_ref[bb, rs, cs], k_ref[bb, rs, cs]
        gcol2 = bg_ref[bb, rs, GDN_HEADS + hh:GDN_HEADS + hh + 1]
        bcol_w = widen(jnp.broadcast_to(bg_ref[bb, rs, hh:hh + 1], (PAIR, PAIR)))
        gcol_w = widen(jnp.broadcast_to(gcol2, (PAIR, PAIR)))
        grow = bgt_ref[bb, GDN_HEADS + hh:GDN_HEADS + hh + 1, rs]
        dec = jnp.where(lower, jnp.exp(jnp.minimum(gcol_w - grow, 0.0)), 0.0)
        gram = _dot_nt(jnp.concatenate([k2, q2], axis=0), k2)
        xm = jnp.where(strict, -(bcol_w * widen(gram[0:PAIR]) * dec), 0.0)
        a_w = jnp.where(lower, widen(gram[PAIR:2 * PAIR]) * dec, 0.0)
        pre.append((xm, a_w))

    def decayed_operands(item, a_w):
        bb, hh, p = item
        cs = slice(hh * LANES, (hh + 1) * LANES)
        rs = slice(p * PAIR, (p + 1) * PAIR)
        q2, k2 = q_ref[bb, rs, cs], k_ref[bb, rs, cs]
        gcol2 = bg_ref[bb, rs, GDN_HEADS + hh:GDN_HEADS + hh + 1]
        grow = bgt_ref[bb, GDN_HEADS + hh:GDN_HEADS + hh + 1, rs]
        g_end = jnp.where(top_col, grow[:, CHUNK - 1:CHUNK], grow[:, PAIR - 1:PAIR])
        e_g = jnp.exp(gcol2)
        k2f = k2.astype(F32)
        qd = (q2.astype(F32) * e_g).astype(BF16)
        kdt = (k2f * jnp.exp(g_end - gcol2)).T
        a_bd = bdiag(a_w)
        kva_s[bb, hh, 2 * p] = jnp.concatenate(
            [jnp.where(left_p, kdt, 0.0), a_bd[0:CHUNK]], axis=0).astype(BF16)
        kva_s[bb, hh, 2 * p + 1] = jnp.concatenate(
            [jnp.where(left_p, 0.0, kdt), a_bd[CHUNK:PAIR]], axis=0).astype(BF16)
        return qd, (k2f * e_g).astype(BF16)

    levels = 4
    per_level = -(-len(items) // levels)
    side = []
    ts = [eye + xm for xm, _ in pre]
    xs = [_split(xm) for xm, _ in pre]
    ps = [_split(mm([xp], xp)[0]) for xp in xs]
    for lvl in range(levels):
        both = [mm([_split(t), pp], pp) for t, pp in zip(ts, ps)]
        for idx in range(lvl * per_level, min((lvl + 1) * per_level, len(items))):
            side.append(decayed_operands(items[idx], pre[idx][1]))
        ts = [t + tp for t, (tp, _) in zip(ts, both)]
        ps = [_split(sq) for _, sq in both]
    ts = [t + mm([_split(t)], pp)[0] for t, pp in zip(ts, ps)]
    for (bb, hh, p), t, (qd, kg) in zip(items, ts, side):
        cs = slice(hh * LANES, (hh + 1) * LANES)
        rs = slice(p * PAIR, (p + 1) * PAIR)
        tb = bdiag(t * bgt_ref[bb, hh:hh + 1, rs]).astype(BF16)
        uw = _dot(tb, jnp.concatenate([v_ref[bb, rs, cs], kg], axis=1))
        u_s[bb, hh, rs, :] = uw[:, 0:GDN_DV]
        w2 = uw[:, GDN_DV:GDN_DV + GDN_DK].astype(BF16)
        for e in range(2):
            es = slice(e * CHUNK, (e + 1) * CHUNK)
            wq_s[bb, hh, 2 * p + e] = jnp.concatenate([w2[es], qd[es]], axis=0)

    chains = [(bb, hh) for bb in range(nb) for hh in range(GDN_HEADS)]
    for c in range(nc):
        e = c % 2
        rs = slice(c * CHUNK, (c + 1) * CHUNK)
        es = slice(e * CHUNK, (e + 1) * CHUNK)
        sts = [state[ch] for ch in chains]
        rr = [_dot(wq_s[bb, hh, c], st.astype(BF16)) for (bb, hh), st in zip(chains, sts)]
        for (bb, hh), r in zip(chains, rr):
            vn_s[bb, hh, es, :] = (u_s[bb, hh, rs, :] - r[0:CHUNK]).astype(BF16)
        for (bb, hh), st, r in zip(chains, sts, rr):
            cs = slice(hh * LANES, (hh + 1) * LANES)
            g_end = bgt_ref[bb, GDN_HEADS + hh:GDN_HEADS + hh + 1, (c + 1) * CHUNK - 1:(c + 1) * CHUNK]
            kva = _dot(kva_s[bb, hh, c], vn_s[bb, hh])
            state[bb, hh] = st * jnp.exp(g_end) + kva[0:GDN_DK]
            o = r[CHUNK:PAIR] + kva[GDN_DK:GDN_DK + CHUNK]
            on = o * lax.rsqrt(jnp.mean(o * o, axis=-1, keepdims=True) + NORM_EPS) * ng_ref[...]
            o_ref[bb, rs, cs] = (on * gate_ref[bb, rs, cs].astype(F32)).astype(BF16)


def _gdn(gq, gk, gv, agate, bg, bgt, ng_row):
    b, s, _ = gq.shape
    tc = GDN_TC
    row = lambda si: (0, si, 0)
    hw_spec = pl.BlockSpec((b, tc, HEADW), row)
    per_chain = (b, GDN_HEADS)
    return pl.pallas_call(
        _gdn_kernel,
        grid=(s // tc,),
        in_specs=[hw_spec, hw_spec, hw_spec, hw_spec,
                  pl.BlockSpec((b, tc, LANES), row),
                  pl.BlockSpec((b, 8, tc), lambda si: (0, 0, si)),
                  pl.BlockSpec((1, GDN_DV), lambda si: (0, 0))],
        out_specs=hw_spec,
        out_shape=jax.ShapeDtypeStruct((b, s, HEADW), BF16),
        scratch_shapes=[pltpu.VMEM(per_chain + (GDN_DK, GDN_DV), F32),
                        pltpu.VMEM(per_chain + (tc // CHUNK, PAIR, GDN_DK), BF16),
                        pltpu.VMEM(per_chain + (tc, GDN_DV), F32),
                        pltpu.VMEM(per_chain + (tc // CHUNK, GDN_DK + CHUNK, PAIR), BF16),
                        pltpu.VMEM(per_chain + (PAIR, GDN_DV), BF16)],
        compiler_params=pltpu.CompilerParams(
            dimension_semantics=("arbitrary",), vmem_limit_bytes=VMEM_LIMIT),
        name="gdn",
    )(gq, gk, gv, agate, bg, bgt, ng_row)


def _attn_kernel(q_ref, k_ref, v_ref, gate_ref, lamp_ref, sg_ref, gqk_ref, o_ref,
                 vext, acc, pbuf, kaug_s, *, lambda_init, tq):
    s_len = q_ref.shape[0]
    nq = s_len // tq
    tk = tq
    nch = tq // CHUNK
    w = 2 * DIFF_D

    vext[:, 0:w] = v_ref[...]
    vext[:, w:2 * w] = jnp.ones((s_len, w), BF16)

    @pl.when(jnp.logical_and(pl.program_id(0) == 0, pl.program_id(1) == 0))
    def _():
        acc[...] = jnp.zeros(acc.shape, F32)

    lane = lax.broadcasted_iota(jnp.int32, (tk, w), 1)
    rchunk = lax.broadcasted_iota(jnp.int32, (tk, w), 0) // CHUNK
    lo = lane < DIFF_D
    hi = jnp.logical_not(lo)
    half = (lo, hi)

    gqk = gqk_ref[...]
    bound = (jnp.max(jnp.abs(gqk[0:1, :])) * jnp.max(jnp.abs(gqk[1:2, :]))
             * (DIFF_D * DIFF_D ** -0.5 * math.log2(math.e) * 1.01))

    lp = lamp_ref[...]
    lam = (jnp.exp(jnp.sum(lp[0:1, :] * lp[1:2, :], axis=-1, keepdims=True))
           - jnp.exp(jnp.sum(lp[2:3, :] * lp[3:4, :], axis=-1, keepdims=True)) + lambda_init)

    def rows_of(blk):
        return pl.ds(pl.multiple_of(blk * tq, tq), tq)

    def write_rows(qi, a0, a1):
        o = a0[:, 0:w] / a0[:, w:2 * w] - lam * (a1[:, 0:w] / a1[:, w:2 * w])
        on = o * lax.rsqrt(jnp.mean(o * o, axis=-1, keepdims=True) + NORM_EPS) * sg_ref[...]
        on = on * (1.0 - lambda_init)
        o_ref[rows_of(qi), :] = (on * gate_ref[rows_of(qi), :].astype(F32)).astype(BF16)


    @pl.when(bound <= MAX_STATIC_SHIFT)
    def _():
        q_aug = []
        for sub, base in enumerate((DIFF_D, 0)):
            mlane = lane - (base + 1)
            in_mask = (mlane >= 0) & (mlane < nch)
            q_aug.append(jnp.where((lane == base) | (in_mask & (rchunk == mlane)), 1.0, 0.0).astype(BF16))
            plain = jnp.where(lane == base, -bound, 0.0)
            kaug_s[0, sub] = plain.astype(BF16)
            kaug_s[1, sub] = jnp.where(in_mask & (rchunk > mlane), -MASK_BIG, plain).astype(BF16)

        def scores(qi, j, pslot):
            q = q_ref[rows_of(qi), :]
            kb = k_ref[rows_of(j), :]
            is_diag = jnp.asarray(j == qi, jnp.int32)
            for sub in range(2):
                qs = jnp.where(half[sub], q, q_aug[sub])
                ks = jnp.where(half[sub], kb, kaug_s[is_diag, sub])
                pbuf[pslot, sub] = jnp.exp2(_dot_nt(qs, ks)).astype(BF16)

        def accumulate(qi, j, pslot):
            ve = vext[rows_of(j), :]
            keep = jnp.where(j == 0, 0.0, 1.0)
            for sub in range(2):
                acc[qi, sub] = acc[qi, sub] * keep + _dot(pbuf[pslot, sub], ve)

        def advance(st):
            qi, j, pqi, pj, n = st
            accumulate(pqi, pj, n & 1)
            scores(qi, j, (n + 1) & 1)
            wrap = j >= qi
            return (jnp.where(wrap, qi + 1, qi), jnp.where(wrap, 0, j + 1), qi, j, n + 1)

        todo = nq * (nq + 1) // 2 - 1
        unroll = max([u for u in range(ATT_UNROLL // 2, 2 * ATT_UNROLL) if todo % u == 0] or [ATT_UNROLL],
                     key=lambda u: -abs(u - ATT_UNROLL))

        def trip(t, carry):
            st, done = carry
            complete = st[2]
            for u in range(WRITES_PER_TRIP):
                qw = jnp.clip(done + u, 0, jnp.maximum(complete - 1, 0))
                write_rows(qw, acc[qw, 0], acc[qw, 1])
            done = jnp.minimum(done + WRITES_PER_TRIP, complete)
            for _ in range(unroll):
                st = advance(st)
            return st, done

        scores(0, 0, 0)
        one, zero = jnp.int32(1), jnp.int32(0)
        trips, rest = divmod(todo, unroll)
        st, done = lax.fori_loop(0, trips, trip, ((one, zero, zero, zero, zero), zero))
        for _ in range(rest):
            st = advance(st)
        accumulate(st[2], st[3], st[4] & 1)

        def write_block(qi, carry):
            write_rows(qi, acc[qi, 0], acc[qi, 1])
            return carry

        lax.fori_loop(done, nq, write_block, 0)

    @pl.when(bound > MAX_STATIC_SHIFT)
    def _():
        zero = jnp.zeros((tk, w), BF16)
        rr = lax.broadcasted_iota(jnp.int32, (tq, tk), 0) // CHUNK
        cc = lax.broadcasted_iota(jnp.int32, (tq, tk), 1) // CHUNK
        causal = cc <= rr

        def outer(qi, carry):
            q = q_ref[rows_of(qi), :]

            def step(j, ms, diag):
                kb = k_ref[rows_of(j), :]
                ve = vext[rows_of(j), :]
                new = []
                for sub in range(2):
                    s = _dot_nt(q, jnp.where(half[sub], kb, zero))
                    if diag:
                        s = jnp.where(causal, s, NEG_BIG)
                    m_new = jnp.maximum(ms[sub], jnp.max(s, axis=-1, keepdims=True))
                    p = jnp.exp2(s - m_new)
                    acc[qi, sub] = jnp.exp2(ms[sub] - m_new) * acc[qi, sub] + _dot(p.astype(BF16), ve)
                    new.append(m_new)
                return tuple(new)

            init = (jnp.full((tq, 1), NEG_BIG, F32), jnp.full((tq, 1), NEG_BIG, F32))
            ms = lax.fori_loop(0, qi, lambda j, c: step(j, c, False), init)
            step(qi, ms, True)
            write_rows(qi, acc[qi, 0], acc[qi, 1])
            return carry

        lax.fori_loop(0, nq, outer, 0)


def _attn(bq, bk, bv, bgate, lam_params, sg_row, gqk_rows, lambda_init):
    b, s, _ = bq.shape
    tq = ATT_TQ
    w = 2 * DIFF_D
    hspec = pl.BlockSpec((None, s, w), lambda bi, hi: (bi, 0, hi))
    const = lambda bi, hi: (0, 0)
    return pl.pallas_call(
        functools.partial(_attn_kernel, lambda_init=lambda_init, tq=tq),
        grid=(b, DIFF_HEADS),
        in_specs=[hspec, hspec, hspec, hspec,
                  pl.BlockSpec((8, LANES), const),
                  pl.BlockSpec((1, w), const),
                  pl.BlockSpec((2, LANES), const)],
        out_specs=hspec,
        out_shape=jax.ShapeDtypeStruct((b, s, HEADW), BF16),
        scratch_shapes=[pltpu.VMEM((s, 2 * w), BF16),
                        pltpu.VMEM((s // tq, 2, tq, 2 * w), F32),
                        pltpu.VMEM((2, 2, tq, tq), BF16),
                        pltpu.VMEM((2, 2, tq, w), BF16)],
        compiler_params=pltpu.CompilerParams(
            dimension_semantics=("arbitrary", "arbitrary"), vmem_limit_bytes=VMEM_LIMIT),
        name="attn",
    )(bq, bk, bv, bgate, lam_params, sg_row, gqk_rows)


def _out_kernel(x_ref, ma_ref, mb_ref, wa_ref, wb_ref, gate_ref, o_ref):
    y = _dot(ma_ref[...], wa_ref[...]) + _dot(mb_ref[...], wb_ref[...])
    o_ref[...] = x_ref[...] + gate_ref[...] * y


def _out(x, mixed_a, mixed_b, w_a, w_b, gate):
    b, s, d = x.shape
    tm = OUT_TM
    row = lambda bi, si: (bi, si, 0)
    const2 = lambda bi, si: (0, 0)
    return pl.pallas_call(
        _out_kernel,
        grid=(b, s // tm),
        in_specs=[pl.BlockSpec((None, tm, d), row),
                  pl.BlockSpec((None, tm, HEADW), row),
                  pl.BlockSpec((None, tm, HEADW), row),
                  pl.BlockSpec((HEADW, d), const2),
                  pl.BlockSpec((HEADW, d), const2),
                  pl.BlockSpec((None, 1, d), lambda bi, si: (bi, 0, 0))],
        out_specs=pl.BlockSpec((None, tm, d), row),
        out_shape=jax.ShapeDtypeStruct((b, s, d), F32),
        compiler_params=pltpu.CompilerParams(
            dimension_semantics=("arbitrary", "arbitrary"), vmem_limit_bytes=VMEM_LIMIT),
        name="out",
    )(x, mixed_a, mixed_b, w_a, w_b, gate)


def _constants():
    sub = np.arange(LANES) % DIFF_D
    half = ROT_DIM // 2
    inv_freq = ROPE_THETA ** (-np.arange(0, ROT_DIM, 2, dtype=np.float32) / ROT_DIM)
    freq_rows = np.zeros((2 * half, LANES), np.float32)
    freq_rows[:half] = inv_freq[:, None]
    idx_row = np.where(sub < ROT_DIM, sub % half, half).astype(np.int32)[None, :]
    sgn_rows = np.stack([np.where(sub < half, -1.0, 0.0),
                         np.where((sub >= half) & (sub < ROT_DIM), 1.0, 0.0)]).astype(np.float32)
    r = np.arange(GMAT_W)
    lmat = (r[:, None] // CHUNK == r[None, :] // CHUNK) & (r[None, :] <= r[:, None])
    g = np.arange(GMAT_W)
    gmat = np.where(g[:, None] // DIFF_D == g[None, :] // DIFF_D, 1.0 / DIFF_D, 0.0)
    return (jnp.asarray(freq_rows), jnp.asarray(sgn_rows), jnp.asarray(idx_row),
            jnp.asarray(lmat, BF16), jnp.asarray(gmat, BF16))


def _pad_lanes(v, offset):
    return jnp.zeros((1, LANES), F32).at[0, offset:offset + v.shape[0]].set(v.astype(F32))


def _layer(x, c_pad, pos_row, lambda_init, norm_g, w_ada_layers, layer, b_ada, w_in, conv_w, a_log, dt_bias, gdn_norm_g,
           q_norm_g, k_norm_g, lam_q1, lam_k1, lam_q2, lam_k2, subln_g, w_out):
    b, s, d = x.shape
    mod = _ada(c_pad, w_ada_layers, layer, b_ada[None, :])[:b]
    shift = mod[:, None, 0:d]
    scale = mod[:, None, d:2 * d]
    gate = mod[:, None, 2 * d:3 * d]

    nk = GDN_HEADS * GDN_DK
    o_beta = 2 * nk + GDN_HEADS * GDN_DV
    o_agate = o_beta + 2 * GDN_HEADS
    assert (OFF_AGATE, OFF_BD) == (o_beta, o_beta + w_in.shape[1] - o_agate)
    bd = jnp.pad(w_in[:, o_beta:o_agate], ((0, 0), (0, LANES - 2 * GDN_HEADS)))
    w_all = jnp.concatenate([w_in[:, 0:o_beta], w_in[:, o_agate:], bd], axis=1).astype(BF16)
    freq_rows, sgn_rows, idx_row, lmat, gmat = _constants()
    lam_params = jnp.zeros((8, LANES), F32)
    for r, v in enumerate((lam_q1, lam_k1, lam_q2, lam_k2)):
        lam_params = lam_params.at[r, 0:DIFF_D].set(v.astype(F32))

    gq, gk, gv, agate, bq, bk, bv, bgate, bg, bgt = _proj(
        x, pos_row, shift, scale, norm_g[None, :], w_all, conv_w,
        _pad_lanes(a_log, GDN_HEADS), _pad_lanes(dt_bias, GDN_HEADS),
        jnp.tile(q_norm_g, 2 * DIFF_HEADS)[None, :], jnp.tile(k_norm_g, 2 * DIFF_HEADS)[None, :],
        lmat, gmat, freq_rows, sgn_rows, idx_row)
    mixed_a = _gdn(gq, gk, gv, agate, bg, bgt, gdn_norm_g[None, :])
    gqk_rows = jnp.concatenate([_pad_lanes(q_norm_g, 0), _pad_lanes(k_norm_g, 0)], axis=0)
    mixed_b = _attn(bq, bk, bv, bgate, lam_params, subln_g[None, :], gqk_rows, lambda_init)
    w_o = w_out.astype(BF16)
    return _out(x, mixed_a, mixed_b, w_o[:HEADW], w_o[HEADW:], gate)


def kernel(x, c, positions, norm_g, w_ada, b_ada, w_in, conv_w, a_log, dt_bias, gdn_norm_g, q_norm_g,
           k_norm_g, lambda_q1, lambda_k1, lambda_q2, lambda_k2, subln_g, w_out):
    b, s, d = x.shape
    for tile in (PROJ_TM, GDN_TC, ATT_TQ, OUT_TM):
        assert s % tile == 0, (s, tile)
    assert d == 2 * HEADW and w_in.shape[-1] == W_COLS - LANES + 2 * GDN_HEADS
    c_pad = jnp.pad(c, ((0, -b % 16), (0, 0)))
    pos_row = positions[:, None, :]
    for l in range(norm_g.shape[0]):
        lambda_init = 0.8 - 0.6 * math.exp(-0.3 * l)
        x = _layer(x, c_pad, pos_row, lambda_init, norm_g[l], w_ada, l, b_ada[l], w_in[l], conv_w[l],
                   a_log[l], dt_bias[l], gdn_norm_g[l], q_norm_g[l], k_norm_g[l], lambda_q1[l],
                   lambda_k1[l], lambda_q2[l], lambda_k2[l], subln_g[l], w_out[l])
    return x
```

```python
import functools
import math

import jax
import jax.numpy as jnp
import numpy as np
from jax import lax
from jax.experimental import pallas as pl
from jax.experimental.pallas import tpu as pltpu

F32 = jnp.float32
BF16 = jnp.bfloat16

CHUNK = 64
PAIR = 2 * CHUNK
GDN_HEADS = 4
GDN_DK = 128
GDN_DV = 128
CONV_K = 4
DIFF_HEADS = 4
DIFF_D = 64
ROT_DIM = 16
ROPE_THETA = 500000.0
NORM_EPS = 1e-6
LANES = 128
NEG_BIG = -1e30
MASK_BIG = 256.0

GDN_QKV = 3 * GDN_HEADS * GDN_DK
HEADW = 512
OFF_GQKV = 0
OFF_AGATE = GDN_QKV
OFF_BQ = OFF_AGATE + HEADW
OFF_BK = OFF_BQ + HEADW
OFF_BV = OFF_BK + HEADW
OFF_BGATE = OFF_BV + HEADW
OFF_BD = OFF_BGATE + HEADW
W_COLS = OFF_BD + LANES

PROJ_TM = 512
PROJ_RB = 128
GMAT_W = 256
GDN_TC = 256
ATT_TQ = 512
WRITES_PER_TRIP = 2
ATT_UNROLL = 16
MAX_STATIC_SHIFT = 60.0
OUT_TM = 1024
VMEM_LIMIT = 48 * 1024 * 1024


def _silu(x):
    h = 0.5 * x
    return h + h * jnp.tanh(h)


def _dot(a, b, **kw):
    return jnp.dot(a, b, preferred_element_type=F32, **kw)


def _dot_nt(a, b, **kw):
    return lax.dot_general(a, b, (((1,), (1,)), ((), ())), preferred_element_type=F32, **kw)


def _split(a):
    hi = a.astype(BF16)
    return hi, (a - hi.astype(F32)).astype(BF16)


def _ada_kernel(c_ref, w_ref, b_ref, o_ref):
    a_hi, a_lo = _split(_silu(c_ref[...]))
    w_hi, w_lo = _split(w_ref[...])
    o_ref[...] = _dot(a_hi, w_hi) + _dot(a_hi, w_lo) + _dot(a_lo, w_hi) + b_ref[...]


def _ada(c_pad, w_ada_layers, layer, b_ada):
    rows, d = c_pad.shape
    n = w_ada_layers.shape[2]
    tn = 512
    return pl.pallas_call(
        _ada_kernel,
        grid=(n // tn,),
        in_specs=[pl.BlockSpec((rows, d), lambda j: (0, 0)),
                  pl.BlockSpec((None, d, tn), lambda j: (layer, 0, j)),
                  pl.BlockSpec((1, tn), lambda j: (0, j))],
        out_specs=pl.BlockSpec((rows, tn), lambda j: (0, j)),
        out_shape=jax.ShapeDtypeStruct((rows, n), F32),
        name="ada",
    )(c_pad, w_ada_layers, b_ada)


def _proj_kernel(x_ref, pos_ref, shift_ref, scale_ref, ng_ref, w_ref, convw_ref, alog_ref, dtb_ref,
                 qg_ref, kg_ref, lmat_ref, gmat_ref, freq_ref, sgn_ref, idx_ref,
                 gq_ref, gk_ref, gv_ref, ag_ref, bq_ref, bk_ref, bv_ref, bgate_ref, bg_ref, bgt_ref,
                 cq, ck, cv, z0, z1, h_s, gm_s, tab):
    tm = x_ref.shape[0]
    x = x_ref[...]
    ms = jnp.mean(x * x, axis=-1, keepdims=True)
    gain = ng_ref[...] * (1.0 + scale_ref[...])
    h_s[...] = (x * lax.rsqrt(ms + NORM_EPS) * gain + shift_ref[...]).astype(BF16)

    cbufs = (cq, ck, cv)
    zbuf = (z0, z1)

    @pl.when(pl.program_id(1) == 0)
    def _():
        for cb in cbufs:
            cb[0:8, :] = jnp.zeros((8, HEADW), F32)

    def mm_pieces(dst, row0, off, width):
        def piece(c0):
            c1 = min(c0 + GMAT_W, width)
            dst[row0:row0 + tm, c0:c1] = _dot(h_s[...], w_ref[:, off + c0:off + c1])
        return [functools.partial(piece, c0) for c0 in range(0, width, GMAT_W)]

    def conv_pieces(g, out_ref):
        cb = cbufs[g]

        def piece(hh):
            hs = slice(hh * LANES, (hh + 1) * LANES)
            cs = slice(g * HEADW + hh * LANES, g * HEADW + (hh + 1) * LANES)
            for r0 in range(0, tm, PROJ_RB):
                acc = convw_ref[3:4, cs] * cb[8 + r0:8 + r0 + PROJ_RB, hs]
                for j in range(CONV_K - 1):
                    acc = acc + convw_ref[j:j + 1, cs] * cb[5 + j + r0:5 + j + r0 + PROJ_RB, hs]
                a = _silu(acc)
                if g < 2:
                    a = a * lax.rsqrt(jnp.sum(a * a, axis=-1, keepdims=True) + NORM_EPS)
                    if g == 0:
                        a = a * (GDN_DK ** -0.5)
                out_ref[r0:r0 + PROJ_RB, hs] = a.astype(BF16)
            cb[0:8, hs] = cb[tm:tm + 8, hs]
        return [functools.partial(piece, hh) for hh in range(GDN_HEADS)]

    def gate_pieces(slot, out_ref):
        def piece(r0):
            rs = slice(r0, r0 + PROJ_RB)
            out_ref[rs, :] = _silu(zbuf[slot][rs, :]).astype(BF16)
        return [functools.partial(piece, r0) for r0 in range(0, tm, PROJ_RB)]

    def qk_pieces(slot, g_ref, out_ref, post):
        def piece(c0):
            z = zbuf[slot][:, c0:c0 + GMAT_W]
            gm_s[:, c0:c0 + GMAT_W] = _dot((z * z).astype(BF16), gmat_ref[...])
            for r0 in range(0, tm, PROJ_RB):
                rs = slice(r0, r0 + PROJ_RB)
                for h0 in range(c0, c0 + GMAT_W, LANES):
                    hs = slice(h0, h0 + LANES)
                    ys = zbuf[slot][rs, hs] * lax.rsqrt(gm_s[rs, hs] + NORM_EPS) * g_ref[:, hs]
                    r = (ys * tab[0, rs, :] + pltpu.roll(ys, LANES - ROT_DIM // 2, axis=1) * tab[1, rs, :]
                         + pltpu.roll(ys, ROT_DIM // 2, axis=1) * tab[2, rs, :])
                    if post is not None:
                        r = r * post
                    out_ref[rs, hs] = r.astype(BF16)
        return [functools.partial(piece, c0) for c0 in range(0, HEADW, GMAT_W)]

    def table_pieces():
        def piece(r0):
            rs = slice(r0, r0 + LANES)
            ang = freq_ref[...] * pos_ref[:, rs].astype(F32)
            pad = jnp.zeros((LANES - ang.shape[0], LANES), F32)
            idx = jnp.broadcast_to(idx_ref[...], (LANES, LANES))
            cos_t = jnp.take_along_axis(jnp.concatenate([jnp.cos(ang), pad], axis=0).T, idx, axis=1)
            sin_t = jnp.take_along_axis(jnp.concatenate([jnp.sin(ang), pad], axis=0).T, idx, axis=1)
            tab[0, rs, :] = cos_t
            tab[1, rs, :] = sin_t * sgn_ref[0:1, :]
            tab[2, rs, :] = sin_t * sgn_ref[1:2, :]
        return [functools.partial(piece, r0) for r0 in range(0, tm, LANES)]

    def ep_bd(slot):
        z = zbuf[slot][:, 0:LANES]
        lane = lax.broadcasted_iota(jnp.int32, (1, LANES), 1)
        beta = 1.0 / (1.0 + jnp.exp(-z))
        zz = z + dtb_ref[...]
        softplus = jnp.maximum(zz, 0.0) + jnp.log(1.0 + jnp.exp(-jnp.abs(zz)))
        is_g = (lane >= GDN_HEADS) & (lane < 2 * GDN_HEADS)
        g = jnp.where(is_g, -jnp.exp(alog_ref[...]) * softplus, 0.0)
        g_hi = g.astype(BF16)
        g_lo = (g - g_hi.astype(F32)).astype(BF16)
        gc = jnp.concatenate(
            [_dot(lmat_ref[...], g_hi[r0:r0 + GMAT_W]) + _dot(lmat_ref[...], g_lo[r0:r0 + GMAT_W])
             for r0 in range(0, tm, GMAT_W)], axis=0)
        bg = jnp.where(lane < GDN_HEADS, beta, gc)
        bg_ref[...] = bg
        bgt_ref[...] = bg.T[0:8, :]

    def cast_bv():
        bv_ref[...] = zbuf[0][...].astype(BF16)

    def emit(*lists):
        for k in range(max(len(pl_) for pl_ in lists)):
            for pl_ in lists:
                if k < len(pl_):
                    pl_[k]()

    q_scale = DIFF_D ** -0.5 * math.log2(math.e)
    stages = [
        (mm_pieces(cq, 8, OFF_GQKV, HEADW), conv_pieces(0, gq_ref)),
        (mm_pieces(z1, 0, OFF_BD, LANES) + mm_pieces(ck, 8, OFF_GQKV + HEADW, HEADW),
         conv_pieces(1, gk_ref) + [functools.partial(ep_bd, 1)]),
        (mm_pieces(cv, 8, OFF_GQKV + 2 * HEADW, HEADW), conv_pieces(2, gv_ref)),
        (mm_pieces(z0, 0, OFF_BQ, HEADW), qk_pieces(0, qg_ref, bq_ref, q_scale)),
        (mm_pieces(z1, 0, OFF_BK, HEADW), qk_pieces(1, kg_ref, bk_ref, None)),
        (mm_pieces(z0, 0, OFF_AGATE, HEADW), gate_pieces(0, ag_ref)),
        (mm_pieces(z1, 0, OFF_BGATE, HEADW), gate_pieces(1, bgate_ref)),
        (mm_pieces(z0, 0, OFF_BV, HEADW), [cast_bv]),
    ]
    emit(stages[0][0], table_pieces())
    for i, (_, epilogue) in enumerate(stages):
        emit(stages[i + 1][0] if i + 1 < len(stages) else [], epilogue)


def _proj(x, pos_row, shift, scale, norm_g, w_all, conv_w, alog_row, dtb_row, qg_row, kg_row,
          lmat, gmat, freq_rows, sgn_rows, idx_row):
    b, s, d = x.shape
    tm = PROJ_TM
    row = lambda bi, si: (bi, si, 0)
    const2 = lambda bi, si: (0, 0)
    per_b = lambda bi, si: (bi, 0, 0)
    hw = jax.ShapeDtypeStruct((b, s, HEADW), BF16)
    out_shape = [hw] * 8 + [jax.ShapeDtypeStruct((b, s, LANES), F32),
                            jax.ShapeDtypeStruct((b, 8, s), F32)]
    hw_spec = pl.BlockSpec((None, tm, HEADW), row)
    out_specs = [hw_spec] * 8 + [pl.BlockSpec((None, tm, LANES), row),
                                 pl.BlockSpec((None, 8, tm), lambda bi, si: (bi, 0, si))]
    in_specs = [
        pl.BlockSpec((None, tm, d), row),
        pl.BlockSpec((None, 1, tm), lambda bi, si: (bi, 0, si)),
        pl.BlockSpec((None, 1, d), per_b),
        pl.BlockSpec((None, 1, d), per_b),
        pl.BlockSpec((1, d), const2),
        pl.BlockSpec((d, W_COLS), const2),
        pl.BlockSpec((CONV_K, GDN_QKV), const2),
        pl.BlockSpec((1, LANES), const2),
        pl.BlockSpec((1, LANES), const2),
        pl.BlockSpec((1, HEADW), const2),
        pl.BlockSpec((1, HEADW), const2),
        pl.BlockSpec((GMAT_W, GMAT_W), const2),
        pl.BlockSpec((GMAT_W, GMAT_W), const2),
        pl.BlockSpec((2 * (ROT_DIM // 2), LANES), const2),
        pl.BlockSpec((2, LANES), const2),
        pl.BlockSpec((1, LANES), const2),
    ]
    return pl.pallas_call(
        _proj_kernel,
        grid=(b, s // tm),
        in_specs=in_specs,
        out_specs=out_specs,
        out_shape=out_shape,
        scratch_shapes=[pltpu.VMEM((tm + 8, HEADW), F32)] * 3 + [pltpu.VMEM((tm, HEADW), F32)] * 2
        + [pltpu.VMEM((tm, d), BF16), pltpu.VMEM((tm, HEADW), F32), pltpu.VMEM((3, tm, LANES), F32)],
        compiler_params=pltpu.CompilerParams(
            dimension_semantics=("arbitrary", "arbitrary"), vmem_limit_bytes=VMEM_LIMIT),
        name="proj",
    )(x, pos_row, shift, scale, norm_g, w_all, conv_w, alog_row, dtb_row, qg_row, kg_row,
      lmat, gmat, freq_rows, sgn_rows, idx_row)


def _gdn_kernel(q_ref, k_ref, v_ref, gate_ref, bg_ref, bgt_ref, ng_ref, o_ref,
                state, wq_s, u_s, kva_s, vn_s):
    nb, tc = q_ref.shape[0], q_ref.shape[1]
    npair = tc // PAIR
    nc = tc // CHUNK

    @pl.when(pl.program_id(0) == 0)
    def _():
        state[...] = jnp.zeros(state.shape, F32)
        vn_s[...] = jnp.zeros(vn_s.shape, BF16)

    ii = lax.broadcasted_iota(jnp.int32, (CHUNK, PAIR), 0)
    lane_w = lax.broadcasted_iota(jnp.int32, (CHUNK, PAIR), 1)
    jj = lane_w % CHUNK
    left_w = lane_w < CHUNK
    left_p = lax.broadcasted_iota(jnp.int32, (PAIR, PAIR), 1) < CHUNK
    top_col = lax.broadcasted_iota(jnp.int32, (PAIR, 1), 0) < CHUNK
    eye = (ii == jj).astype(F32)
    lower = ii >= jj
    strict = ii > jj

    def widen(m):
        return jnp.where(left_w, m[0:CHUNK], m[CHUNK:PAIR])

    def bdiag(wd):
        z = jnp.zeros_like(wd)
        return jnp.concatenate([jnp.where(left_w, wd, z), jnp.where(left_w, z, wd)], axis=0)

    def mm(parts, rhs):
        n = len(parts)
        b_hi, b_lo = bdiag(rhs[0]), bdiag(rhs[1])
        his = [hi_ for hi_, _ in parts]
        r_hi = _dot(jnp.concatenate(his + [lo_ for _, lo_ in parts], axis=0), b_hi)
        r_lo = _dot(jnp.concatenate(his, axis=0) if n > 1 else his[0], b_lo)
        blk = lambda r, i: r[i * CHUNK:(i + 1) * CHUNK]
        return [blk(r_hi, i) + blk(r_hi, n + i) + blk(r_lo, i) for i in range(n)]

    items = [(bb, hh, p) for bb in range(nb) for hh in range(GDN_HEADS) for p in range(npair)]
    pre = []
    for bb, hh, p in items:
        cs = slice(hh * LANES, (hh + 1) * LANES)
        rs = slice(p * PAIR, (p + 1) * PAIR)
        q2, k2 = q_ref[bb, rs, cs], k_ref[bb, rs, cs]
        gcol2 = bg_ref[bb, rs, GDN_HEADS + hh:GDN_HEADS + hh + 1]
        bcol_w = widen(jnp.broadcast_to(bg_ref[bb, rs, hh:hh + 1], (PAIR, PAIR)))
        gcol_w = widen(jnp.broadcast_to(gcol2, (PAIR, PAIR)))
        grow = bgt_ref[bb, GDN_HEADS + hh:GDN_HEADS + hh + 1, rs]
        dec = jnp.where(lower, jnp.exp(jnp.minimum(gcol_w - grow, 0.0)), 0.0)
        gram = _dot_nt(jnp.concatenate([k2, q2], axis=0), k2)
        xm = jnp.where(strict, -(bcol_w * widen(gram[0:PAIR]) * dec), 0.0)
        a_w = jnp.where(lower, widen(gram[PAIR:2 * PAIR]) * dec, 0.0)
        pre.append((xm, a_w))

    def decayed_operands(item, a_w):
        bb, hh, p = item
        cs = slice(hh * LANES, (hh + 1) * LANES)
        rs = slice(p * PAIR, (p + 1) * PAIR)
        q2, k2 = q_ref[bb, rs, cs], k_ref[bb, rs, cs]
        gcol2 = bg_ref[bb, rs, GDN_HEADS + hh:GDN_HEADS + hh + 1]
        grow = bgt_ref[bb, GDN_HEADS + hh:GDN_HEADS + hh + 1, rs]
        g_end = jnp.where(top_col, grow[:, CHUNK - 1:CHUNK], grow[:, PAIR - 1:PAIR])
        e_g = jnp.exp(gcol2)
        k2f = k2.astype(F32)
        qd = (q2.astype(F32) * e_g).astype(BF16)
        kdt = (k2f * jnp.exp(g_end - gcol2)).T
        a_bd = bdiag(a_w)
        kva_s[bb, hh, 2 * p] = jnp.concatenate(
            [jnp.where(left_p, kdt, 0.0), a_bd[0:CHUNK]], axis=0).astype(BF16)
        kva_s[bb, hh, 2 * p + 1] = jnp.concatenate(
            [jnp.where(left_p, 0.0, kdt), a_bd[CHUNK:PAIR]], axis=0).astype(BF16)
        return qd, (k2f * e_g).astype(BF16)

    levels = 4
    per_level = -(-len(items) // levels)
    side = []
    ts = [eye + xm for xm, _ in pre]
    xs = [_split(xm) for xm, _ in pre]
    ps = [_split(mm([xp], xp)[0]) for xp in xs]
    for lvl in range(levels):
        both = [mm([_split(t), pp], pp) for t, pp in zip(ts, ps)]
        for idx in range(lvl * per_level, min((lvl + 1) * per_level, len(items))):
            side.append(decayed_operands(items[idx], pre[idx][1]))
        ts = [t + tp for t, (tp, _) in zip(ts, both)]
        ps = [_split(sq) for _, sq in both]
    ts = [t + mm([_split(t)], pp)[0] for t, pp in zip(ts, ps)]
    for (bb, hh, p), t, (qd, kg) in zip(items, ts, side):
        cs = slice(hh * LANES, (hh + 1) * LANES)
        rs = slice(p * PAIR, (p + 1) * PAIR)
        tb = bdiag(t * bgt_ref[bb, hh:hh + 1, rs]).astype(BF16)
        uw = _dot(tb, jnp.concatenate([v_ref[bb, rs, cs], kg], axis=1))
        u_s[bb, hh, rs, :] = uw[:, 0:GDN_DV]
        w2 = uw[:, GDN_DV:GDN_DV + GDN_DK].astype(BF16)
        for e in range(2):
            es = slice(e * CHUNK, (e + 1) * CHUNK)
            wq_s[bb, hh, 2 * p + e] = jnp.concatenate([w2[es], qd[es]], axis=0)

    chains = [(bb, hh) for bb in range(nb) for hh in range(GDN_HEADS)]
    for c in range(nc):
        e = c % 2
        rs = slice(c * CHUNK, (c + 1) * CHUNK)
        es = slice(e * CHUNK, (e + 1) * CHUNK)
        sts = [state[ch] for ch in chains]
        rr = [_dot(wq_s[bb, hh, c], st.astype(BF16)) for (bb, hh), st in zip(chains, sts)]
        for (bb, hh), r in zip(chains, rr):
            vn_s[bb, hh, es, :] = (u_s[bb, hh, rs, :] - r[0:CHUNK]).astype(BF16)
        for (bb, hh), st, r in zip(chains, sts, rr):
            cs = slice(hh * LANES, (hh + 1) * LANES)
            g_end = bgt_ref[bb, GDN_HEADS + hh:GDN_HEADS + hh + 1, (c + 1) * CHUNK - 1:(c + 1) * CHUNK]
            kva = _dot(kva_s[bb, hh, c], vn_s[bb, hh])
            state[bb, hh] = st * jnp.exp(g_end) + kva[0:GDN_DK]
            o = r[CHUNK:PAIR] + kva[GDN_DK:GDN_DK + CHUNK]
            on = o * lax.rsqrt(jnp.mean(o * o, axis=-1, keepdims=True) + NORM_EPS) * ng_ref[...]
            o_ref[bb, rs, cs] = (on * gate_ref[bb, rs, cs].astype(F32)).astype(BF16)


def _gdn(gq, gk, gv, agate, bg, bgt, ng_row):
    b, s, _ = gq.shape
    tc = GDN_TC
    row = lambda si: (0, si, 0)
    hw_spec = pl.BlockSpec((b, tc, HEADW), row)
    per_chain = (b, GDN_HEADS)
    return pl.pallas_call(
        _gdn_kernel,
        grid=(s // tc,),
        in_specs=[hw_spec, hw_spec, hw_spec, hw_spec,
                  pl.BlockSpec((b, tc, LANES), row),
                  pl.BlockSpec((b, 8, tc), lambda si: (0, 0, si)),
                  pl.BlockSpec((1, GDN_DV), lambda si: (0, 0))],
        out_specs=hw_spec,
        out_shape=jax.ShapeDtypeStruct((b, s, HEADW), BF16),
        scratch_shapes=[pltpu.VMEM(per_chain + (GDN_DK, GDN_DV), F32),
                        pltpu.VMEM(per_chain + (tc // CHUNK, PAIR, GDN_DK), BF16),
                        pltpu.VMEM(per_chain + (tc, GDN_DV), F32),
                        pltpu.VMEM(per_chain + (tc // CHUNK, GDN_DK + CHUNK, PAIR), BF16),
                        pltpu.VMEM(per_chain + (PAIR, GDN_DV), BF16)],
        compiler_params=pltpu.CompilerParams(
            dimension_semantics=("arbitrary",), vmem_limit_bytes=VMEM_LIMIT),
        name="gdn",
    )(gq, gk, gv, agate, bg, bgt, ng_row)


def _attn_kernel(q_ref, k_ref, v_ref, gate_ref, lamp_ref, sg_ref, gqk_ref, o_ref,
                 vext, acc, pbuf, kaug_s, *, lambda_init, tq):
    s_len = q_ref.shape[0]
    nq = s_len // tq
    tk = tq
    nch = tq // CHUNK
    w = 2 * DIFF_D

    vext[:, 0:w] = v_ref[...]
    vext[:, w:2 * w] = jnp.ones((s_len, w), BF16)

    @pl.when(jnp.logical_and(pl.program_id(0) == 0, pl.program_id(1) == 0))
    def _():
        acc[...] = jnp.zeros(acc.shape, F32)

    lane = lax.broadcasted_iota(jnp.int32, (tk, w), 1)
    rchunk = lax.broadcasted_iota(jnp.int32, (tk, w), 0) // CHUNK
    lo = lane < DIFF_D
    hi = jnp.logical_not(lo)
    half = (lo, hi)

    gqk = gqk_ref[...]
    bound = (jnp.max(jnp.abs(gqk[0:1, :])) * jnp.max(jnp.abs(gqk[1:2, :]))
             * (DIFF_D * DIFF_D ** -0.5 * math.log2(math.e) * 1.01))

    lp = lamp_ref[...]
    lam = (jnp.exp(jnp.sum(lp[0:1, :] * lp[1:2, :], axis=-1, keepdims=True))
           - jnp.exp(jnp.sum(lp[2:3, :] * lp[3:4, :], axis=-1, keepdims=True)) + lambda_init)

    def rows_of(blk):
        return pl.ds(pl.multiple_of(blk * tq, tq), tq)

    def write_rows(qi, a0, a1):
        o = a0[:, 0:w] / a0[:, w:2 * w] - lam * (a1[:, 0:w] / a1[:, w:2 * w])
        on = o * lax.rsqrt(jnp.mean(o * o, axis=-1, keepdims=True) + NORM_EPS) * sg_ref[...]
        on = on * (1.0 - lambda_init)
        o_ref[rows_of(qi), :] = (on * gate_ref[rows_of(qi), :].astype(F32)).astype(BF16)


    @pl.when(bound <= MAX_STATIC_SHIFT)
    def _():
        q_aug = []
        for sub, base in enumerate((DIFF_D, 0)):
            mlane = lane - (base + 1)
            in_mask = (mlane >= 0) & (mlane < nch)
            q_aug.append(jnp.where((lane == base) | (in_mask & (rchunk == mlane)), 1.0, 0.0).astype(BF16))
            plain = jnp.where(lane == base, -bound, 0.0)
            kaug_s[0, sub] = plain.astype(BF16)
            kaug_s[1, sub] = jnp.where(in_mask & (rchunk > mlane), -MASK_BIG, plain).astype(BF16)

        def scores(qi, j, pslot):
            q = q_ref[rows_of(qi), :]
            kb = k_ref[rows_of(j), :]
            is_diag = jnp.asarray(j == qi, jnp.int32)
            for sub in range(2):
                qs = jnp.where(half[sub], q, q_aug[sub])
                ks = jnp.where(half[sub], kb, kaug_s[is_diag, sub])
                pbuf[pslot, sub] = jnp.exp2(_dot_nt(qs, ks)).astype(BF16)

        def accumulate(qi, j, pslot):
            ve = vext[rows_of(j), :]
            keep = jnp.where(j == 0, 0.0, 1.0)
            for sub in range(2):
                acc[qi, sub] = acc[qi, sub] * keep + _dot(pbuf[pslot, sub], ve)

        def advance(st):
            qi, j, pqi, pj, n = st
            accumulate(pqi, pj, n & 1)
            scores(qi, j, (n + 1) & 1)
            wrap = j >= qi
            return (jnp.where(wrap, qi + 1, qi), jnp.where(wrap, 0, j + 1), qi, j, n + 1)

        todo = nq * (nq + 1) // 2 - 1
        unroll = max([u for u in range(ATT_UNROLL // 2, 2 * ATT_UNROLL) if todo % u == 0] or [ATT_UNROLL],
                     key=lambda u: -abs(u - ATT_UNROLL))

        def trip(t, carry):
            st, done = carry
            complete = st[2]
            for u in range(WRITES_PER_TRIP):
                qw = jnp.clip(done + u, 0, jnp.maximum(complete - 1, 0))
                write_rows(qw, acc[qw, 0], acc[qw, 1])
            done = jnp.minimum(done + WRITES_PER_TRIP, complete)
            for _ in range(unroll):
                st = advance(st)
            return st, done

        scores(0, 0, 0)
        one, zero = jnp.int32(1), jnp.int32(0)
        trips, rest = divmod(todo, unroll)
        st, done = lax.fori_loop(0, trips, trip, ((one, zero, zero, zero, zero), zero))
        for _ in range(rest):
            st = advance(st)
        accumulate(st[2], st[3], st[4] & 1)

        def write_block(qi, carry):
            write_rows(qi, acc[qi, 0], acc[qi, 1])
            return carry

        lax.fori_loop(done, nq, write_block, 0)

    @pl.when(bound > MAX_STATIC_SHIFT)
    def _():
        zero = jnp.zeros((tk, w), BF16)
        rr = lax.broadcasted_iota(jnp.int32, (tq, tk), 0) // CHUNK
        cc = lax.broadcasted_iota(jnp.int32, (tq, tk), 1) // CHUNK
        causal = cc <= rr

        def outer(qi, carry):
            q = q_ref[rows_of(qi), :]

            def step(j, ms, diag):
                kb = k_ref[rows_of(j), :]
                ve = vext[rows_of(j), :]
                new = []
                for sub in range(2):
                    s = _dot_nt(q, jnp.where(half[sub], kb, zero))
                    if diag:
                        s = jnp.where(causal, s, NEG_BIG)
                    m_new = jnp.maximum(ms[sub], jnp.max(s, axis=-1, keepdims=True))
                    p = jnp.exp2(s - m_new)
                    acc[qi, sub] = jnp.exp2(ms[sub] - m_new) * acc[qi, sub] + _dot(p.astype(BF16), ve)
                    new.append(m_new)
                return tuple(new)

            init = (jnp.full((tq, 1), NEG_BIG, F32), jnp.full((tq, 1), NEG_BIG, F32))
            ms = lax.fori_loop(0, qi, lambda j, c: step(j, c, False), init)
            step(qi, ms, True)
            write_rows(qi, acc[qi, 0], acc[qi, 1])
            return carry

        lax.fori_loop(0, nq, outer, 0)


def _attn(bq, bk, bv, bgate, lam_params, sg_row, gqk_rows, lambda_init):
    b, s, _ = bq.shape
    tq = ATT_TQ
    w = 2 * DIFF_D
    hspec = pl.BlockSpec((None, s, w), lambda bi, hi: (bi, 0, hi))
    const = lambda bi, hi: (0, 0)
    return pl.pallas_call(
        functools.partial(_attn_kernel, lambda_init=lambda_init, tq=tq),
        grid=(b, DIFF_HEADS),
        in_specs=[hspec, hspec, hspec, hspec,
                  pl.BlockSpec((8, LANES), const),
                  pl.BlockSpec((1, w), const),
                  pl.BlockSpec((2, LANES), const)],
        out_specs=hspec,
        out_shape=jax.ShapeDtypeStruct((b, s, HEADW), BF16),
        scratch_shapes=[pltpu.VMEM((s, 2 * w), BF16),
                        pltpu.VMEM((s // tq, 2, tq, 2 * w), F32),
                        pltpu.VMEM((2, 2, tq, tq), BF16),
                        pltpu.VMEM((2, 2, tq, w), BF16)],
        compiler_params=pltpu.CompilerParams(
            dimension_semantics=("arbitrary", "arbitrary"), vmem_limit_bytes=VMEM_LIMIT),
        name="attn",
    )(bq, bk, bv, bgate, lam_params, sg_row, gqk_rows)


def _out(x, mixed_a, mixed_b, w_a, w_b, gate):
    b, s, d = x.shape
    tm = OUT_TM
    row = lambda bi, si: (bi, si, 0)
    deep = pl.Buffered(3)

    def tile(x_ref, ma_ref, mb_ref, gate_ref, o_ref, *, wa_ref, wb_ref):
        y = _dot(ma_ref[0], wa_ref[...]) + _dot(mb_ref[0], wb_ref[...])
        o_ref[0] = x_ref[0] + gate_ref[0] * y

    def outer(x_hbm, ma_hbm, mb_hbm, gate_hbm, wa_ref, wb_ref, o_hbm):
        pltpu.emit_pipeline(
            functools.partial(tile, wa_ref=wa_ref, wb_ref=wb_ref),
            grid=(b, s // tm),
            in_specs=[pl.BlockSpec((1, tm, d), row, pipeline_mode=deep),
                      pl.BlockSpec((1, tm, HEADW), row, pipeline_mode=deep),
                      pl.BlockSpec((1, tm, HEADW), row, pipeline_mode=deep),
                      pl.BlockSpec((1, 1, d), lambda bi, si: (bi, 0, 0))],
            out_specs=[pl.BlockSpec((1, tm, d), row)],
        )(x_hbm, ma_hbm, mb_hbm, gate_hbm, o_hbm)

    any_spec = pl.BlockSpec(memory_space=pl.ANY)
    vmem_spec = pl.BlockSpec(memory_space=pltpu.VMEM)
    return pl.pallas_call(
        outer,
        in_specs=[any_spec, any_spec, any_spec, any_spec, vmem_spec, vmem_spec],
        out_specs=any_spec,
        out_shape=jax.ShapeDtypeStruct((b, s, d), F32),
        compiler_params=pltpu.CompilerParams(vmem_limit_bytes=VMEM_LIMIT),
        name="out",
    )(x, mixed_a, mixed_b, gate, w_a, w_b)


def _constants():
    sub = np.arange(LANES) % DIFF_D
    half = ROT_DIM // 2
    inv_freq = ROPE_THETA ** (-np.arange(0, ROT_DIM, 2, dtype=np.float32) / ROT_DIM)
    freq_rows = np.zeros((2 * half, LANES), np.float32)
    freq_rows[:half] = inv_freq[:, None]
    idx_row = np.where(sub < ROT_DIM, sub % half, half).astype(np.int32)[None, :]
    sgn_rows = np.stack([np.where(sub < half, -1.0, 0.0),
                         np.where((sub >= half) & (sub < ROT_DIM), 1.0, 0.0)]).astype(np.float32)
    r = np.arange(GMAT_W)
    lmat = (r[:, None] // CHUNK == r[None, :] // CHUNK) & (r[None, :] <= r[:, None])
    g = np.arange(GMAT_W)
    gmat = np.where(g[:, None] // DIFF_D == g[None, :] // DIFF_D, 1.0 / DIFF_D, 0.0)
    return (jnp.asarray(freq_rows), jnp.asarray(sgn_rows), jnp.asarray(idx_row),
            jnp.asarray(lmat, BF16), jnp.asarray(gmat, BF16))


def _pad_lanes(v, offset):
    return jnp.zeros((1, LANES), F32).at[0, offset:offset + v.shape[0]].set(v.astype(F32))


def _layer(x, c_pad, pos_row, lambda_init, norm_g, w_ada_layers, layer, b_ada, w_in, conv_w, a_log, dt_bias, gdn_norm_g,
           q_norm_g, k_norm_g, lam_q1, lam_k1, lam_q2, lam_k2, subln_g, w_out):
    b, s, d = x.shape
    mod = _ada(c_pad, w_ada_layers, layer, b_ada[None, :])[:b]
    shift = mod[:, None, 0:d]
    scale = mod[:, None, d:2 * d]
    gate = mod[:, None, 2 * d:3 * d]

    nk = GDN_HEADS * GDN_DK
    o_beta = 2 * nk + GDN_HEADS * GDN_DV
    o_agate = o_beta + 2 * GDN_HEADS
    assert (OFF_AGATE, OFF_BD) == (o_beta, o_beta + w_in.shape[1] - o_agate)
    bd = jnp.pad(w_in[:, o_beta:o_agate], ((0, 0), (0, LANES - 2 * GDN_HEADS)))
    w_all = jnp.concatenate([w_in[:, 0:o_beta], w_in[:, o_agate:], bd], axis=1).astype(BF16)
    freq_rows, sgn_rows, idx_row, lmat, gmat = _constants()
    lam_params = jnp.zeros((8, LANES), F32)
    for r, v in enumerate((lam_q1, lam_k1, lam_q2, lam_k2)):
        lam_params = lam_params.at[r, 0:DIFF_D].set(v.astype(F32))

    gq, gk, gv, agate, bq, bk, bv, bgate, bg, bgt = _proj(
        x, pos_row, shift, scale, norm_g[None, :], w_all, conv_w,
        _pad_lanes(a_log, GDN_HEADS), _pad_lanes(dt_bias, GDN_HEADS),
        jnp.tile(q_norm_g, 2 * DIFF_HEADS)[None, :], jnp.tile(k_norm_g, 2 * DIFF_HEADS)[None, :],
        lmat, gmat, freq_rows, sgn_rows, idx_row)
    mixed_a = _gdn(gq, gk, gv, agate, bg, bgt, gdn_norm_g[None, :])
    gqk_rows = jnp.concatenate([_pad_lanes(q_norm_g, 0), _pad_lanes(k_norm_g, 0)], axis=0)
    mixed_b = _attn(bq, bk, bv, bgate, lam_params, subln_g[None, :], gqk_rows, lambda_init)
    w_o = w_out.astype(BF16)
    return _out(x, mixed_a, mixed_b, w_o[:HEADW], w_o[HEADW:], gate)


def kernel(x, c, positions, norm_g, w_ada, b_ada, w_in, conv_w, a_log, dt_bias, gdn_norm_g, q_norm_g,
           k_norm_g, lambda_q1, lambda_k1, lambda_q2, lambda_k2, subln_g, w_out):
    b, s, d = x.shape
    for tile in (PROJ_TM, GDN_TC, ATT_TQ, OUT_TM):
        assert s % tile == 0, (s, tile)
    assert d == 2 * HEADW and w_in.shape[-1] == W_COLS - LANES + 2 * GDN_HEADS
    c_pad = jnp.pad(c, ((0, -b % 16), (0, 0)))
    pos_row = positions[:, None, :]
    for l in range(norm_g.shape[0]):
        lambda_init = 0.8 - 0.6 * math.exp(-0.3 * l)
        x = _layer(x, c_pad, pos_row, lambda_init, norm_g[l], w_ada, l, b_ada[l], w_in[l], conv_w[l],
                   a_log[l], dt_bias[l], gdn_norm_g[l], q_norm_g[l], k_norm_g[l], lambda_q1[l],
                   lambda_k1[l], lambda_q2[l], lambda_k2[l], subln_g[l], w_out[l])
    return x
```
